```python
import jax, jax.numpy as jnp
from jax import lax
import numpy as np

D_MODEL = 1024
BATCH = 8
SEQ = 2048
DEPTH = 1

NSA_HEADS = 8
NSA_KV_GROUPS = 2
NSA_HEAD_DIM = 64
CMP_BLOCK = 32
CMP_STRIDE = 16
CMP_HIDDEN = 128
SEL_BLOCK = 64
SEL_TOPK = 8
WINDOW = 512
Q_BLOCK = 128
GLA_HEADS = 4
GLA_DK = 64
GLA_DV = 128
GLA_CHUNK = 64
GLA_LOWRANK = 16
GLA_TAU = 16.0
D_FF = 2816
EPS = 1e-6
NEG_INF = -1e30
FORCE = 1e9

NSA_Q_W = NSA_HEADS * NSA_HEAD_DIM
NSA_KV_W = NSA_KV_GROUPS * NSA_HEAD_DIM
GLA_K_W = GLA_HEADS * GLA_DK
GLA_V_W = GLA_HEADS * GLA_DV
D_MIX = NSA_Q_W + GLA_V_W
IN_SPLITS = (NSA_Q_W, NSA_KV_W, NSA_KV_W, NSA_KV_W, NSA_KV_W, NSA_KV_W, NSA_KV_W, NSA_HEADS * 3, GLA_K_W, GLA_K_W, GLA_V_W, GLA_V_W, GLA_LOWRANK)
D_IN_PROJ = sum(IN_SPLITS)

kernel_name = 'nsa_gla_macaron_hybrid'


def rmsnorm(x, g):
    xf = x.astype(jnp.float32)
    y = xf * lax.rsqrt(jnp.mean(xf * xf, axis=-1, keepdims=True) + EPS)
    return (y * g.astype(jnp.float32)).astype(x.dtype)


def swiglu_ffn(x, w_gate, w_up, w_down):
    return (jax.nn.silu(x @ w_gate) * (x @ w_up)) @ w_down


def split_columns(y, sizes):
    offsets = [int(o) for o in np.cumsum(sizes)[:-1]]
    return jnp.split(y, offsets, axis=-1)


def compress_blocks(kv, pe, w1, w2):
    B, T, G, dh = kv.shape
    n_cmp = (T - CMP_BLOCK) // CMP_STRIDE + 1
    idx = jnp.arange(n_cmp)[:, None] * CMP_STRIDE + jnp.arange(CMP_BLOCK)[None, :]
    blocks = kv[:, idx] + pe[None, None, :, None, :]
    blocks = blocks.transpose(0, 1, 3, 2, 4).reshape(B, n_cmp, G, CMP_BLOCK * dh)
    return jax.nn.gelu(blocks @ w1) @ w2


def nsa_attention(q, k_cmp, v_cmp, k_slc, v_slc, k_win, v_win, gates, pe_k, w1_k, w2_k, pe_v, w1_v, w2_v):
    B, T, H, dh = q.shape
    G = NSA_KV_GROUPS
    R = H // G
    f32 = jnp.float32
    scale = dh ** -0.5
    slopes = jnp.exp2(-8.0 * jnp.arange(1, H + 1, dtype=f32) / H).reshape(G, R)

    kc = compress_blocks(k_cmp, pe_k, w1_k, w2_k)
    vc = compress_blocks(v_cmp, pe_v, w1_v, w2_v)
    n_cmp = kc.shape[1]
    cmp_start = jnp.arange(n_cmp, dtype=jnp.int32) * CMP_STRIDE
    cmp_end = cmp_start + CMP_BLOCK - 1
    cmp_center = cmp_start.astype(f32) + (CMP_BLOCK - 1) / 2.0

    n_sel_blocks = T // SEL_BLOCK
    top_n = min(SEL_TOPK, n_sel_blocks)
    sel_start = jnp.arange(n_sel_blocks, dtype=jnp.int32) * SEL_BLOCK
    overlap = ((cmp_start[:, None] < sel_start[None, :] + SEL_BLOCK)
               & (cmp_start[:, None] + CMP_BLOCK > sel_start[None, :])).astype(f32)
    k_blk = k_slc.reshape(B, n_sel_blocks, SEL_BLOCK, G, dh).transpose(0, 3, 1, 2, 4)
    v_blk = v_slc.reshape(B, n_sel_blocks, SEL_BLOCK, G, dh).transpose(0, 3, 1, 2, 4)

    k_pad = jnp.pad(k_win, ((0, 0), (WINDOW, 0), (0, 0), (0, 0)))
    v_pad = jnp.pad(v_win, ((0, 0), (WINDOW, 0), (0, 0), (0, 0)))

    n_qb = T // Q_BLOCK
    q_blocks = q.reshape(B, n_qb, Q_BLOCK, G, R, dh).transpose(1, 0, 2, 3, 4, 5)
    g_blocks = gates.reshape(B, n_qb, Q_BLOCK, G, R, 3).transpose(1, 0, 2, 3, 4, 5)
    b_ix = jnp.arange(B)[:, None, None, None]
    g_ix = jnp.arange(G)[None, :, None, None]
    win_off = jnp.arange(WINDOW + Q_BLOCK, dtype=jnp.int32) - WINDOW
    sel_off = jnp.arange(SEL_BLOCK, dtype=jnp.int32)
    j_blk = jnp.arange(n_sel_blocks, dtype=jnp.int32)

    def block_fn(args):
        qb, gb, blk = args
        t = blk * Q_BLOCK + jnp.arange(Q_BLOCK, dtype=jnp.int32)
        tf = t.astype(f32)
        qs = qb * scale

        s = jnp.einsum('bqgrd,bngd->bgrqn', qs, kc).astype(f32)
        s = s - slopes[:, :, None, None] * (tf[:, None] - cmp_center[None, :])
        valid_c = cmp_end[None, :] <= t[:, None]
        s = jnp.where(valid_c, s, NEG_INF)
        p_c = jax.nn.softmax(s, axis=-1) * valid_c
        o_cmp = jnp.einsum('bgrqn,bngd->bqgrd', p_c.astype(vc.dtype), vc)

        imp = jnp.einsum('bgrqn,ns->bgqs', p_c, overlap)
        cur = t // SEL_BLOCK
        forced = (j_blk[None, :] == 0) | (j_blk[None, :] == cur[:, None]) | (j_blk[None, :] == cur[:, None] - 1)
        causal_blk = j_blk[None, :] <= cur[:, None]
        imp = jnp.where(forced, FORCE, imp)
        imp = jnp.where(causal_blk, imp, -FORCE)
        _, sel = lax.top_k(imp, top_n)
        k_sel = k_blk[b_ix, g_ix, sel]
        v_sel = v_blk[b_ix, g_ix, sel]
        kpos = sel[..., None] * SEL_BLOCK + sel_off
        dist = t[None, None, :, None, None] - kpos
        s = jnp.einsum('bqgrd,bgqnkd->bgrqnk', qs, k_sel).astype(f32)
        s = s - slopes[None, :, :, None, None, None] * dist[:, :, None].astype(f32)
        s = jnp.where((dist >= 0)[:, :, None], s, NEG_INF)
        p_s = jax.nn.softmax(s.reshape(B, G, R, Q_BLOCK, top_n * SEL_BLOCK), axis=-1).reshape(s.shape)
        o_slc = jnp.einsum('bgrqnk,bgqnkd->bqgrd', p_s.astype(v_sel.dtype), v_sel)

        kw = lax.dynamic_slice_in_dim(k_pad, blk * Q_BLOCK, WINDOW + Q_BLOCK, axis=1)
        vw = lax.dynamic_slice_in_dim(v_pad, blk * Q_BLOCK, WINDOW + Q_BLOCK, axis=1)
        kpos_w = blk * Q_BLOCK + win_off
        dist_w = t[:, None] - kpos_w[None, :]
        mask_w = (dist_w >= 0) & (dist_w < WINDOW) & (kpos_w[None, :] >= 0)
        s = jnp.einsum('bqgrd,bkgd->bgrqk', qs, kw).astype(f32)
        s = s - slopes[:, :, None, None] * dist_w.astype(f32)
        s = jnp.where(mask_w, s, NEG_INF)
        p_w = jax.nn.softmax(s, axis=-1)
        o_win = jnp.einsum('bgrqk,bkgd->bqgrd', p_w.astype(vw.dtype), vw)

        o = gb[..., 0:1] * o_cmp + gb[..., 1:2] * o_slc + gb[..., 2:3] * o_win
        return o.reshape(B, Q_BLOCK, H * dh)

    out = lax.map(block_fn, (q_blocks, g_blocks, jnp.arange(n_qb, dtype=jnp.int32)))
    return out.transpose(1, 0, 2, 3).reshape(B, T, H * dh)


def gla_attention(q, k, v, log_a):
    B, T, Hh, dk = q.shape
    dv = v.shape[-1]
    C = GLA_CHUNK
    N = T // C
    f32 = jnp.float32

    def chunks(z):
        return z.astype(f32).reshape(B, N, C, Hh, z.shape[-1]).transpose(1, 0, 3, 2, 4)

    qc = chunks(q) * (dk ** -0.5)
    kc = chunks(k)
    vc = chunks(v)
    b = jnp.cumsum(chunks(log_a), axis=3)
    b_last = b[:, :, :, -1:]
    q_in = qc * jnp.exp(b)
    k_in = kc * jnp.exp(-b)
    k_st = kc * jnp.exp(b_last - b)
    tril = jnp.tril(jnp.ones((C, C), dtype=bool))
    A = jnp.where(tril, jnp.einsum('nbhid,nbhjd->nbhij', q_in, k_in), 0.0)
    o_intra = jnp.einsum('nbhij,nbhjv->nbhiv', A, vc)

    def step(S, xs):
        qi, ki, vi, bl = xs
        o = jnp.einsum('bhid,bhdv->bhiv', qi, S)
        S = S * jnp.exp(bl)[:, :, 0, :, None] + jnp.einsum('bhjd,bhjv->bhdv', ki, vi)
        return S, o

    S0 = jnp.zeros((B, Hh, dk, dv), f32)
    _, o_inter = lax.scan(step, S0, (q_in, k_st, vc, b_last))
    o = o_intra + o_inter
    return o.transpose(1, 0, 3, 2, 4).reshape(B, T, Hh, dv)


def hybrid_mixer(h, w_in, pe_k, w1_k, w2_k, pe_v, w1_v, w2_v, w_a2, b_a, gla_g, w_out):
    B, T, _ = h.shape
    proj = h @ w_in
    (q, kc, vc, ks, vs, kw, vw, g_nsa, gq, gk, gv, gr, ga) = split_columns(proj, IN_SPLITS)
    kv = lambda z: z.reshape(B, T, NSA_KV_GROUPS, NSA_HEAD_DIM)
    gates = jax.nn.sigmoid(g_nsa.astype(jnp.float32)).astype(h.dtype).reshape(B, T, NSA_HEADS, 3)
    o_nsa = nsa_attention(q.reshape(B, T, NSA_HEADS, NSA_HEAD_DIM), kv(kc), kv(vc), kv(ks), kv(vs), kv(kw), kv(vw),
                          gates, pe_k, w1_k, w2_k, pe_v, w1_v, w2_v)

    log_a = jax.nn.log_sigmoid((ga @ w_a2 + b_a).astype(jnp.float32)) / GLA_TAU
    o_gla = gla_attention(gq.reshape(B, T, GLA_HEADS, GLA_DK), gk.reshape(B, T, GLA_HEADS, GLA_DK),
                          gv.reshape(B, T, GLA_HEADS, GLA_DV), log_a.reshape(B, T, GLA_HEADS, GLA_DK))
    o_gla = rmsnorm(o_gla, gla_g) * jax.nn.silu(gr.astype(jnp.float32)).reshape(B, T, GLA_HEADS, GLA_DV)
    o_gla = o_gla.reshape(B, T, GLA_V_W).astype(h.dtype)

    return jnp.concatenate([o_nsa, o_gla], axis=-1) @ w_out


def setup_inputs(seed: int = 0) -> dict:
    key = jax.random.key(seed)
    ks = jax.random.split(key, 24)
    f32 = jnp.float32

    def nrm(k, shape, scale):
        return jax.random.normal(k, shape, f32) * scale

    def gain(k, n):
        return 1.0 + 0.02 * jax.random.normal(k, (DEPTH, n), f32)

    L = DEPTH
    return {
        'x': nrm(ks[0], (BATCH, SEQ, D_MODEL), 1.0),
        'ffn1_norm': gain(ks[1], D_MODEL),
        'ffn1_w_gate': nrm(ks[2], (L, D_MODEL, D_FF), D_MODEL ** -0.5),
        'ffn1_w_up': nrm(ks[3], (L, D_MODEL, D_FF), D_MODEL ** -0.5),
        'ffn1_w_down': nrm(ks[4], (L, D_FF, D_MODEL), D_FF ** -0.5),
        'mix_norm': gain(ks[5], D_MODEL),
        'w_in': nrm(ks[6], (L, D_MODEL, D_IN_PROJ), D_MODEL ** -0.5),
        'nsa_pe_k': nrm(ks[7], (L, CMP_BLOCK, NSA_HEAD_DIM), 0.02),
        'nsa_w1_k': nrm(ks[8], (L, CMP_BLOCK * NSA_HEAD_DIM, CMP_HIDDEN), (CMP_BLOCK * NSA_HEAD_DIM) ** -0.5),
        'nsa_w2_k': nrm(ks[9], (L, CMP_HIDDEN, NSA_HEAD_DIM), CMP_HIDDEN ** -0.5),
        'nsa_pe_v': nrm(ks[10], (L, CMP_BLOCK, NSA_HEAD_DIM), 0.02),
        'nsa_w1_v': nrm(ks[11], (L, CMP_BLOCK * NSA_HEAD_DIM, CMP_HIDDEN), (CMP_BLOCK * NSA_HEAD_DIM) ** -0.5),
        'nsa_w2_v': nrm(ks[12], (L, CMP_HIDDEN, NSA_HEAD_DIM), CMP_HIDDEN ** -0.5),
        'gla_w_a2': nrm(ks[13], (L, GLA_LOWRANK, GLA_K_W), GLA_LOWRANK ** -0.5),
        'gla_b_a': nrm(ks[14], (L, GLA_K_W), 0.1),
        'gla_norm': gain(ks[15], GLA_DV),
        'w_out': nrm(ks[16], (L, D_MIX, D_MODEL), D_MIX ** -0.5),
        'ffn2_norm': gain(ks[17], D_MODEL),
        'ffn2_w_gate': nrm(ks[18], (L, D_MODEL, D_FF), D_MODEL ** -0.5),
        'ffn2_w_up': nrm(ks[19], (L, D_MODEL, D_FF), D_MODEL ** -0.5),
        'ffn2_w_down': nrm(ks[20], (L, D_FF, D_MODEL), D_FF ** -0.5),
        'final_norm': 1.0 + 0.02 * jax.random.normal(ks[21], (D_MODEL,), f32),
    }


def reference(x, ffn1_norm, ffn1_w_gate, ffn1_w_up, ffn1_w_down, mix_norm, w_in,
              nsa_pe_k, nsa_w1_k, nsa_w2_k, nsa_pe_v, nsa_w1_v, nsa_w2_v,
              gla_w_a2, gla_b_a, gla_norm, w_out,
              ffn2_norm, ffn2_w_gate, ffn2_w_up, ffn2_w_down, final_norm):
    for l in range(DEPTH):
        x = x + 0.5 * swiglu_ffn(rmsnorm(x, ffn1_norm[l]), ffn1_w_gate[l], ffn1_w_up[l], ffn1_w_down[l])
        x = x + hybrid_mixer(rmsnorm(x, mix_norm[l]), w_in[l],
                             nsa_pe_k[l], nsa_w1_k[l], nsa_w2_k[l], nsa_pe_v[l], nsa_w1_v[l], nsa_w2_v[l],
                             gla_w_a2[l], gla_b_a[l], gla_norm[l], w_out[l])
        x = x + 0.5 * swiglu_ffn(rmsnorm(x, ffn2_norm[l]), ffn2_w_gate[l], ffn2_w_up[l], ffn2_w_down[l])
    return rmsnorm(x, final_norm)
```

```python
import functools

import jax
import jax.numpy as jnp
from jax import lax
from jax.experimental import pallas as pl
from jax.experimental.pallas import tpu as pltpu

F32 = jnp.float32
BF16 = jnp.bfloat16
I32 = jnp.int32

D_MODEL = 1024
SEQ = 2048
D_FF = 2816
EPS = 1e-6
NEG_INF = -1e30
FORCE = 1e9

NSA_HEADS = 8
NSA_GROUPS = 2
NSA_REP = NSA_HEADS // NSA_GROUPS
HEAD_DIM = 64
CMP_BLOCK = 32
CMP_STRIDE = 16
CMP_HIDDEN = 128
N_CMP_PAD = SEQ // CMP_STRIDE
SEL_BLOCK = 64
SEL_SHIFT = 6
assert 1 << SEL_SHIFT == SEL_BLOCK
N_SEL = SEQ // SEL_BLOCK
SEL_TOPK = 8
WINDOW = 512
Q_BLOCK = 128
N_QB = SEQ // Q_BLOCK
SLC_CHUNK = 256
WIN_CHUNK = 128

GLA_HEADS = 4
GLA_DK = 64
GLA_DV = 128
GLA_CHUNK = 64
GLA_LOWRANK = 16
GLA_TAU = 16.0

LANES = 128
V7X_VMEM_BYTES = 64 * 1024 * 1024

C_Q = (0, 512)
C_KC = (512, 640)
C_VC = (640, 768)
C_KVS = (768, 1024)
C_KVW = (1024, 1280)
C_MISC = (1280, 1536)
C_GQ = (1536, 1792)
C_GK = (1792, 2048)
C_GV = (2048, 2560)
C_GR = (2560, 3072)
D_IN_PAD = 3072
MISC_GA_OFF = 12


def _params(sem, vmem_bytes):
    return pltpu.CompilerParams(dimension_semantics=sem,
                                vmem_limit_bytes=min(int(vmem_bytes), V7X_VMEM_BYTES - (8 << 20)))


def _rms(x, g):
    return x * lax.rsqrt(jnp.mean(x * x, axis=-1, keepdims=True) + EPS) * g


def _dot(a, b, **kw):
    return jnp.dot(a, b, preferred_element_type=F32, **kw)


def _dot_nt(a, b, **kw):
    return lax.dot_general(a, b, (((1,), (1,)), ((), ())), preferred_element_type=F32, **kw)


def _dot_tn(a, b, **kw):
    return lax.dot_general(a, b, (((0,), (0,)), ((), ())), preferred_element_type=F32, **kw)


def _ffn_kernel(x_ref, g_ref, wg_ref, wu_ref, wd_ref, fg_ref, o_ref, h_ref, acc_ref, *, n_f, final_norm):
    f = pl.program_id(1)

    @pl.when(f == 0)
    def _():
        h_ref[...] = _rms(x_ref[...], g_ref[...]).astype(BF16)
        acc_ref[...] = jnp.zeros_like(acc_ref)

    h = h_ref[...]
    gate = _dot(h, wg_ref[...])
    up = _dot(h, wu_ref[...])
    act = (gate * jax.nn.sigmoid(gate)) * up
    acc_ref[...] += _dot(act.astype(BF16), wd_ref[...])

    @pl.when(f == n_f - 1)
    def _():
        y = x_ref[...] + 0.5 * acc_ref[...]
        if final_norm:
            y = _rms(y, fg_ref[...])
        o_ref[...] = y


def _ffn(x, gain, wg, wu, wd, final_gain, *, final_norm, tm=512, tf=256):
    n, d = x.shape
    n_f = D_FF // tf
    vmem = 2 * 2 * tm * d * 4 + tm * d * (2 + 4) + 2 * 3 * d * tf * 2 + 3 * tm * tf * 4 + (8 << 20)
    return pl.pallas_call(
        functools.partial(_ffn_kernel, n_f=n_f, final_norm=final_norm),
        grid=(n // tm, n_f),
        in_specs=[
            pl.BlockSpec((tm, d), lambda i, f: (i, 0)),
            pl.BlockSpec((1, d), lambda i, f: (0, 0)),
            pl.BlockSpec((d, tf), lambda i, f: (0, f)),
            pl.BlockSpec((d, tf), lambda i, f: (0, f)),
            pl.BlockSpec((tf, d), lambda i, f: (f, 0)),
            pl.BlockSpec((1, d), lambda i, f: (0, 0)),
        ],
        out_specs=pl.BlockSpec((tm, d), lambda i, f: (i, 0)),
        out_shape=jax.ShapeDtypeStruct((n, d), F32),
        scratch_shapes=[pltpu.VMEM((tm, d), BF16), pltpu.VMEM((tm, d), F32)],
        compiler_params=_params(("parallel", "arbitrary"), vmem),
        name="ffn",
    )(x, gain, wg, wu, wd, final_gain)


def _inproj_kernel(x_ref, g_ref, w_ref, q_ref, kc_ref, vc_ref, kvs_ref, kvw_ref, misc_ref,
                   gq_ref, gk_ref, gv_ref, gr_ref):
    h = _rms(x_ref[...], g_ref[...]).astype(BF16)

    def proj(c):
        return _dot(h, w_ref[:, c[0]:c[1]])

    q_ref[...] = (proj(C_Q) * (HEAD_DIM ** -0.5)).astype(BF16)
    kc_ref[...] = proj(C_KC)
    vc_ref[...] = proj(C_VC)
    kvs_ref[...] = proj(C_KVS).astype(BF16)
    kvw_ref[...] = proj(C_KVW).astype(BF16)
    misc_ref[...] = proj(C_MISC)
    gq_ref[...] = proj(C_GQ)
    gk_ref[...] = proj(C_GK)
    gv_ref[...] = proj(C_GV).astype(BF16)
    gr_ref[...] = proj(C_GR)


def _inproj(x, gain, w, tm=512):
    n, d = x.shape
    outs = [(C_Q, BF16), (C_KC, F32), (C_VC, F32), (C_KVS, BF16), (C_KVW, BF16), (C_MISC, F32),
            (C_GQ, F32), (C_GK, F32), (C_GV, BF16), (C_GR, F32)]
    vmem = 2 * tm * d * 4 + 2 * d * D_IN_PAD * 2 + 2 * tm * D_IN_PAD * 4 + (8 << 20)
    return pl.pallas_call(
        _inproj_kernel,
        grid=(n // tm,),
        in_specs=[
            pl.BlockSpec((tm, d), lambda i: (i, 0)),
            pl.BlockSpec((1, d), lambda i: (0, 0)),
            pl.BlockSpec((d, D_IN_PAD), lambda i: (0, 0)),
        ],
        out_specs=[pl.BlockSpec((tm, c[1] - c[0]), lambda i: (i, 0)) for c, _ in outs],
        out_shape=[jax.ShapeDtypeStruct((n, c[1] - c[0]), dt) for c, dt in outs],
        compiler_params=_params(("parallel",), vmem),
        name="inproj",
    )(x, gain, w)


def _compress_kernel(kc_ref, vc_ref, pek_ref, pev_ref, w1k_ref, w1v_ref, w2k_ref, w2v_ref, o_ref):
    half = CMP_BLOCK // 2

    def hidden(x_ref, pe_ref, w1_ref):
        acc_a = jnp.zeros((N_CMP_PAD, 2 * CMP_HIDDEN), F32)
        acc_b = jnp.zeros((N_CMP_PAD, 2 * CMP_HIDDEN), F32)
        for l in range(half):
            rows = x_ref[pl.ds(l, N_CMP_PAD, stride=CMP_STRIDE), :]
            acc_a += _dot((rows + pe_ref[l:l + 1, :]).astype(BF16), w1_ref[l])
            acc_b += _dot((rows + pe_ref[half + l:half + l + 1, :]).astype(BF16), w1_ref[half + l])
        pre = acc_a + pltpu.roll(acc_b, N_CMP_PAD - 1, axis=0)
        return jax.nn.gelu(pre, approximate=True).astype(BF16)

    out = _dot(hidden(kc_ref, pek_ref, w1k_ref), w2k_ref[...])
    out += _dot(hidden(vc_ref, pev_ref, w1v_ref), w2v_ref[...])
    row = lax.broadcasted_iota(I32, out.shape, 0)
    o_ref[0] = jnp.where(row < N_CMP_PAD - 1, out, 0.0).astype(BF16)


def _compress(kc, vc, pek, pev, w1k, w1v, w2k, w2v):
    n = kc.shape[0]
    b = n // SEQ
    full = lambda a: pl.BlockSpec(a.shape, lambda i: (0,) * a.ndim)
    return pl.pallas_call(
        _compress_kernel,
        grid=(b,),
        in_specs=[
            pl.BlockSpec((SEQ, LANES), lambda i: (i, 0)),
            pl.BlockSpec((SEQ, LANES), lambda i: (i, 0)),
            full(pek), full(pev), full(w1k), full(w1v), full(w2k), full(w2v),
        ],
        out_specs=pl.BlockSpec((1, N_CMP_PAD, 2 * LANES), lambda i: (i, 0, 0)),
        out_shape=jax.ShapeDtypeStruct((b, N_CMP_PAD, 2 * LANES), BF16),
        compiler_params=_params(("parallel",), 32 << 20),
        name="compress",
    )(kc, vc, pek, pev, w1k, w1v, w2k, w2v)


def _nsa_kernel(q_ref, misc_ref, cmp_ref, kvs_ref, kvw_ref, o_ref, mask_ref):
    g = pl.program_id(1)
    qb = pl.program_id(2)
    t0 = qb * Q_BLOCK
    slopes = [jnp.where(g == 0, 2.0 ** -(r + 1), 2.0 ** -(r + 1 + NSA_REP)).astype(F32)
              for r in range(NSA_REP)]

    lane = lax.broadcasted_iota(I32, (Q_BLOCK, LANES), 1)
    sub = lax.broadcasted_iota(I32, (Q_BLOCK, LANES), 0)
    low = lane < HEAD_DIM

    qf = q_ref[...].astype(F32)
    parts = []
    for r in range(NSA_REP):
        tile = qf[:, (r // 2) * LANES:(r // 2 + 1) * LANES]
        if r % 2:
            tile = pltpu.roll(tile, HEAD_DIM, axis=1)
        parts.append(jnp.where(low, tile, 0.0).astype(BF16))
    q4 = jnp.concatenate(parts, axis=0)

    tq = t0 + sub

    kvc = cmp_ref[0]
    s_c = _dot_nt(q4, kvc)
    valid_c = lane * CMP_STRIDE + (CMP_BLOCK - 1) <= tq
    dist_c = tq.astype(F32) - (lane.astype(F32) * CMP_STRIDE + (CMP_BLOCK - 1) / 2.0)
    p_sum = jnp.zeros((Q_BLOCK, LANES), F32)
    p_parts = []
    for r in range(NSA_REP):
        s = s_c[r * Q_BLOCK:(r + 1) * Q_BLOCK] - slopes[r] * dist_c
        s = jnp.where(valid_c, s, NEG_INF)
        e = jnp.exp(s - jnp.max(s, axis=-1, keepdims=True))
        p = jnp.where(valid_c, e / jnp.sum(e, axis=-1, keepdims=True), 0.0)
        p_sum += p
        p_parts.append(p.astype(BF16))
    o_cmp = _dot(jnp.concatenate(p_parts, axis=0), kvc)

    ov_t = ((lane < 4 * sub + 4) & (lane > 4 * sub - 2) & (sub < N_SEL)).astype(F32)
    imp_t = _dot_nt(ov_t, p_sum, precision=lax.Precision.HIGHEST)[:N_SEL]
    j_blk = lax.broadcasted_iota(I32, (N_SEL, LANES), 0)
    cur = (t0 + lax.broadcasted_iota(I32, (N_SEL, LANES), 1)) >> SEL_SHIFT
    forced = (j_blk == 0) | (j_blk == cur) | (j_blk == cur - 1)
    imp_t = jnp.where(forced, FORCE, imp_t)
    imp_t = jnp.where(j_blk <= cur, imp_t, -FORCE)
    rank = jnp.zeros((N_SEL, LANES), I32)
    for jp in range(N_SEL):
        row = imp_t[jp:jp + 1, :]
        beats = (row > imp_t) | ((row == imp_t) & (j_blk > jp))
        rank += beats.astype(I32)
    sel_t = (rank < SEL_TOPK).astype(F32)
    sel_q = jnp.concatenate([sel_t, jnp.zeros((LANES - N_SEL, LANES), F32)], axis=0).T
    sel_q = sel_q.astype(BF16)
    e_row = lax.broadcasted_iota(I32, (LANES, SLC_CHUNK), 0)
    e_col = lax.broadcasted_iota(I32, (LANES, SLC_CHUNK), 1) >> SEL_SHIFT
    for c in range(SEQ // SLC_CHUNK):
        expand = (e_row == e_col + c * (SLC_CHUNK // SEL_BLOCK)).astype(BF16)
        mask_ref[c] = _dot(sel_q, expand)

    def attend(n_steps, first_chunk, chunk, kv_ref, allowed_fn):
        kpos0 = lax.broadcasted_iota(I32, (Q_BLOCK, chunk), 1)
        tq_c = t0 + lax.broadcasted_iota(I32, (Q_BLOCK, chunk), 0)

        def body(i, carry):
            ms, ls, accs = carry
            c = first_chunk + i
            kv = kv_ref[pl.ds(pl.multiple_of(c * chunk, chunk), chunk), :]
            s_all = _dot_nt(q4, kv)
            dist = tq_c - (kpos0 + c * chunk)
            allowed = allowed_fn(c, dist)
            dist_f = dist.astype(F32)
            new_m, new_l, alphas, p_parts = [], [], [], []
            for r in range(NSA_REP):
                s = s_all[r * Q_BLOCK:(r + 1) * Q_BLOCK] - slopes[r] * dist_f
                s = jnp.where(allowed, s, NEG_INF)
                m_new = jnp.maximum(ms[r], jnp.max(s, axis=-1, keepdims=True))
                alpha = jnp.exp(ms[r] - m_new)
                p = jnp.where(allowed, jnp.exp(s - m_new), 0.0)
                new_m.append(m_new)
                new_l.append(alpha * ls[r] + jnp.sum(p, axis=-1, keepdims=True))
                alphas.append(alpha)
                p_parts.append(p.astype(BF16))
            pv = _dot(jnp.concatenate(p_parts, axis=0), kv)
            new_acc = [alphas[r] * accs[r] + pv[r * Q_BLOCK:(r + 1) * Q_BLOCK] for r in range(NSA_REP)]
            return tuple(new_m), tuple(new_l), tuple(new_acc)

        init = (tuple(jnp.full((Q_BLOCK, 1), NEG_INF, F32) for _ in range(NSA_REP)),
                tuple(jnp.zeros((Q_BLOCK, 1), F32) for _ in range(NSA_REP)),
                tuple(jnp.zeros((Q_BLOCK, LANES), F32) for _ in range(NSA_REP)))
        _, ls, accs = lax.fori_loop(0, n_steps, body, init)
        return [accs[r] / ls[r] for r in range(NSA_REP)]

    n_slc = (t0 + Q_BLOCK + SLC_CHUNK - 1) // SLC_CHUNK
    o_slc = attend(n_slc, 0, SLC_CHUNK, kvs_ref,
                   lambda c, dist: (mask_ref[c] > 0.5) & (dist >= 0))

    first_w = jnp.maximum(qb - WINDOW // WIN_CHUNK, 0)
    o_win = attend(qb - first_w + 1, first_w, WIN_CHUNK, kvw_ref,
                   lambda c, dist: (dist >= 0) & (dist < WINDOW))

    gates = jax.nn.sigmoid(misc_ref[...])
    heads = []
    for r in range(NSA_REP):
        o = (gates[:, 3 * r:3 * r + 1] * o_cmp[r * Q_BLOCK:(r + 1) * Q_BLOCK]
             + gates[:, 3 * r + 1:3 * r + 2] * o_slc[r]
             + gates[:, 3 * r + 2:3 * r + 3] * o_win[r])
        heads.append(o)
    for pair in range(NSA_REP // 2):
        tile = jnp.where(low, pltpu.roll(heads[2 * pair], HEAD_DIM, axis=1), heads[2 * pair + 1])
        o_ref[:, pair * LANES:(pair + 1) * LANES] = tile.astype(BF16)


def _nsa(q, misc, cmp, kvs, kvw):
    n = q.shape[0]
    b = n // SEQ
    gw = NSA_REP * HEAD_DIM
    return pl.pallas_call(
        _nsa_kernel,
        grid=(b, NSA_GROUPS, N_QB),
        in_specs=[
            pl.BlockSpec((Q_BLOCK, gw), lambda i, g, j: (i * N_QB + j, g)),
            pl.BlockSpec((Q_BLOCK, LANES), lambda i, g, j: (i * N_QB + j, g)),
            pl.BlockSpec((1, N_CMP_PAD, LANES), lambda i, g, j: (i, 0, g)),
            pl.BlockSpec((SEQ, LANES), lambda i, g, j: (i, g)),
            pl.BlockSpec((SEQ, LANES), lambda i, g, j: (i, g)),
        ],
        out_specs=pl.BlockSpec((Q_BLOCK, gw), lambda i, g, j: (i * N_QB + j, g)),
        out_shape=jax.ShapeDtypeStruct((n, NSA_GROUPS * gw), BF16),
        scratch_shapes=[pltpu.VMEM((SEQ // SLC_CHUNK, Q_BLOCK, SLC_CHUNK), F32)],
        compiler_params=_params(("parallel", "parallel", "arbitrary"), 32 << 20),
        name="nsa",
    )(q, misc, cmp, kvs, kvw)


def _gla_kernel(q_ref, k_ref, v_ref, r_ref, misc_ref, wa_ref, ba_ref, gn_ref, o_ref):
    c = GLA_CHUNK
    lane = lax.broadcasted_iota(I32, (c, LANES), 1)
    tri = lax.broadcasted_iota(I32, (c, c), 0) >= lax.broadcasted_iota(I32, (c, c), 1)
    tri_f = tri.astype(F32)
    lane_s = lax.broadcasted_iota(I32, (GLA_DV, LANES), 1)
    hi = lax.Precision.HIGHEST

    def body(n, st):
        rows = pl.ds(pl.multiple_of(n * c, c), c)
        la = _dot(misc_ref[rows, :], wa_ref[...], precision=hi) + ba_ref[...]
        log_a = (jnp.minimum(la, 0.0) - jnp.log1p(jnp.exp(-jnp.abs(la)))) * (1.0 / GLA_TAU)
        b = _dot(tri_f, log_a, precision=hi)
        b_last = b[c - 1:c, :]
        q = q_ref[rows, :] * (GLA_DK ** -0.5)
        k = k_ref[rows, :]
        q_in = q * jnp.exp(b)
        k_in = (k * jnp.exp(-b)).astype(BF16)
        k_st = (k * jnp.exp(b_last - b)).astype(BF16)
        st_b = st.astype(BF16)
        upd = []
        for h in range(2):
            mh = (lane < GLA_DK) if h == 0 else (lane >= GLA_DK)
            qh = jnp.where(mh, q_in, 0.0).astype(BF16)
            a = jnp.where(tri, _dot_nt(qh, k_in), 0.0)
            vh = v_ref[rows, h * GLA_DV:(h + 1) * GLA_DV]
            o = _dot(a.astype(BF16), vh) + _dot_nt(qh, st_b)
            upd.append(_dot_tn(vh, k_st))
            y = _rms(o, gn_ref[...])
            gate = r_ref[rows, h * GLA_DV:(h + 1) * GLA_DV]
            o_ref[rows, h * GLA_DV:(h + 1) * GLA_DV] = (y * (gate * jax.nn.sigmoid(gate))).astype(BF16)
        return st * jnp.exp(b_last) + jnp.where(lane_s < GLA_DK, upd[0], upd[1])

    lax.fori_loop(0, SEQ // c, body, jnp.zeros((GLA_DV, LANES), F32))


def _gla(gq, gk, gv, gr, misc, wa, ba, gn):
    n = gq.shape[0]
    b = n // SEQ
    return pl.pallas_call(
        _gla_kernel,
        grid=(b, GLA_HEADS // 2),
        in_specs=[
            pl.BlockSpec((SEQ, LANES), lambda i, p: (i, p)),
            pl.BlockSpec((SEQ, LANES), lambda i, p: (i, p)),
            pl.BlockSpec((SEQ, 2 * GLA_DV), lambda i, p: (i, p)),
            pl.BlockSpec((SEQ, 2 * GLA_DV), lambda i, p: (i, p)),
            pl.BlockSpec((SEQ, LANES), lambda i, p: (i, 0)),
            pl.BlockSpec((LANES, LANES), lambda i, p: (0, p)),
            pl.BlockSpec((1, LANES), lambda i, p: (0, p)),
            pl.BlockSpec((1, GLA_DV), lambda i, p: (0, 0)),
        ],
        out_specs=pl.BlockSpec((SEQ, 2 * GLA_DV), lambda i, p: (i, p)),
        out_shape=jax.ShapeDtypeStruct((n, GLA_HEADS * GLA_DV), BF16),
        compiler_params=_params(("parallel", "parallel"), 40 << 20),
        name="gla",
    )(gq, gk, gv, gr, misc, wa, ba, gn)


def _outproj_kernel(x_ref, a_ref, b_ref, wa_ref, wb_ref, o_ref):
    o_ref[...] = x_ref[...] + _dot(a_ref[...], wa_ref[...]) + _dot(b_ref[...], wb_ref[...])


def _outproj(x, o_nsa, o_gla, w_a, w_b, tm=512):
    n, d = x.shape
    ka, kb = o_nsa.shape[1], o_gla.shape[1]
    vmem = 2 * 2 * tm * d * 4 + 2 * tm * (ka + kb) * 2 + 2 * (ka + kb) * d * 2 + (8 << 20)
    return pl.pallas_call(
        _outproj_kernel,
        grid=(n // tm,),
        in_specs=[
            pl.BlockSpec((tm, d), lambda i: (i, 0)),
            pl.BlockSpec((tm, ka), lambda i: (i, 0)),
            pl.BlockSpec((tm, kb), lambda i: (i, 0)),
            pl.BlockSpec((ka, d), lambda i: (0, 0)),
            pl.BlockSpec((kb, d), lambda i: (0, 0)),
        ],
        out_specs=pl.BlockSpec((tm, d), lambda i: (i, 0)),
        out_shape=jax.ShapeDtypeStruct((n, d), F32),
        compiler_params=_params(("parallel",), vmem),
        name="outproj",
    )(x, o_nsa, o_gla, w_a, w_b)


def _permute_w_in(w):
    d = w.shape[0]
    col = lambda lo, hi: w[:, lo:hi]
    zeros = lambda k: jnp.zeros((d, k), w.dtype)
    hd = HEAD_DIM

    def kv_pair(k0, v0):
        return [col(k0, k0 + hd), col(v0, v0 + hd), col(k0 + hd, k0 + 2 * hd), col(v0 + hd, v0 + 2 * hd)]

    n_gate = 3 * NSA_REP
    misc = [col(1280, 1280 + n_gate), col(2840, 2856), zeros(LANES - n_gate - GLA_LOWRANK),
            col(1280 + n_gate, 1304), zeros(LANES - n_gate)]
    parts = ([col(0, 512), col(512, 640), col(640, 768)] + kv_pair(768, 896) + kv_pair(1024, 1152) + misc
             + [col(1304, 1560), col(1560, 1816), col(1816, 2328), col(2328, 2840)])
    out = jnp.concatenate(parts, axis=1)
    assert out.shape[1] == D_IN_PAD
    return out.astype(BF16)


def _compress_weights(pe, w1, w2, out_off):
    hd, hid = HEAD_DIM, CMP_HIDDEN
    pe2 = jnp.concatenate([pe, pe], axis=1)
    w1 = w1.reshape(CMP_BLOCK, hd, hid)
    z1 = jnp.zeros_like(w1)
    w1e = jnp.concatenate([jnp.concatenate([w1, z1], axis=2), jnp.concatenate([z1, w1], axis=2)], axis=1)
    w2e = jnp.zeros((2 * hid, 2 * LANES), w2.dtype)
    w2e = w2e.at[:hid, out_off:out_off + hd].set(w2)
    w2e = w2e.at[hid:, LANES + out_off:LANES + out_off + hd].set(w2)
    return pe2, w1e.astype(BF16), w2e.astype(BF16)


def kernel(x, ffn1_norm, ffn1_w_gate, ffn1_w_up, ffn1_w_down, mix_norm, w_in, nsa_pe_k, nsa_w1_k, nsa_w2_k,
           nsa_pe_v, nsa_w1_v, nsa_w2_v, gla_w_a2, gla_b_a, gla_norm, w_out, ffn2_norm, ffn2_w_gate,
           ffn2_w_up, ffn2_w_down, final_norm):
    bsz, seq, d = x.shape
    assert (seq, d) == (SEQ, D_MODEL) and ffn1_norm.shape[0] == 1
    n = bsz * seq
    xf = x.reshape(n, d)
    row = lambda v: v.reshape(1, -1).astype(F32)
    bf = lambda w: w.astype(BF16)
    ones = jnp.ones((1, d), F32)

    x1 = _ffn(xf, row(ffn1_norm[0]), bf(ffn1_w_gate[0]), bf(ffn1_w_up[0]), bf(ffn1_w_down[0]), ones,
              final_norm=False)

    q, kc, vc, kvs, kvw, misc, gq, gk, gv, gr = _inproj(x1, row(mix_norm[0]), _permute_w_in(w_in[0]))

    pek, w1k, w2k = _compress_weights(nsa_pe_k[0], nsa_w1_k[0], nsa_w2_k[0], 0)
    pev, w1v, w2v = _compress_weights(nsa_pe_v[0], nsa_w1_v[0], nsa_w2_v[0], HEAD_DIM)
    cmp = _compress(kc, vc, pek, pev, w1k, w1v, w2k, w2v)

    o_nsa = _nsa(q, misc, cmp, kvs, kvw)

    wa = jnp.zeros((LANES, GLA_HEADS * GLA_DK), F32).at[MISC_GA_OFF:MISC_GA_OFF + GLA_LOWRANK].set(gla_w_a2[0])
    o_gla = _gla(gq, gk, gv, gr, misc, wa, row(gla_b_a[0]), row(gla_norm[0]))

    d_nsa = NSA_HEADS * HEAD_DIM
    x2 = _outproj(x1, o_nsa, o_gla, bf(w_out[0][:d_nsa]), bf(w_out[0][d_nsa:]))

    out = _ffn(x2, row(ffn2_norm[0]), bf(ffn2_w_gate[0]), bf(ffn2_w_up[0]), bf(ffn2_w_down[0]),
               row(final_norm), final_norm=True)
    return out.reshape(bsz, seq, d)
```

```python
import functools

import jax
import jax.numpy as jnp
from jax import lax
from jax.experimental import pallas as pl
from jax.experimental.pallas import tpu as pltpu

F32 = jnp.float32
BF16 = jnp.bfloat16
I32 = jnp.int32

D_MODEL = 1024
SEQ = 2048
D_FF = 2816
EPS = 1e-6
NEG_INF = -1e30
FORCE = 1e9

NSA_HEADS = 8
NSA_GROUPS = 2
NSA_REP = NSA_HEADS // NSA_GROUPS
HEAD_DIM = 64
CMP_BLOCK = 32
CMP_STRIDE = 16
CMP_HIDDEN = 128
N_CMP_PAD = SEQ // CMP_STRIDE
SEL_BLOCK = 64
SEL_SHIFT = 6
assert 1 << SEL_SHIFT == SEL_BLOCK
N_SEL = SEQ // SEL_BLOCK
SEL_TOPK = 8
WINDOW = 512
Q_BLOCK = 128
N_QB = SEQ // Q_BLOCK
KV_CHUNK = 256
QB_PER_CLASS = KV_CHUNK // Q_BLOCK

GLA_HEADS = 4
GLA_DK = 64
GLA_DV = 128
GLA_CHUNK = 64
GLA_LOWRANK = 16
GLA_TAU = 16.0

LANES = 128
V7X_VMEM_BYTES = 64 * 1024 * 1024

C_Q = (0, 512)
C_KC = (512, 640)
C_VC = (640, 768)
C_KSA = (768, 1024)
C_KWA = (1024, 1280)
C_VS = (1280, 1408)
C_VW = (1408, 1536)
C_MISC = (1536, 1792)
C_GQ = (1792, 2048)
C_GK = (2048, 2304)
C_GV = (2304, 2816)
C_GR = (2816, 3328)
D_IN_PAD = 3328
MISC_GA_OFF = 12

X_SEL = HEAD_DIM
X_HI = HEAD_DIM + N_SEL
X_LO = X_HI + 1
POS_SPLIT = 64
POS_SHIFT = 6
assert 1 << POS_SHIFT == POS_SPLIT and X_LO < LANES


def _params(sem, vmem_bytes):
    return pltpu.CompilerParams(dimension_semantics=sem,
                                vmem_limit_bytes=min(int(vmem_bytes), V7X_VMEM_BYTES - (8 << 20)))


def _rms(x, g):
    return x * lax.rsqrt(jnp.mean(x * x, axis=-1, keepdims=True) + EPS) * g


def _dot(a, b, **kw):
    return jnp.dot(a, b, preferred_element_type=F32, **kw)


def _dot_nt(a, b, **kw):
    return lax.dot_general(a, b, (((1,), (1,)), ((), ())), preferred_element_type=F32, **kw)


def _dot_tn(a, b, **kw):
    return lax.dot_general(a, b, (((0,), (0,)), ((), ())), preferred_element_type=F32, **kw)


def _ffn_kernel(x_ref, g_ref, wg_ref, wu_ref, wd_ref, fg_ref, o_ref, h_ref, acc_ref, *, n_f, final_norm):
    f = pl.program_id(1)

    @pl.when(f == 0)
    def _():
        h_ref[...] = _rms(x_ref[...], g_ref[...]).astype(BF16)
        acc_ref[...] = jnp.zeros_like(acc_ref)

    h = h_ref[...]
    gate = _dot(h, wg_ref[...])
    up = _dot(h, wu_ref[...])
    act = (gate * jax.nn.sigmoid(gate)) * up
    acc_ref[...] += _dot(act.astype(BF16), wd_ref[...])

    @pl.when(f == n_f - 1)
    def _():
        y = x_ref[...] + 0.5 * acc_ref[...]
        if final_norm:
            y = _rms(y, fg_ref[...])
        o_ref[...] = y


def _ffn(x, gain, wg, wu, wd, final_gain, *, final_norm, tm=512, tf=256):
    n, d = x.shape
    n_f = D_FF // tf
    vmem = 2 * 2 * tm * d * 4 + tm * d * (2 + 4) + 2 * 3 * d * tf * 2 + 3 * tm * tf * 4 + (8 << 20)
    return pl.pallas_call(
        functools.partial(_ffn_kernel, n_f=n_f, final_norm=final_norm),
        grid=(n // tm, n_f),
        in_specs=[
            pl.BlockSpec((tm, d), lambda i, f: (i, 0)),
            pl.BlockSpec((1, d), lambda i, f: (0, 0)),
            pl.BlockSpec((d, tf), lambda i, f: (0, f)),
            pl.BlockSpec((d, tf), lambda i, f: (0, f)),
            pl.BlockSpec((tf, d), lambda i, f: (f, 0)),
            pl.BlockSpec((1, d), lambda i, f: (0, 0)),
        ],
        out_specs=pl.BlockSpec((tm, d), lambda i, f: (i, 0)),
        out_shape=jax.ShapeDtypeStruct((n, d), F32),
        scratch_shapes=[pltpu.VMEM((tm, d), BF16), pltpu.VMEM((tm, d), F32)],
        compiler_params=_params(("parallel", "arbitrary"), vmem),
        name="ffn",
    )(x, gain, wg, wu, wd, final_gain)


def _key_extras(pos, lane, with_block):
    lg = lane & (LANES - 1)
    ext = jnp.where(lg == X_HI, (pos >> POS_SHIFT).astype(F32),
                    jnp.where(lg == X_LO, (pos & (POS_SPLIT - 1)).astype(F32), 0.0))
    if with_block:
        ext = jnp.where((lg >= X_SEL) & (lg < X_HI) & ((pos >> SEL_SHIFT) == lg - X_SEL), 1.0, ext)
    return ext


def _inproj_kernel(x_ref, g_ref, w_ref, q_ref, kc_ref, vc_ref, ksa_ref, kwa_ref, vs_ref, vw_ref, misc_ref,
                   gq_ref, gk_ref, gv_ref, gr_ref, *, tm):
    h = _rms(x_ref[...], g_ref[...]).astype(BF16)

    def proj(c):
        return _dot(h, w_ref[:, c[0]:c[1]])

    shape = (tm, 2 * LANES)
    lane = lax.broadcasted_iota(I32, shape, 1)
    pos = (pl.program_id(0) * tm + lax.broadcasted_iota(I32, shape, 0)) & (SEQ - 1)
    is_key = (lane & (LANES - 1)) < HEAD_DIM

    q_ref[...] = (proj(C_Q) * (HEAD_DIM ** -0.5)).astype(BF16)
    kc_ref[...] = proj(C_KC)
    vc_ref[...] = proj(C_VC)
    ksa_ref[...] = jnp.where(is_key, proj(C_KSA), _key_extras(pos, lane, True)).astype(BF16)
    kwa_ref[...] = jnp.where(is_key, proj(C_KWA), _key_extras(pos, lane, False)).astype(BF16)
    vs_ref[...] = proj(C_VS).astype(BF16)
    vw_ref[...] = proj(C_VW).astype(BF16)
    misc_ref[...] = proj(C_MISC)
    gq_ref[...] = proj(C_GQ)
    gk_ref[...] = proj(C_GK)
    gv_ref[...] = proj(C_GV).astype(BF16)
    gr_ref[...] = proj(C_GR)


def _inproj(x, gain, w, tm=512):
    n, d = x.shape
    assert SEQ % tm == 0 and SEQ & (SEQ - 1) == 0
    outs = [(C_Q, BF16), (C_KC, F32), (C_VC, F32), (C_KSA, BF16), (C_KWA, BF16), (C_VS, BF16), (C_VW, BF16),
            (C_MISC, F32), (C_GQ, F32), (C_GK, F32), (C_GV, BF16), (C_GR, F32)]
    vmem = 2 * tm * d * 4 + 2 * d * D_IN_PAD * 2 + 2 * tm * D_IN_PAD * 4 + (8 << 20)
    return pl.pallas_call(
        functools.partial(_inproj_kernel, tm=tm),
        grid=(n // tm,),
        in_specs=[
            pl.BlockSpec((tm, d), lambda i: (i, 0)),
            pl.BlockSpec((1, d), lambda i: (0, 0)),
            pl.BlockSpec((d, D_IN_PAD), lambda i: (0, 0)),
        ],
        out_specs=[pl.BlockSpec((tm, c[1] - c[0]), lambda i: (i, 0)) for c, _ in outs],
        out_shape=[jax.ShapeDtypeStruct((n, c[1] - c[0]), dt) for c, dt in outs],
        compiler_params=_params(("parallel",), vmem),
        name="inproj",
    )(x, gain, w)


def _compress_kernel(kc_ref, vc_ref, pek_ref, pev_ref, w1k_ref, w1v_ref, w2k_ref, w2v_ref, kca_ref, vc2_ref):
    half = CMP_BLOCK // 2

    def hidden(x_ref, pe_ref, w1_ref):
        acc_a = jnp.zeros((N_CMP_PAD, 2 * CMP_HIDDEN), F32)
        acc_b = jnp.zeros((N_CMP_PAD, 2 * CMP_HIDDEN), F32)
        for l in range(half):
            rows = x_ref[pl.ds(l, N_CMP_PAD, stride=CMP_STRIDE), :]
            acc_a += _dot((rows + pe_ref[l:l + 1, :]).astype(BF16), w1_ref[l])
            acc_b += _dot((rows + pe_ref[half + l:half + l + 1, :]).astype(BF16), w1_ref[half + l])
        pre = acc_a + pltpu.roll(acc_b, N_CMP_PAD - 1, axis=0)
        return jax.nn.gelu(pre, approximate=True).astype(BF16)

    kc = _dot(hidden(kc_ref, pek_ref, w1k_ref), w2k_ref[...])
    vc = _dot(hidden(vc_ref, pev_ref, w1v_ref), w2v_ref[...])
    lane = lax.broadcasted_iota(I32, kc.shape, 1)
    row = lax.broadcasted_iota(I32, kc.shape, 0)
    kc = jnp.where((lane & (LANES - 1)) < HEAD_DIM, kc, _key_extras(2 * CMP_STRIDE * row + CMP_BLOCK - 1, lane, False))
    kca_ref[0] = jnp.where(row < N_CMP_PAD - 1, kc, 0.0).astype(BF16)
    vc2_ref[0] = jnp.where(row[:, :LANES] < N_CMP_PAD - 1, vc, 0.0).astype(BF16)


def _compress(kc, vc, pek, pev, w1k, w1v, w2k, w2v):
    n = kc.shape[0]
    b = n // SEQ
    full = lambda a: pl.BlockSpec(a.shape, lambda i: (0,) * a.ndim)
    return pl.pallas_call(
        _compress_kernel,
        grid=(b,),
        in_specs=[
            pl.BlockSpec((SEQ, LANES), lambda i: (i, 0)),
            pl.BlockSpec((SEQ, LANES), lambda i: (i, 0)),
            full(pek), full(pev), full(w1k), full(w1v), full(w2k), full(w2v),
        ],
        out_specs=[pl.BlockSpec((1, N_CMP_PAD, 2 * LANES), lambda i: (i, 0, 0)),
                   pl.BlockSpec((1, N_CMP_PAD, LANES), lambda i: (i, 0, 0))],
        out_shape=[jax.ShapeDtypeStruct((b, N_CMP_PAD, 2 * LANES), BF16),
                   jax.ShapeDtypeStruct((b, N_CMP_PAD, LANES), BF16)],
        compiler_params=_params(("parallel",), 32 << 20),
        name="compress",
    )(kc, vc, pek, pev, w1k, w1v, w2k, w2v)


def _attend(q4, k, v, chunk_masks):
    s_all = _dot_nt(q4, k)
    p_parts, inv = [], []
    for r in range(NSA_REP):
        cols = []
        for c, allowed in enumerate(chunk_masks):
            s = s_all[r * Q_BLOCK:(r + 1) * Q_BLOCK, c * KV_CHUNK:(c + 1) * KV_CHUNK]
            cols.append(s if allowed is None else jnp.where(allowed, s, NEG_INF))
        s = cols[0] if len(cols) == 1 else jnp.concatenate(cols, axis=1)
        p = jnp.exp(s - jnp.max(s, axis=-1, keepdims=True))
        inv.append(1.0 / jnp.sum(p, axis=-1, keepdims=True))
        p_parts.append(p.astype(BF16))
    pv = _dot(jnp.concatenate(p_parts, axis=0), v)
    return [pv[r * Q_BLOCK:(r + 1) * Q_BLOCK] * inv[r] for r in range(NSA_REP)]


def _nsa_kernel(q_ref, misc_ref, kca_ref, vc2_ref, ksa_ref, vs_ref, kwa_ref, vw_ref, o_ref):
    for cls in range(N_QB // QB_PER_CLASS):
        @pl.when(pl.program_id(2) == cls)
        def _(cls=cls):
            _nsa_body(cls, q_ref, misc_ref, kca_ref, vc2_ref, ksa_ref, vs_ref, kwa_ref, vw_ref, o_ref)


def _nsa_body(cls, q_ref, misc_ref, kca_ref, vc2_ref, ksa_ref, vs_ref, kwa_ref, vw_ref, o_ref):
    g = pl.program_id(1)
    qb = cls * QB_PER_CLASS + pl.program_id(3)
    t0 = qb * Q_BLOCK
    slopes = [jnp.where(g == 0, 2.0 ** -(r + 1), 2.0 ** -(r + 1 + NSA_REP)).astype(F32)
              for r in range(NSA_REP)]

    lane = lax.broadcasted_iota(I32, (Q_BLOCK, LANES), 1)
    sub = lax.broadcasted_iota(I32, (Q_BLOCK, LANES), 0)
    low = lane < HEAD_DIM
    tq = t0 + sub

    qf = q_ref[...].astype(F32)
    q_tiles = []
    for r in range(NSA_REP):
        tile = qf[:, (r // 2) * LANES:(r // 2 + 1) * LANES]
        if r % 2:
            tile = pltpu.roll(tile, HEAD_DIM, axis=1)
        q_tiles.append(jnp.where(low, tile, 0.0))

    def stack_queries(slope_scale, sel_lanes):
        parts = []
        for r in range(NSA_REP):
            ext = jnp.where(lane == X_HI, POS_SPLIT * slope_scale * slopes[r],
                            jnp.where(lane == X_LO, slope_scale * slopes[r], sel_lanes))
            parts.append(jnp.where(low, q_tiles[r], ext).astype(BF16))
        return jnp.concatenate(parts, axis=0)

    kca = kca_ref[0]
    s_c = _dot_nt(stack_queries(0.5, 0.0), kca)
    valid_c = lane * CMP_STRIDE + (CMP_BLOCK - 1) <= tq
    p_sum = jnp.zeros((Q_BLOCK, LANES), F32)
    p_parts = []
    for r in range(NSA_REP):
        s = jnp.where(valid_c, s_c[r * Q_BLOCK:(r + 1) * Q_BLOCK], NEG_INF)
        e = jnp.exp(s - jnp.max(s, axis=-1, keepdims=True))
        p = jnp.where(valid_c, e / jnp.sum(e, axis=-1, keepdims=True), 0.0)
        p_sum += p
        p_parts.append(p.astype(BF16))
    o_cmp = _dot(jnp.concatenate(p_parts, axis=0), vc2_ref[0])

    ov_t = ((lane < 4 * sub + 4) & (lane > 4 * sub - 2) & (sub < N_SEL)).astype(F32)
    imp_t = _dot_nt(ov_t, p_sum, precision=lax.Precision.HIGHEST)[:N_SEL]
    j_blk = lax.broadcasted_iota(I32, (N_SEL, LANES), 0)
    cur = (t0 + lax.broadcasted_iota(I32, (N_SEL, LANES), 1)) >> SEL_SHIFT
    forced = (j_blk == 0) | (j_blk == cur) | (j_blk == cur - 1)
    imp_t = jnp.where(forced, FORCE, imp_t)
    imp_t = jnp.where(j_blk <= cur, imp_t, -FORCE)
    rank = jnp.zeros((N_SEL, LANES), I32)
    for jp in range(N_SEL):
        row = imp_t[jp:jp + 1, :]
        beats = (row > imp_t) | ((row == imp_t) & (j_blk > jp))
        rank += beats.astype(I32)
    drop_t = jnp.where(rank < SEL_TOPK, 0.0, NEG_INF)
    drop_q = jnp.concatenate([drop_t, jnp.zeros((LANES - N_SEL, LANES), F32)], axis=0).T
    drop_q = pltpu.roll(drop_q, X_SEL, axis=1)

    q4_plain = stack_queries(1.0, 0.0)
    q4_sel = stack_queries(1.0, drop_q)

    def dist_to(c):
        return (t0 + lax.broadcasted_iota(I32, (Q_BLOCK, KV_CHUNK), 0)
                - (c * KV_CHUNK + lax.broadcasted_iota(I32, (Q_BLOCK, KV_CHUNK), 1)))

    c_diag = cls
    dist_diag = dist_to(c_diag)
    n_slc = (c_diag + 1) * KV_CHUNK
    o_slc = _attend(q4_sel, ksa_ref[:n_slc, :], vs_ref[:n_slc, :], [None] * c_diag + [dist_diag >= 0])

    c_first = max(c_diag - WINDOW // KV_CHUNK, 0)
    masks = [None] * (c_diag - c_first + 1)
    masks[-1] = dist_diag >= 0
    if c_diag - c_first == WINDOW // KV_CHUNK:
        masks[0] = dist_to(c_first) < WINDOW
    win_rows = slice(c_first * KV_CHUNK, n_slc)
    o_win = _attend(q4_plain, kwa_ref[win_rows, :], vw_ref[win_rows, :], masks)

    gates = jax.nn.sigmoid(misc_ref[...])
    heads = []
    for r in range(NSA_REP):
        heads.append(gates[:, 3 * r:3 * r + 1] * o_cmp[r * Q_BLOCK:(r + 1) * Q_BLOCK]
                     + gates[:, 3 * r + 1:3 * r + 2] * o_slc[r]
                     + gates[:, 3 * r + 2:3 * r + 3] * o_win[r])
    for pair in range(NSA_REP // 2):
        even, odd = heads[2 * pair], heads[2 * pair + 1]
        left = jnp.where(g == 0, even, pltpu.roll(even, HEAD_DIM, axis=1))
        right = jnp.where(g == 0, pltpu.roll(odd, HEAD_DIM, axis=1), odd)
        o_ref[:, pair * LANES:(pair + 1) * LANES] = jnp.where(low, left, right).astype(BF16)


def _nsa(q, misc, kca, vc2, ksa, vs, kwa, vw):
    n = q.shape[0]
    b = n // SEQ
    gw = NSA_REP * HEAD_DIM
    assert WINDOW % KV_CHUNK == 0 and KV_CHUNK % Q_BLOCK == 0
    qrow = lambda i, g, c, j: i * N_QB + c * QB_PER_CLASS + j
    return pl.pallas_call(
        _nsa_kernel,
        grid=(b, NSA_GROUPS, N_QB // QB_PER_CLASS, QB_PER_CLASS),
        in_specs=[
            pl.BlockSpec((Q_BLOCK, gw), lambda i, g, c, j: (qrow(i, g, c, j), g)),
            pl.BlockSpec((Q_BLOCK, LANES), lambda i, g, c, j: (qrow(i, g, c, j), g)),
            pl.BlockSpec((1, N_CMP_PAD, LANES), lambda i, g, c, j: (i, 0, g)),
            pl.BlockSpec((1, N_CMP_PAD, LANES), lambda i, g, c, j: (i, 0, 0)),
            pl.BlockSpec((SEQ, LANES), lambda i, g, c, j: (i, g)),
            pl.BlockSpec((SEQ, LANES), lambda i, g, c, j: (i, 0)),
            pl.BlockSpec((SEQ, LANES), lambda i, g, c, j: (i, g)),
            pl.BlockSpec((SEQ, LANES), lambda i, g, c, j: (i, 0)),
        ],
        out_specs=pl.BlockSpec((Q_BLOCK, gw), lambda i, g, c, j: (qrow(i, g, c, j), g)),
        out_shape=jax.ShapeDtypeStruct((n, NSA_GROUPS * gw), BF16),
        compiler_params=_params(("parallel", "parallel", "arbitrary", "arbitrary"), 48 << 20),
        name="nsa",
    )(q, misc, kca, vc2, ksa, vs, kwa, vw)


def _gla_kernel(q_ref, k_ref, v_ref, r_ref, misc_ref, wa_ref, ba_ref, gn_ref, o_ref):
    c = GLA_CHUNK
    lane = lax.broadcasted_iota(I32, (c, LANES), 1)
    tri = lax.broadcasted_iota(I32, (c, c), 0) >= lax.broadcasted_iota(I32, (c, c), 1)
    tri_f = tri.astype(F32)
    lane_s = lax.broadcasted_iota(I32, (GLA_DV, LANES), 1)
    hi = lax.Precision.HIGHEST

    def body(n, st):
        rows = pl.ds(pl.multiple_of(n * c, c), c)
        la = _dot(misc_ref[rows, :], wa_ref[...], precision=hi) + ba_ref[...]
        log_a = (jnp.minimum(la, 0.0) - jnp.log1p(jnp.exp(-jnp.abs(la)))) * (1.0 / GLA_TAU)
        b = _dot(tri_f, log_a, precision=hi)
        b_last = b[c - 1:c, :]
        q = q_ref[rows, :] * (GLA_DK ** -0.5)
        k = k_ref[rows, :]
        q_in = q * jnp.exp(b)
        k_in = (k * jnp.exp(-b)).astype(BF16)
        k_st = (k * jnp.exp(b_last - b)).astype(BF16)
        st_b = st.astype(BF16)
        upd = []
        for h in range(2):
            mh = (lane < GLA_DK) if h == 0 else (lane >= GLA_DK)
            qh = jnp.where(mh, q_in, 0.0).astype(BF16)
            a = jnp.where(tri, _dot_nt(qh, k_in), 0.0)
            vh = v_ref[rows, h * GLA_DV:(h + 1) * GLA_DV]
            o = _dot(a.astype(BF16), vh) + _dot_nt(qh, st_b)
            upd.append(_dot_tn(vh, k_st))
            y = _rms(o, gn_ref[...])
            gate = r_ref[rows, h * GLA_DV:(h + 1) * GLA_DV]
            o_ref[rows, h * GLA_DV:(h + 1) * GLA_DV] = (y * (gate * jax.nn.sigmoid(gate))).astype(BF16)
        return st * jnp.exp(b_last) + jnp.where(lane_s < GLA_DK, upd[0], upd[1])

    lax.fori_loop(0, SEQ // c, body, jnp.zeros((GLA_DV, LANES), F32))


def _gla(gq, gk, gv, gr, misc, wa, ba, gn):
    n = gq.shape[0]
    b = n // SEQ
    return pl.pallas_call(
        _gla_kernel,
        grid=(b, GLA_HEADS // 2),
        in_specs=[
            pl.BlockSpec((SEQ, LANES), lambda i, p: (i, p)),
            pl.BlockSpec((SEQ, LANES), lambda i, p: (i, p)),
            pl.BlockSpec((SEQ, 2 * GLA_DV), lambda i, p: (i, p)),
            pl.BlockSpec((SEQ, 2 * GLA_DV), lambda i, p: (i, p)),
            pl.BlockSpec((SEQ, LANES), lambda i, p: (i, 0)),
            pl.BlockSpec((LANES, LANES), lambda i, p: (0, p)),
            pl.BlockSpec((1, LANES), lambda i, p: (0, p)),
            pl.BlockSpec((1, GLA_DV), lambda i, p: (0, 0)),
        ],
        out_specs=pl.BlockSpec((SEQ, 2 * GLA_DV), lambda i, p: (i, p)),
        out_shape=jax.ShapeDtypeStruct((n, GLA_HEADS * GLA_DV), BF16),
        compiler_params=_params(("parallel", "parallel"), 40 << 20),
        name="gla",
    )(gq, gk, gv, gr, misc, wa, ba, gn)


def _outproj_kernel(x_ref, a_ref, b_ref, wa_ref, wb_ref, o_ref):
    o_ref[...] = x_ref[...] + _dot(a_ref[...], wa_ref[...]) + _dot(b_ref[...], wb_ref[...])


def _outproj(x, o_nsa, o_gla, w_a, w_b, tm=512):
    n, d = x.shape
    ka, kb = o_nsa.shape[1], o_gla.shape[1]
    vmem = 2 * 2 * tm * d * 4 + 2 * tm * (ka + kb) * 2 + 2 * (ka + kb) * d * 2 + (8 << 20)
    return pl.pallas_call(
        _outproj_kernel,
        grid=(n // tm,),
        in_specs=[
            pl.BlockSpec((tm, d), lambda i: (i, 0)),
            pl.BlockSpec((tm, ka), lambda i: (i, 0)),
            pl.BlockSpec((tm, kb), lambda i: (i, 0)),
            pl.BlockSpec((ka, d), lambda i: (0, 0)),
            pl.BlockSpec((kb, d), lambda i: (0, 0)),
        ],
        out_specs=pl.BlockSpec((tm, d), lambda i: (i, 0)),
        out_shape=jax.ShapeDtypeStruct((n, d), F32),
        compiler_params=_params(("parallel",), vmem),
        name="outproj",
    )(x, o_nsa, o_gla, w_a, w_b)


def _permute_w_in(w):
    d = w.shape[0]
    col = lambda lo, hi: w[:, lo:hi]
    zeros = lambda k: jnp.zeros((d, k), w.dtype)
    hd = HEAD_DIM

    def keys_padded(k0):
        return [col(k0, k0 + hd), zeros(LANES - hd), col(k0 + hd, k0 + 2 * hd), zeros(LANES - hd)]

    n_gate = 3 * NSA_REP
    misc = [col(1280, 1280 + n_gate), col(2840, 2856), zeros(LANES - n_gate - GLA_LOWRANK),
            col(1280 + n_gate, 1304), zeros(LANES - n_gate)]
    parts = ([col(0, 512), col(512, 640), col(640, 768)] + keys_padded(768) + keys_padded(1024)
             + [col(896, 1024), col(1152, 1280)] + misc
             + [col(1304, 1560), col(1560, 1816), col(1816, 2328), col(2328, 2840)])
    out = jnp.concatenate(parts, axis=1)
    assert out.shape[1] == D_IN_PAD
    return out.astype(BF16)


def _compress_weights(pe, w1, w2, group_stride):
    hd, hid = HEAD_DIM, CMP_HIDDEN
    pe2 = jnp.concatenate([pe, pe], axis=1)
    w1 = w1.reshape(CMP_BLOCK, hd, hid)
    z1 = jnp.zeros_like(w1)
    w1e = jnp.concatenate([jnp.concatenate([w1, z1], axis=2), jnp.concatenate([z1, w1], axis=2)], axis=1)
    w2e = jnp.zeros((NSA_GROUPS * hid, NSA_GROUPS * group_stride), w2.dtype)
    for g in range(NSA_GROUPS):
        w2e = w2e.at[g * hid:(g + 1) * hid, g * group_stride:g * group_stride + hd].set(w2)
    return pe2, w1e.astype(BF16), w2e.astype(BF16)


def kernel(x, ffn1_norm, ffn1_w_gate, ffn1_w_up, ffn1_w_down, mix_norm, w_in, nsa_pe_k, nsa_w1_k, nsa_w2_k,
           nsa_pe_v, nsa_w1_v, nsa_w2_v, gla_w_a2, gla_b_a, gla_norm, w_out, ffn2_norm, ffn2_w_gate,
           ffn2_w_up, ffn2_w_down, final_norm):
    bsz, seq, d = x.shape
    assert (seq, d) == (SEQ, D_MODEL) and ffn1_norm.shape[0] == 1
    n = bsz * seq
    xf = x.reshape(n, d)
    row = lambda v: v.reshape(1, -1).astype(F32)
    bf = lambda w: w.astype(BF16)
    ones = jnp.ones((1, d), F32)

    x1 = _ffn(xf, row(ffn1_norm[0]), bf(ffn1_w_gate[0]), bf(ffn1_w_up[0]), bf(ffn1_w_down[0]), ones,
              final_norm=False)

    q, kc, vc, ksa, kwa, vs, vw, misc, gq, gk, gv, gr = _inproj(x1, row(mix_norm[0]), _permute_w_in(w_in[0]))

    pek, w1k, w2k = _compress_weights(nsa_pe_k[0], nsa_w1_k[0], nsa_w2_k[0], LANES)
    pev, w1v, w2v = _compress_weights(nsa_pe_v[0], nsa_w1_v[0], nsa_w2_v[0], HEAD_DIM)
    kca, vc2 = _compress(kc, vc, pek, pev, w1k, w1v, w2k, w2v)

    o_nsa = _nsa(q, misc, kca, vc2, ksa, vs, kwa, vw)

    wa = jnp.zeros((LANES, GLA_HEADS * GLA_DK), F32).at[MISC_GA_OFF:MISC_GA_OFF + GLA_LOWRANK].set(gla_w_a2[0])
    o_gla = _gla(gq, gk, gv, gr, misc, wa, row(gla_b_a[0]), row(gla_norm[0]))

    d_nsa = NSA_HEADS * HEAD_DIM
    x2 = _outproj(x1, o_nsa, o_gla, bf(w_out[0][:d_nsa]), bf(w_out[0][d_nsa:]))

    out = _ffn(x2, row(ffn2_norm[0]), bf(ffn2_w_gate[0]), bf(ffn2_w_up[0]), bf(ffn2_w_down[0]),
               row(final_norm), final_norm=True)
    return out.reshape(bsz, seq, d)
```

```python
import functools

import jax
import jax.numpy as jnp
from jax import lax
from jax.experimental import pallas as pl
from jax.experimental.pallas import tpu as pltpu

F32 = jnp.float32
BF16 = jnp.bfloat16
I32 = jnp.int32

D_MODEL = 1024
SEQ = 2048
D_FF = 2816
EPS = 1e-6
NEG_INF = -1e30
FORCE = 1e9

NSA_HEADS = 8
NSA_GROUPS = 2
NSA_REP = NSA_HEADS // NSA_GROUPS
HEAD_DIM = 64
CMP_BLOCK = 32
CMP_STRIDE = 16
CMP_HIDDEN = 128
N_CMP_PAD = SEQ // CMP_STRIDE
SEL_BLOCK = 64
SEL_SHIFT = 6
assert 1 << SEL_SHIFT == SEL_BLOCK
N_SEL = SEQ // SEL_BLOCK
SEL_TOPK = 8
WINDOW = 512
Q_BLOCK = 128
N_QB = SEQ // Q_BLOCK
KV_CHUNK = 256
QB_PER_CLASS = KV_CHUNK // Q_BLOCK

GLA_HEADS = 4
GLA_DK = 64
GLA_DV = 128
GLA_CHUNK = 64
GLA_CHUNK_SHIFT = 6
assert 1 << GLA_CHUNK_SHIFT == GLA_CHUNK
GLA_BLOCK = 256
GLA_LOWRANK = 16
GLA_TAU = 16.0

LANES = 128
V7X_VMEM_BYTES = 64 * 1024 * 1024

C_Q = (0, 512)
C_KC = (512, 640)
C_VC = (640, 768)
C_KSA = (768, 1024)
C_KWA = (1024, 1280)
C_VS = (1280, 1408)
C_VW = (1408, 1536)
C_MISC = (1536, 1792)
C_GQ = (1792, 2048)
C_GK = (2048, 2304)
C_GV = (2304, 2816)
C_GR = (2816, 3328)
D_IN_PAD = 3328
MISC_GA_OFF = 12

X_SEL = HEAD_DIM
X_HI = HEAD_DIM + N_SEL
X_LO = X_HI + 1
POS_SPLIT = 64
POS_SHIFT = 6
assert 1 << POS_SHIFT == POS_SPLIT and X_LO < LANES


def _params(sem, vmem_bytes):
    return pltpu.CompilerParams(dimension_semantics=sem,
                                vmem_limit_bytes=min(int(vmem_bytes), V7X_VMEM_BYTES - (8 << 20)))


def _rms(x, g):
    return x * lax.rsqrt(jnp.mean(x * x, axis=-1, keepdims=True) + EPS) * g


def _dot(a, b, **kw):
    return jnp.dot(a, b, preferred_element_type=F32, **kw)


def _dot_nt(a, b, **kw):
    return lax.dot_general(a, b, (((1,), (1,)), ((), ())), preferred_element_type=F32, **kw)


def _dot_tn(a, b, **kw):
    return lax.dot_general(a, b, (((0,), (0,)), ((), ())), preferred_element_type=F32, **kw)


def _ffn_kernel(x_ref, g_ref, wg_ref, wu_ref, wd_ref, fg_ref, o_ref, h_ref, acc_ref, *, n_f, final_norm):
    f = pl.program_id(1)

    @pl.when(f == 0)
    def _():
        h_ref[...] = _rms(x_ref[...], g_ref[...]).astype(BF16)
        acc_ref[...] = jnp.zeros_like(acc_ref)

    h = h_ref[...]
    gate = _dot(h, wg_ref[...])
    up = _dot(h, wu_ref[...])
    act = (gate * jax.nn.sigmoid(gate)) * up
    acc_ref[...] += _dot(act.astype(BF16), wd_ref[...])

    @pl.when(f == n_f - 1)
    def _():
        y = x_ref[...] + 0.5 * acc_ref[...]
        if final_norm:
            y = _rms(y, fg_ref[...])
        o_ref[...] = y


def _ffn(x, gain, wg, wu, wd, final_gain, *, final_norm, tm=512, tf=256):
    n, d = x.shape
    n_f = D_FF // tf
    vmem = 2 * 2 * tm * d * 4 + tm * d * (2 + 4) + 2 * 3 * d * tf * 2 + 3 * tm * tf * 4 + (8 << 20)
    return pl.pallas_call(
        functools.partial(_ffn_kernel, n_f=n_f, final_norm=final_norm),
        grid=(n // tm, n_f),
        in_specs=[
            pl.BlockSpec((tm, d), lambda i, f: (i, 0)),
            pl.BlockSpec((1, d), lambda i, f: (0, 0)),
            pl.BlockSpec((d, tf), lambda i, f: (0, f)),
            pl.BlockSpec((d, tf), lambda i, f: (0, f)),
            pl.BlockSpec((tf, d), lambda i, f: (f, 0)),
            pl.BlockSpec((1, d), lambda i, f: (0, 0)),
        ],
        out_specs=pl.BlockSpec((tm, d), lambda i, f: (i, 0)),
        out_shape=jax.ShapeDtypeStruct((n, d), F32),
        scratch_shapes=[pltpu.VMEM((tm, d), BF16), pltpu.VMEM((tm, d), F32)],
        compiler_params=_params(("parallel", "arbitrary"), vmem),
        name="ffn",
    )(x, gain, wg, wu, wd, final_gain)


def _key_extras(pos, lane, with_block):
    lg = lane & (LANES - 1)
    ext = jnp.where(lg == X_HI, (pos >> POS_SHIFT).astype(F32),
                    jnp.where(lg == X_LO, (pos & (POS_SPLIT - 1)).astype(F32), 0.0))
    if with_block:
        ext = jnp.where((lg >= X_SEL) & (lg < X_HI) & ((pos >> SEL_SHIFT) == lg - X_SEL), 1.0, ext)
    return ext


def _inproj_kernel(x_ref, g_ref, w_ref, q_ref, kc_ref, vc_ref, ksa_ref, kwa_ref, vs_ref, vw_ref, misc_ref,
                   gq_ref, gk_ref, gv_ref, gr_ref, *, tm):
    h = _rms(x_ref[...], g_ref[...]).astype(BF16)

    def proj(c):
        return _dot(h, w_ref[:, c[0]:c[1]])

    shape = (tm, 2 * LANES)
    lane = lax.broadcasted_iota(I32, shape, 1)
    pos = (pl.program_id(0) * tm + lax.broadcasted_iota(I32, shape, 0)) & (SEQ - 1)
    is_key = (lane & (LANES - 1)) < HEAD_DIM

    q_ref[...] = (proj(C_Q) * (HEAD_DIM ** -0.5)).astype(BF16)
    kc_ref[...] = proj(C_KC)
    vc_ref[...] = proj(C_VC)
    ksa_ref[...] = jnp.where(is_key, proj(C_KSA), _key_extras(pos, lane, True)).astype(BF16)
    kwa_ref[...] = jnp.where(is_key, proj(C_KWA), _key_extras(pos, lane, False)).astype(BF16)
    vs_ref[...] = proj(C_VS).astype(BF16)
    vw_ref[...] = proj(C_VW).astype(BF16)
    misc_ref[...] = proj(C_MISC)
    gq_ref[...] = proj(C_GQ)
    gk_ref[...] = proj(C_GK)
    gv_ref[...] = proj(C_GV).astype(BF16)
    gr_ref[...] = proj(C_GR)


def _inproj(x, gain, w, tm=512):
    n, d = x.shape
    assert SEQ % tm == 0 and SEQ & (SEQ - 1) == 0
    outs = [(C_Q, BF16), (C_KC, F32), (C_VC, F32), (C_KSA, BF16), (C_KWA, BF16), (C_VS, BF16), (C_VW, BF16),
            (C_MISC, F32), (C_GQ, F32), (C_GK, F32), (C_GV, BF16), (C_GR, F32)]
    vmem = 2 * tm * d * 4 + 2 * d * D_IN_PAD * 2 + 2 * tm * D_IN_PAD * 4 + (8 << 20)
    return pl.pallas_call(
        functools.partial(_inproj_kernel, tm=tm),
        grid=(n // tm,),
        in_specs=[
            pl.BlockSpec((tm, d), lambda i: (i, 0)),
            pl.BlockSpec((1, d), lambda i: (0, 0)),
            pl.BlockSpec((d, D_IN_PAD), lambda i: (0, 0)),
        ],
        out_specs=[pl.BlockSpec((tm, c[1] - c[0]), lambda i: (i, 0)) for c, _ in outs],
        out_shape=[jax.ShapeDtypeStruct((n, c[1] - c[0]), dt) for c, dt in outs],
        compiler_params=_params(("parallel",), vmem),
        name="inproj",
    )(x, gain, w)


def _compress_kernel(kc_ref, vc_ref, pek_ref, pev_ref, w1k_ref, w1v_ref, w2k_ref, w2v_ref, kca_ref, vc2_ref):
    half = CMP_BLOCK // 2

    def hidden(x_ref, pe_ref, w1_ref):
        acc_a = jnp.zeros((N_CMP_PAD, 2 * CMP_HIDDEN), F32)
        acc_b = jnp.zeros((N_CMP_PAD, 2 * CMP_HIDDEN), F32)
        for l in range(half):
            rows = x_ref[pl.ds(l, N_CMP_PAD, stride=CMP_STRIDE), :]
            acc_a += _dot((rows + pe_ref[l:l + 1, :]).astype(BF16), w1_ref[l])
            acc_b += _dot((rows + pe_ref[half + l:half + l + 1, :]).astype(BF16), w1_ref[half + l])
        pre = acc_a + pltpu.roll(acc_b, N_CMP_PAD - 1, axis=0)
        return jax.nn.gelu(pre, approximate=True).astype(BF16)

    kc = _dot(hidden(kc_ref, pek_ref, w1k_ref), w2k_ref[...])
    vc = _dot(hidden(vc_ref, pev_ref, w1v_ref), w2v_ref[...])
    lane = lax.broadcasted_iota(I32, kc.shape, 1)
    row = lax.broadcasted_iota(I32, kc.shape, 0)
    kc = jnp.where((lane & (LANES - 1)) < HEAD_DIM, kc, _key_extras(2 * CMP_STRIDE * row + CMP_BLOCK - 1, lane, False))
    kca_ref[0] = jnp.where(row < N_CMP_PAD - 1, kc, 0.0).astype(BF16)
    vc2_ref[0] = jnp.where(row[:, :LANES] < N_CMP_PAD - 1, vc, 0.0).astype(BF16)


def _compress(kc, vc, pek, pev, w1k, w1v, w2k, w2v):
    n = kc.shape[0]
    b = n // SEQ
    full = lambda a: pl.BlockSpec(a.shape, lambda i: (0,) * a.ndim)
    return pl.pallas_call(
        _compress_kernel,
        grid=(b,),
        in_specs=[
            pl.BlockSpec((SEQ, LANES), lambda i: (i, 0)),
            pl.BlockSpec((SEQ, LANES), lambda i: (i, 0)),
            full(pek), full(pev), full(w1k), full(w1v), full(w2k), full(w2v),
        ],
        out_specs=[pl.BlockSpec((1, N_CMP_PAD, 2 * LANES), lambda i: (i, 0, 0)),
                   pl.BlockSpec((1, N_CMP_PAD, LANES), lambda i: (i, 0, 0))],
        out_shape=[jax.ShapeDtypeStruct((b, N_CMP_PAD, 2 * LANES), BF16),
                   jax.ShapeDtypeStruct((b, N_CMP_PAD, LANES), BF16)],
        compiler_params=_params(("parallel",), 32 << 20),
        name="compress",
    )(kc, vc, pek, pev, w1k, w1v, w2k, w2v)


def _attend(q4, k, v, chunk_masks):
    s_all = _dot_nt(q4, k)
    p_parts, inv = [], []
    for r in range(NSA_REP):
        cols = []
        for c, allowed in enumerate(chunk_masks):
            s = s_all[r * Q_BLOCK:(r + 1) * Q_BLOCK, c * KV_CHUNK:(c + 1) * KV_CHUNK]
            cols.append(s if allowed is None else jnp.where(allowed, s, NEG_INF))
        s = cols[0] if len(cols) == 1 else jnp.concatenate(cols, axis=1)
        p = jnp.exp(s - jnp.max(s, axis=-1, keepdims=True))
        inv.append(1.0 / jnp.sum(p, axis=-1, keepdims=True))
        p_parts.append(p.astype(BF16))
    pv = _dot(jnp.concatenate(p_parts, axis=0), v)
    return [pv[r * Q_BLOCK:(r + 1) * Q_BLOCK] * inv[r] for r in range(NSA_REP)]


def _nsa_kernel(q_ref, misc_ref, kca_ref, vc2_ref, ksa_ref, vs_ref, kwa_ref, vw_ref, o_ref):
    for cls in range(N_QB // QB_PER_CLASS):
        @pl.when(pl.program_id(2) == cls)
        def _(cls=cls):
            _nsa_body(cls, q_ref, misc_ref, kca_ref, vc2_ref, ksa_ref, vs_ref, kwa_ref, vw_ref, o_ref)


def _nsa_body(cls, q_ref, misc_ref, kca_ref, vc2_ref, ksa_ref, vs_ref, kwa_ref, vw_ref, o_ref):
    g = pl.program_id(1)
    qb = cls * QB_PER_CLASS + pl.program_id(3)
    t0 = qb * Q_BLOCK
    slopes = [jnp.where(g == 0, 2.0 ** -(r + 1), 2.0 ** -(r + 1 + NSA_REP)).astype(F32)
              for r in range(NSA_REP)]

    lane = lax.broadcasted_iota(I32, (Q_BLOCK, LANES), 1)
    sub = lax.broadcasted_iota(I32, (Q_BLOCK, LANES), 0)
    low = lane < HEAD_DIM
    tq = t0 + sub

    qf = q_ref[...].astype(F32)
    q_tiles = []
    for r in range(NSA_REP):
        tile = qf[:, (r // 2) * LANES:(r // 2 + 1) * LANES]
        if r % 2:
            tile = pltpu.roll(tile, HEAD_DIM, axis=1)
        q_tiles.append(jnp.where(low, tile, 0.0))

    def stack_queries(slope_scale, sel_lanes):
        parts = []
        for r in range(NSA_REP):
            ext = jnp.where(lane == X_HI, POS_SPLIT * slope_scale * slopes[r],
                            jnp.where(lane == X_LO, slope_scale * slopes[r], sel_lanes))
            parts.append(jnp.where(low, q_tiles[r], ext).astype(BF16))
        return jnp.concatenate(parts, axis=0)

    kca = kca_ref[0]
    s_c = _dot_nt(stack_queries(0.5, 0.0), kca)
    valid_c = lane * CMP_STRIDE + (CMP_BLOCK - 1) <= tq
    p_sum = jnp.zeros((Q_BLOCK, LANES), F32)
    p_parts = []
    for r in range(NSA_REP):
        s = jnp.where(valid_c, s_c[r * Q_BLOCK:(r + 1) * Q_BLOCK], NEG_INF)
        e = jnp.exp(s - jnp.max(s, axis=-1, keepdims=True))
        p = jnp.where(valid_c, e / jnp.sum(e, axis=-1, keepdims=True), 0.0)
        p_sum += p
        p_parts.append(p.astype(BF16))
    o_cmp = _dot(jnp.concatenate(p_parts, axis=0), vc2_ref[0])

    ov_t = ((lane < 4 * sub + 4) & (lane > 4 * sub - 2) & (sub < N_SEL)).astype(F32)
    imp_t = _dot_nt(ov_t, p_sum, precision=lax.Precision.HIGHEST)[:N_SEL]
    j_blk = lax.broadcasted_iota(I32, (N_SEL, LANES), 0)
    cur = (t0 + lax.broadcasted_iota(I32, (N_SEL, LANES), 1)) >> SEL_SHIFT
    forced = (j_blk == 0) | (j_blk == cur) | (j_blk == cur - 1)
    imp_t = jnp.where(forced, FORCE, imp_t)
    imp_t = jnp.where(j_blk <= cur, imp_t, -FORCE)
    rank = jnp.zeros((N_SEL, LANES), I32)
    for jp in range(N_SEL):
        row = imp_t[jp:jp + 1, :]
        beats = (row > imp_t) | ((row == imp_t) & (j_blk > jp))
        rank += beats.astype(I32)
    drop_t = jnp.where(rank < SEL_TOPK, 0.0, NEG_INF)
    drop_q = jnp.concatenate([drop_t, jnp.zeros((LANES - N_SEL, LANES), F32)], axis=0).T
    drop_q = pltpu.roll(drop_q, X_SEL, axis=1)

    q4_plain = stack_queries(1.0, 0.0)
    q4_sel = stack_queries(1.0, drop_q)

    def dist_to(c):
        return (t0 + lax.broadcasted_iota(I32, (Q_BLOCK, KV_CHUNK), 0)
                - (c * KV_CHUNK + lax.broadcasted_iota(I32, (Q_BLOCK, KV_CHUNK), 1)))

    c_diag = cls
    dist_diag = dist_to(c_diag)
    n_slc = (c_diag + 1) * KV_CHUNK
    o_slc = _attend(q4_sel, ksa_ref[:n_slc, :], vs_ref[:n_slc, :], [None] * c_diag + [dist_diag >= 0])

    c_first = max(c_diag - WINDOW // KV_CHUNK, 0)
    masks = [None] * (c_diag - c_first + 1)
    masks[-1] = dist_diag >= 0
    if c_diag - c_first == WINDOW // KV_CHUNK:
        masks[0] = dist_to(c_first) < WINDOW
    win_rows = slice(c_first * KV_CHUNK, n_slc)
    o_win = _attend(q4_plain, kwa_ref[win_rows, :], vw_ref[win_rows, :], masks)

    gates = jax.nn.sigmoid(misc_ref[...])
    heads = []
    for r in range(NSA_REP):
        heads.append(gates[:, 3 * r:3 * r + 1] * o_cmp[r * Q_BLOCK:(r + 1) * Q_BLOCK]
                     + gates[:, 3 * r + 1:3 * r + 2] * o_slc[r]
                     + gates[:, 3 * r + 2:3 * r + 3] * o_win[r])
    for pair in range(NSA_REP // 2):
        even, odd = heads[2 * pair], heads[2 * pair + 1]
        left = jnp.where(g == 0, even, pltpu.roll(even, HEAD_DIM, axis=1))
        right = jnp.where(g == 0, pltpu.roll(odd, HEAD_DIM, axis=1), odd)
        o_ref[:, pair * LANES:(pair + 1) * LANES] = jnp.where(low, left, right).astype(BF16)


def _nsa(q, misc, kca, vc2, ksa, vs, kwa, vw):
    n = q.shape[0]
    b = n // SEQ
    gw = NSA_REP * HEAD_DIM
    assert WINDOW % KV_CHUNK == 0 and KV_CHUNK % Q_BLOCK == 0
    qrow = lambda i, g, c, j: i * N_QB + c * QB_PER_CLASS + j
    return pl.pallas_call(
        _nsa_kernel,
        grid=(b, NSA_GROUPS, N_QB // QB_PER_CLASS, QB_PER_CLASS),
        in_specs=[
            pl.BlockSpec((Q_BLOCK, gw), lambda i, g, c, j: (qrow(i, g, c, j), g)),
            pl.BlockSpec((Q_BLOCK, LANES), lambda i, g, c, j: (qrow(i, g, c, j), g)),
            pl.BlockSpec((1, N_CMP_PAD, LANES), lambda i, g, c, j: (i, 0, g)),
            pl.BlockSpec((1, N_CMP_PAD, LANES), lambda i, g, c, j: (i, 0, 0)),
            pl.BlockSpec((SEQ, LANES), lambda i, g, c, j: (i, g)),
            pl.BlockSpec((SEQ, LANES), lambda i, g, c, j: (i, 0)),
            pl.BlockSpec((SEQ, LANES), lambda i, g, c, j: (i, g)),
            pl.BlockSpec((SEQ, LANES), lambda i, g, c, j: (i, 0)),
        ],
        out_specs=pl.BlockSpec((Q_BLOCK, gw), lambda i, g, c, j: (qrow(i, g, c, j), g)),
        out_shape=jax.ShapeDtypeStruct((n, NSA_GROUPS * gw), BF16),
        compiler_params=_params(("parallel", "parallel", "arbitrary", "arbitrary"), 48 << 20),
        name="nsa",
    )(q, misc, kca, vc2, ksa, vs, kwa, vw)


def _gla_kernel(q_ref, k_ref, v_ref, r_ref, misc_ref, wa_ref, ba_ref, gn_ref, o_ref):
    c, blk = GLA_CHUNK, GLA_BLOCK
    n_c = blk // c
    lane = lax.broadcasted_iota(I32, (blk, LANES), 1)
    row = lax.broadcasted_iota(I32, (blk, LANES), 0)
    in_chunk = row & (c - 1)
    chunk_of_row = row >> GLA_CHUNK_SHIFT
    r2 = lax.broadcasted_iota(I32, (2 * blk, blk), 0) & (blk - 1)
    c2 = lax.broadcasted_iota(I32, (2 * blk, blk), 1)
    intra = (r2 >= c2) & ((r2 >> GLA_CHUNK_SHIFT) == (c2 >> GLA_CHUNK_SHIFT))
    lane_s = lax.broadcasted_iota(I32, (GLA_DV, LANES), 1)
    st = jnp.zeros((GLA_DV, LANES), F32)

    for i_blk in range(SEQ // blk):
        rows = slice(i_blk * blk, (i_blk + 1) * blk)
        la = _dot(misc_ref[rows, :], wa_ref[...], precision=lax.Precision.HIGHEST) + ba_ref[...]
        b = (jnp.minimum(la, 0.0) - jnp.log1p(jnp.exp(-jnp.abs(la)))) * (1.0 / GLA_TAU)
        shift = 1
        while shift < c:
            b = b + jnp.where(in_chunk >= shift, pltpu.roll(b, shift, axis=0), 0.0)
            shift *= 2
        b3 = b.reshape(n_c, c, LANES)
        b_last = b3[:, c - 1:c, :]
        k = k_ref[rows, :]
        q_in = q_ref[rows, :] * (GLA_DK ** -0.5) * jnp.exp(b)
        k_in = (k * jnp.exp(-b)).astype(BF16)
        k_st = (k.reshape(n_c, c, LANES) * jnp.exp(b_last - b3)).reshape(blk, LANES)
        decay = jnp.exp(b_last)
        v = v_ref[rows, :]

        q2 = jnp.concatenate([jnp.where(lane < GLA_DK, q_in, 0.0), jnp.where(lane >= GLA_DK, q_in, 0.0)],
                             axis=0).astype(BF16)
        a = jnp.where(intra, _dot_nt(q2, k_in), 0.0).astype(BF16)
        o_intra = _dot(a, v)
        k_cols = jnp.concatenate([jnp.where(chunk_of_row == i_c, k_st, 0.0) for i_c in range(n_c)], axis=1)
        inc = _dot_tn(v, k_cols.astype(BF16))

        o_inter = []
        for i_c in range(n_c):
            q_c = jnp.concatenate([q2[i_c * c:(i_c + 1) * c], q2[blk + i_c * c:blk + (i_c + 1) * c]], axis=0)
            o_inter.append(_dot_nt(q_c, st.astype(BF16)))
            cols = slice(i_c * LANES, (i_c + 1) * LANES)
            st = st * decay[i_c] + jnp.where(lane_s < GLA_DK, inc[:GLA_DV, cols], inc[GLA_DV:, cols])

        for h in range(2):
            cols = slice(h * GLA_DV, (h + 1) * GLA_DV)
            o = o_intra[h * blk:(h + 1) * blk, cols] + jnp.concatenate(
                [o_inter[i_c][h * c:(h + 1) * c] for i_c in range(n_c)], axis=0)
            gate = r_ref[rows, cols]
            o_ref[rows, cols] = (_rms(o, gn_ref[...]) * (gate * jax.nn.sigmoid(gate))).astype(BF16)


def _gla(gq, gk, gv, gr, misc, wa, ba, gn):
    n = gq.shape[0]
    b = n // SEQ
    return pl.pallas_call(
        _gla_kernel,
        grid=(b, GLA_HEADS // 2),
        in_specs=[
            pl.BlockSpec((SEQ, LANES), lambda i, p: (i, p)),
            pl.BlockSpec((SEQ, LANES), lambda i, p: (i, p)),
            pl.BlockSpec((SEQ, 2 * GLA_DV), lambda i, p: (i, p)),
            pl.BlockSpec((SEQ, 2 * GLA_DV), lambda i, p: (i, p)),
            pl.BlockSpec((SEQ, LANES), lambda i, p: (i, 0)),
            pl.BlockSpec((LANES, LANES), lambda i, p: (0, p)),
            pl.BlockSpec((1, LANES), lambda i, p: (0, p)),
            pl.BlockSpec((1, GLA_DV), lambda i, p: (0, 0)),
        ],
        out_specs=pl.BlockSpec((SEQ, 2 * GLA_DV), lambda i, p: (i, p)),
        out_shape=jax.ShapeDtypeStruct((n, GLA_HEADS * GLA_DV), BF16),
        compiler_params=_params(("parallel", "parallel"), 40 << 20),
        name="gla",
    )(gq, gk, gv, gr, misc, wa, ba, gn)


def _outproj_kernel(x_ref, a_ref, b_ref, wa_ref, wb_ref, o_ref):
    o_ref[...] = x_ref[...] + _dot(a_ref[...], wa_ref[...]) + _dot(b_ref[...], wb_ref[...])


def _outproj(x, o_nsa, o_gla, w_a, w_b, tm=512):
    n, d = x.shape
    ka, kb = o_nsa.shape[1], o_gla.shape[1]
    vmem = 2 * 2 * tm * d * 4 + 2 * tm * (ka + kb) * 2 + 2 * (ka + kb) * d * 2 + (8 << 20)
    return pl.pallas_call(
        _outproj_kernel,
        grid=(n // tm,),
        in_specs=[
            pl.BlockSpec((tm, d), lambda i: (i, 0)),
            pl.BlockSpec((tm, ka), lambda i: (i, 0)),
            pl.BlockSpec((tm, kb), lambda i: (i, 0)),
            pl.BlockSpec((ka, d), lambda i: (0, 0)),
            pl.BlockSpec((kb, d), lambda i: (0, 0)),
        ],
        out_specs=pl.BlockSpec((tm, d), lambda i: (i, 0)),
        out_shape=jax.ShapeDtypeStruct((n, d), F32),
        compiler_params=_params(("parallel",), vmem),
        name="outproj",
    )(x, o_nsa, o_gla, w_a, w_b)


def _permute_w_in(w):
    d = w.shape[0]
    col = lambda lo, hi: w[:, lo:hi]
    zeros = lambda k: jnp.zeros((d, k), w.dtype)
    hd = HEAD_DIM

    def keys_padded(k0):
        return [col(k0, k0 + hd), zeros(LANES - hd), col(k0 + hd, k0 + 2 * hd), zeros(LANES - hd)]

    n_gate = 3 * NSA_REP
    misc = [col(1280, 1280 + n_gate), col(2840, 2856), zeros(LANES - n_gate - GLA_LOWRANK),
            col(1280 + n_gate, 1304), zeros(LANES - n_gate)]
    parts = ([col(0, 512), col(512, 640), col(640, 768)] + keys_padded(768) + keys_padded(1024)
             + [col(896, 1024), col(1152, 1280)] + misc
             + [col(1304, 1560), col(1560, 1816), col(1816, 2328), col(2328, 2840)])
    out = jnp.concatenate(parts, axis=1)
    assert out.shape[1] == D_IN_PAD
    return out.astype(BF16)


def _compress_weights(pe, w1, w2, group_stride):
    hd, hid = HEAD_DIM, CMP_HIDDEN
    pe2 = jnp.concatenate([pe, pe], axis=1)
    w1 = w1.reshape(CMP_BLOCK, hd, hid)
    z1 = jnp.zeros_like(w1)
    w1e = jnp.concatenate([jnp.concatenate([w1, z1], axis=2), jnp.concatenate([z1, w1], axis=2)], axis=1)
    w2e = jnp.zeros((NSA_GROUPS * hid, NSA_GROUPS * group_stride), w2.dtype)
    for g in range(NSA_GROUPS):
        w2e = w2e.at[g * hid:(g + 1) * hid, g * group_stride:g * group_stride + hd].set(w2)
    return pe2, w1e.astype(BF16), w2e.astype(BF16)


def kernel(x, ffn1_norm, ffn1_w_gate, ffn1_w_up, ffn1_w_down, mix_norm, w_in, nsa_pe_k, nsa_w1_k, nsa_w2_k,
           nsa_pe_v, nsa_w1_v, nsa_w2_v, gla_w_a2, gla_b_a, gla_norm, w_out, ffn2_norm, ffn2_w_gate,
           ffn2_w_up, ffn2_w_down, final_norm):
    bsz, seq, d = x.shape
    assert (seq, d) == (SEQ, D_MODEL) and ffn1_norm.shape[0] == 1
    n = bsz * seq
    xf = x.reshape(n, d)
    row = lambda v: v.reshape(1, -1).astype(F32)
    bf = lambda w: w.astype(BF16)
    ones = jnp.ones((1, d), F32)

    x1 = _ffn(xf, row(ffn1_norm[0]), bf(ffn1_w_gate[0]), bf(ffn1_w_up[0]), bf(ffn1_w_down[0]), ones,
              final_norm=False)

    q, kc, vc, ksa, kwa, vs, vw, misc, gq, gk, gv, gr = _inproj(x1, row(mix_norm[0]), _permute_w_in(w_in[0]))

    pek, w1k, w2k = _compress_weights(nsa_pe_k[0], nsa_w1_k[0], nsa_w2_k[0], LANES)
    pev, w1v, w2v = _compress_weights(nsa_pe_v[0], nsa_w1_v[0], nsa_w2_v[0], HEAD_DIM)
    kca, vc2 = _compress(kc, vc, pek, pev, w1k, w1v, w2k, w2v)

    o_nsa = _nsa(q, misc, kca, vc2, ksa, vs, kwa, vw)

    wa = jnp.zeros((LANES, GLA_HEADS * GLA_DK), F32).at[MISC_GA_OFF:MISC_GA_OFF + GLA_LOWRANK].set(gla_w_a2[0])
    o_gla = _gla(gq, gk, gv, gr, misc, wa, row(gla_b_a[0]), row(gla_norm[0]))

    d_nsa = NSA_HEADS * HEAD_DIM
    x2 = _outproj(x1, o_nsa, o_gla, bf(w_out[0][:d_nsa]), bf(w_out[0][d_nsa:]))

    out = _ffn(x2, row(ffn2_norm[0]), bf(ffn2_w_gate[0]), bf(ffn2_w_up[0]), bf(ffn2_w_down[0]),
               row(final_norm), final_norm=True)
    return out.reshape(bsz, seq, d)
```

```python
import functools

import jax
import jax.numpy as jnp
from jax import lax
from jax.experimental import pallas as pl
from jax.experimental.pallas import tpu as pltpu

F32 = jnp.float32
BF16 = jnp.bfloat16
I32 = jnp.int32

D_MODEL = 1024
SEQ = 2048
D_FF = 2816
EPS = 1e-6
NEG_INF = -1e30
FORCE = 1e9

NSA_HEADS = 8
NSA_GROUPS = 2
NSA_REP = NSA_HEADS // NSA_GROUPS
HEAD_DIM = 64
CMP_BLOCK = 32
CMP_STRIDE = 16
CMP_HIDDEN = 128
N_CMP_PAD = SEQ // CMP_STRIDE
SEL_BLOCK = 64
SEL_SHIFT = 6
assert 1 << SEL_SHIFT == SEL_BLOCK
N_SEL = SEQ // SEL_BLOCK
SEL_TOPK = 8
WINDOW = 512
Q_BLOCK = 128
N_QB = SEQ // Q_BLOCK
KV_CHUNK = 256
QB_PER_CLASS = KV_CHUNK // Q_BLOCK

GLA_HEADS = 4
GLA_DK = 64
GLA_DV = 128
GLA_CHUNK = 64
GLA_CHUNK_SHIFT = 6
assert 1 << GLA_CHUNK_SHIFT == GLA_CHUNK
GLA_BLOCK = 256
GLA_LOWRANK = 16
GLA_TAU = 16.0

LANES = 128
V7X_VMEM_BYTES = 64 * 1024 * 1024

C_Q = (0, 512)
C_KC = (512, 640)
C_VC = (640, 768)
C_KSA = (768, 1024)
C_KWA = (1024, 1280)
C_VS = (1280, 1408)
C_VW = (1408, 1536)
C_MISC = (1536, 1792)
C_GQ = (1792, 2048)
C_GK = (2048, 2304)
C_GV = (2304, 2816)
C_GR = (2816, 3328)
D_IN_PAD = 3328
MISC_GA_OFF = 12

X_SEL = HEAD_DIM
X_HI = HEAD_DIM + N_SEL
X_LO = X_HI + 1
POS_SPLIT = 64
POS_SHIFT = 6
assert 1 << POS_SHIFT == POS_SPLIT and X_LO < LANES


def _params(sem, vmem_bytes):
    return pltpu.CompilerParams(dimension_semantics=sem,
                                vmem_limit_bytes=min(int(vmem_bytes), V7X_VMEM_BYTES - (8 << 20)))


def _rms(x, g):
    return x * lax.rsqrt(jnp.mean(x * x, axis=-1, keepdims=True) + EPS) * g


def _dot(a, b, **kw):
    return jnp.dot(a, b, preferred_element_type=F32, **kw)


def _dot_nt(a, b, **kw):
    return lax.dot_general(a, b, (((1,), (1,)), ((), ())), preferred_element_type=F32, **kw)


def _dot_tn(a, b, **kw):
    return lax.dot_general(a, b, (((0,), (0,)), ((), ())), preferred_element_type=F32, **kw)


def _ffn_kernel(x_ref, g_ref, wg_ref, wu_ref, wd_ref, fg_ref, o_ref, act_ref, *, tf, final_norm):
    x = x_ref[...]
    h = _rms(x, g_ref[...]).astype(BF16)
    for c in range(D_FF // tf):
        cols = slice(c * tf, (c + 1) * tf)
        gate = _dot(h, wg_ref[:, cols])
        up = _dot(h, wu_ref[:, cols])
        act_ref[:, cols] = ((gate * jax.nn.sigmoid(gate)) * up).astype(BF16)
    y = x + 0.5 * _dot(act_ref[...], wd_ref[...])
    if final_norm:
        y = _rms(y, fg_ref[...])
    o_ref[...] = y


def _ffn(x, gain, wg, wu, wd, final_gain, *, final_norm, tm=512, tf=256):
    n, d = x.shape
    assert D_FF % tf == 0
    resident = lambda shape: pl.BlockSpec(shape, lambda i: (0, 0), pipeline_mode=pl.Buffered(1))
    vmem = 2 * 2 * tm * d * 4 + 3 * d * D_FF * 2 + tm * D_FF * 2 + tm * d * 2 + 4 * tm * tf * 4 + (8 << 20)
    return pl.pallas_call(
        functools.partial(_ffn_kernel, tf=tf, final_norm=final_norm),
        grid=(n // tm,),
        in_specs=[
            pl.BlockSpec((tm, d), lambda i: (i, 0)),
            resident((1, d)),
            resident((d, D_FF)),
            resident((d, D_FF)),
            resident((D_FF, d)),
            resident((1, d)),
        ],
        out_specs=pl.BlockSpec((tm, d), lambda i: (i, 0)),
        out_shape=jax.ShapeDtypeStruct((n, d), F32),
        scratch_shapes=[pltpu.VMEM((tm, D_FF), BF16)],
        compiler_params=_params(("parallel",), vmem),
        name="ffn",
    )(x, gain, wg, wu, wd, final_gain)


def _key_extras(pos, lane, with_block):
    lg = lane & (LANES - 1)
    ext = jnp.where(lg == X_HI, (pos >> POS_SHIFT).astype(F32),
                    jnp.where(lg == X_LO, (pos & (POS_SPLIT - 1)).astype(F32), 0.0))
    if with_block:
        ext = jnp.where((lg >= X_SEL) & (lg < X_HI) & ((pos >> SEL_SHIFT) == lg - X_SEL), 1.0, ext)
    return ext


def _inproj_kernel(x_ref, g_ref, w_ref, q_ref, kc_ref, vc_ref, ksa_ref, kwa_ref, vs_ref, vw_ref, misc_ref,
                   gq_ref, gk_ref, gv_ref, gr_ref, *, tm):
    h = _rms(x_ref[...], g_ref[...]).astype(BF16)

    def proj(c):
        return _dot(h, w_ref[:, c[0]:c[1]])

    shape = (tm, 2 * LANES)
    lane = lax.broadcasted_iota(I32, shape, 1)
    pos = (pl.program_id(0) * tm + lax.broadcasted_iota(I32, shape, 0)) & (SEQ - 1)
    is_key = (lane & (LANES - 1)) < HEAD_DIM

    q_ref[...] = (proj(C_Q) * (HEAD_DIM ** -0.5)).astype(BF16)
    kc_ref[...] = proj(C_KC)
    vc_ref[...] = proj(C_VC)
    ksa_ref[...] = jnp.where(is_key, proj(C_KSA), _key_extras(pos, lane, True)).astype(BF16)
    kwa_ref[...] = jnp.where(is_key, proj(C_KWA), _key_extras(pos, lane, False)).astype(BF16)
    vs_ref[...] = proj(C_VS).astype(BF16)
    vw_ref[...] = proj(C_VW).astype(BF16)
    misc_ref[...] = proj(C_MISC)
    gq_ref[...] = proj(C_GQ)
    gk_ref[...] = proj(C_GK)
    gv_ref[...] = proj(C_GV).astype(BF16)
    gr_ref[...] = proj(C_GR)


def _inproj(x, gain, w, tm=512):
    n, d = x.shape
    assert SEQ % tm == 0 and SEQ & (SEQ - 1) == 0
    outs = [(C_Q, BF16), (C_KC, F32), (C_VC, F32), (C_KSA, BF16), (C_KWA, BF16), (C_VS, BF16), (C_VW, BF16),
            (C_MISC, F32), (C_GQ, F32), (C_GK, F32), (C_GV, BF16), (C_GR, F32)]
    vmem = 2 * tm * d * 4 + 2 * d * D_IN_PAD * 2 + 2 * tm * D_IN_PAD * 4 + (8 << 20)
    return pl.pallas_call(
        functools.partial(_inproj_kernel, tm=tm),
        grid=(n // tm,),
        in_specs=[
            pl.BlockSpec((tm, d), lambda i: (i, 0)),
            pl.BlockSpec((1, d), lambda i: (0, 0)),
            pl.BlockSpec((d, D_IN_PAD), lambda i: (0, 0)),
        ],
        out_specs=[pl.BlockSpec((tm, c[1] - c[0]), lambda i: (i, 0)) for c, _ in outs],
        out_shape=[jax.ShapeDtypeStruct((n, c[1] - c[0]), dt) for c, dt in outs],
        compiler_params=_params(("parallel",), vmem),
        name="inproj",
    )(x, gain, w)


def _compress_kernel(kc_ref, vc_ref, pek_ref, pev_ref, w1k_ref, w1v_ref, w2k_ref, w2v_ref, kca_ref, vc2_ref):
    half = CMP_BLOCK // 2

    def hidden(x_ref, pe_ref, w1_ref):
        acc_a = jnp.zeros((N_CMP_PAD, 2 * CMP_HIDDEN), F32)
        acc_b = jnp.zeros((N_CMP_PAD, 2 * CMP_HIDDEN), F32)
        for l in range(half):
            rows = x_ref[pl.ds(l, N_CMP_PAD, stride=CMP_STRIDE), :]
            acc_a += _dot((rows + pe_ref[l:l + 1, :]).astype(BF16), w1_ref[l])
            acc_b += _dot((rows + pe_ref[half + l:half + l + 1, :]).astype(BF16), w1_ref[half + l])
        pre = acc_a + pltpu.roll(acc_b, N_CMP_PAD - 1, axis=0)
        return jax.nn.gelu(pre, approximate=True).astype(BF16)

    kc = _dot(hidden(kc_ref, pek_ref, w1k_ref), w2k_ref[...])
    vc = _dot(hidden(vc_ref, pev_ref, w1v_ref), w2v_ref[...])
    lane = lax.broadcasted_iota(I32, kc.shape, 1)
    row = lax.broadcasted_iota(I32, kc.shape, 0)
    kc = jnp.where((lane & (LANES - 1)) < HEAD_DIM, kc, _key_extras(2 * CMP_STRIDE * row + CMP_BLOCK - 1, lane, False))
    kca_ref[0] = jnp.where(row < N_CMP_PAD - 1, kc, 0.0).astype(BF16)
    vc2_ref[0] = jnp.where(row[:, :LANES] < N_CMP_PAD - 1, vc, 0.0).astype(BF16)


def _compress(kc, vc, pek, pev, w1k, w1v, w2k, w2v):
    n = kc.shape[0]
    b = n // SEQ
    full = lambda a: pl.BlockSpec(a.shape, lambda i: (0,) * a.ndim)
    return pl.pallas_call(
        _compress_kernel,
        grid=(b,),
        in_specs=[
            pl.BlockSpec((SEQ, LANES), lambda i: (i, 0)),
            pl.BlockSpec((SEQ, LANES), lambda i: (i, 0)),
            full(pek), full(pev), full(w1k), full(w1v), full(w2k), full(w2v),
        ],
        out_specs=[pl.BlockSpec((1, N_CMP_PAD, 2 * LANES), lambda i: (i, 0, 0)),
                   pl.BlockSpec((1, N_CMP_PAD, LANES), lambda i: (i, 0, 0))],
        out_shape=[jax.ShapeDtypeStruct((b, N_CMP_PAD, 2 * LANES), BF16),
                   jax.ShapeDtypeStruct((b, N_CMP_PAD, LANES), BF16)],
        compiler_params=_params(("parallel",), 32 << 20),
        name="compress",
    )(kc, vc, pek, pev, w1k, w1v, w2k, w2v)


def _attend(q4, k, v, chunk_masks):
    s_all = _dot_nt(q4, k)
    p_parts, inv = [], []
    for r in range(NSA_REP):
        cols = []
        for c, allowed in enumerate(chunk_masks):
            s = s_all[r * Q_BLOCK:(r + 1) * Q_BLOCK, c * KV_CHUNK:(c + 1) * KV_CHUNK]
            cols.append(s if allowed is None else jnp.where(allowed, s, NEG_INF))
        s = cols[0] if len(cols) == 1 else jnp.concatenate(cols, axis=1)
        p = jnp.exp(s - jnp.max(s, axis=-1, keepdims=True))
        inv.append(1.0 / jnp.sum(p, axis=-1, keepdims=True))
        p_parts.append(p.astype(BF16))
    pv = _dot(jnp.concatenate(p_parts, axis=0), v)
    return [pv[r * Q_BLOCK:(r + 1) * Q_BLOCK] * inv[r] for r in range(NSA_REP)]


def _nsa_kernel(q_ref, misc_ref, kca_ref, vc2_ref, ksa_ref, vs_ref, kwa_ref, vw_ref, o_ref):
    for cls in range(N_QB // QB_PER_CLASS):
        @pl.when(pl.program_id(2) == cls)
        def _(cls=cls):
            _nsa_body(cls, q_ref, misc_ref, kca_ref, vc2_ref, ksa_ref, vs_ref, kwa_ref, vw_ref, o_ref)


def _nsa_body(cls, q_ref, misc_ref, kca_ref, vc2_ref, ksa_ref, vs_ref, kwa_ref, vw_ref, o_ref):
    g = pl.program_id(1)
    qb = cls * QB_PER_CLASS + pl.program_id(3)
    t0 = qb * Q_BLOCK
    slopes = [jnp.where(g == 0, 2.0 ** -(r + 1), 2.0 ** -(r + 1 + NSA_REP)).astype(F32)
              for r in range(NSA_REP)]

    lane = lax.broadcasted_iota(I32, (Q_BLOCK, LANES), 1)
    sub = lax.broadcasted_iota(I32, (Q_BLOCK, LANES), 0)
    low = lane < HEAD_DIM
    tq = t0 + sub

    qf = q_ref[...].astype(F32)
    q_tiles = []
    for r in range(NSA_REP):
        tile = qf[:, (r // 2) * LANES:(r // 2 + 1) * LANES]
        if r % 2:
            tile = pltpu.roll(tile, HEAD_DIM, axis=1)
        q_tiles.append(jnp.where(low, tile, 0.0))

    def stack_queries(slope_scale, sel_lanes):
        parts = []
        for r in range(NSA_REP):
            ext = jnp.where(lane == X_HI, POS_SPLIT * slope_scale * slopes[r],
                            jnp.where(lane == X_LO, slope_scale * slopes[r], sel_lanes))
            parts.append(jnp.where(low, q_tiles[r], ext).astype(BF16))
        return jnp.concatenate(parts, axis=0)

    kca = kca_ref[0]
    s_c = _dot_nt(stack_queries(0.5, 0.0), kca)
    valid_c = lane * CMP_STRIDE + (CMP_BLOCK - 1) <= tq
    p_sum = jnp.zeros((Q_BLOCK, LANES), F32)
    p_parts = []
    for r in range(NSA_REP):
        s = jnp.where(valid_c, s_c[r * Q_BLOCK:(r + 1) * Q_BLOCK], NEG_INF)
        e = jnp.exp(s - jnp.max(s, axis=-1, keepdims=True))
        p = jnp.where(valid_c, e / jnp.sum(e, axis=-1, keepdims=True), 0.0)
        p_sum += p
        p_parts.append(p.astype(BF16))
    o_cmp = _dot(jnp.concatenate(p_parts, axis=0), vc2_ref[0])

    ov_t = ((lane < 4 * sub + 4) & (lane > 4 * sub - 2) & (sub < N_SEL)).astype(F32)
    imp_t = _dot_nt(ov_t, p_sum, precision=lax.Precision.HIGHEST)[:N_SEL]
    j_blk = lax.broadcasted_iota(I32, (N_SEL, LANES), 0)
    cur = (t0 + lax.broadcasted_iota(I32, (N_SEL, LANES), 1)) >> SEL_SHIFT
    forced = (j_blk == 0) | (j_blk == cur) | (j_blk == cur - 1)
    imp_t = jnp.where(forced, FORCE, imp_t)
    imp_t = jnp.where(j_blk <= cur, imp_t, -FORCE)
    rank = jnp.zeros((N_SEL, LANES), I32)
    for jp in range(N_SEL):
        row = imp_t[jp:jp + 1, :]
        beats = (row > imp_t) | ((row == imp_t) & (j_blk > jp))
        rank += beats.astype(I32)
    drop_t = jnp.where(rank < SEL_TOPK, 0.0, NEG_INF)
    drop_q = jnp.concatenate([drop_t, jnp.zeros((LANES - N_SEL, LANES), F32)], axis=0).T
    drop_q = pltpu.roll(drop_q, X_SEL, axis=1)

    q4_plain = stack_queries(1.0, 0.0)
    q4_sel = stack_queries(1.0, drop_q)

    def dist_to(c):
        return (t0 + lax.broadcasted_iota(I32, (Q_BLOCK, KV_CHUNK), 0)
                - (c * KV_CHUNK + lax.broadcasted_iota(I32, (Q_BLOCK, KV_CHUNK), 1)))

    c_diag = cls
    dist_diag = dist_to(c_diag)
    n_slc = (c_diag + 1) * KV_CHUNK
    o_slc = _attend(q4_sel, ksa_ref[:n_slc, :], vs_ref[:n_slc, :], [None] * c_diag + [dist_diag >= 0])

    c_first = max(c_diag - WINDOW // KV_CHUNK, 0)
    masks = [None] * (c_diag - c_first + 1)
    masks[-1] = dist_diag >= 0
    if c_diag - c_first == WINDOW // KV_CHUNK:
        masks[0] = dist_to(c_first) < WINDOW
    win_rows = slice(c_first * KV_CHUNK, n_slc)
    o_win = _attend(q4_plain, kwa_ref[win_rows, :], vw_ref[win_rows, :], masks)

    gates = jax.nn.sigmoid(misc_ref[...])
    heads = []
    for r in range(NSA_REP):
        heads.append(gates[:, 3 * r:3 * r + 1] * o_cmp[r * Q_BLOCK:(r + 1) * Q_BLOCK]
                     + gates[:, 3 * r + 1:3 * r + 2] * o_slc[r]
                     + gates[:, 3 * r + 2:3 * r + 3] * o_win[r])
    for pair in range(NSA_REP // 2):
        even, odd = heads[2 * pair], heads[2 * pair + 1]
        left = jnp.where(g == 0, even, pltpu.roll(even, HEAD_DIM, axis=1))
        right = jnp.where(g == 0, pltpu.roll(odd, HEAD_DIM, axis=1), odd)
        o_ref[:, pair * LANES:(pair + 1) * LANES] = jnp.where(low, left, right).astype(BF16)


def _nsa(q, misc, kca, vc2, ksa, vs, kwa, vw):
    n = q.shape[0]
    b = n // SEQ
    gw = NSA_REP * HEAD_DIM
    assert WINDOW % KV_CHUNK == 0 and KV_CHUNK % Q_BLOCK == 0
    qrow = lambda i, g, c, j: i * N_QB + c * QB_PER_CLASS + j
    return pl.pallas_call(
        _nsa_kernel,
        grid=(b, NSA_GROUPS, N_QB // QB_PER_CLASS, QB_PER_CLASS),
        in_specs=[
            pl.BlockSpec((Q_BLOCK, gw), lambda i, g, c, j: (qrow(i, g, c, j), g)),
            pl.BlockSpec((Q_BLOCK, LANES), lambda i, g, c, j: (qrow(i, g, c, j), g)),
            pl.BlockSpec((1, N_CMP_PAD, LANES), lambda i, g, c, j: (i, 0, g)),
            pl.BlockSpec((1, N_CMP_PAD, LANES), lambda i, g, c, j: (i, 0, 0)),
            pl.BlockSpec((SEQ, LANES), lambda i, g, c, j: (i, g)),
            pl.BlockSpec((SEQ, LANES), lambda i, g, c, j: (i, 0)),
            pl.BlockSpec((SEQ, LANES), lambda i, g, c, j: (i, g)),
            pl.BlockSpec((SEQ, LANES), lambda i, g, c, j: (i, 0)),
        ],
        out_specs=pl.BlockSpec((Q_BLOCK, gw), lambda i, g, c, j: (qrow(i, g, c, j), g)),
        out_shape=jax.ShapeDtypeStruct((n, NSA_GROUPS * gw), BF16),
        compiler_params=_params(("parallel", "parallel", "arbitrary", "arbitrary"), 48 << 20),
        name="nsa",
    )(q, misc, kca, vc2, ksa, vs, kwa, vw)


def _gla_kernel(q_ref, k_ref, v_ref, r_ref, misc_ref, wa_ref, ba_ref, gn_ref, o_ref):
    c, blk = GLA_CHUNK, GLA_BLOCK
    n_c = blk // c
    lane = lax.broadcasted_iota(I32, (blk, LANES), 1)
    row = lax.broadcasted_iota(I32, (blk, LANES), 0)
    in_chunk = row & (c - 1)
    chunk_of_row = row >> GLA_CHUNK_SHIFT
    r2 = lax.broadcasted_iota(I32, (2 * blk, blk), 0) & (blk - 1)
    c2 = lax.broadcasted_iota(I32, (2 * blk, blk), 1)
    intra = (r2 >= c2) & ((r2 >> GLA_CHUNK_SHIFT) == (c2 >> GLA_CHUNK_SHIFT))
    lane_s = lax.broadcasted_iota(I32, (GLA_DV, LANES), 1)
    st = jnp.zeros((GLA_DV, LANES), F32)

    for i_blk in range(SEQ // blk):
        rows = slice(i_blk * blk, (i_blk + 1) * blk)
        la = _dot(misc_ref[rows, :], wa_ref[...], precision=lax.Precision.HIGHEST) + ba_ref[...]
        b = (jnp.minimum(la, 0.0) - jnp.log1p(jnp.exp(-jnp.abs(la)))) * (1.0 / GLA_TAU)
        shift = 1
        while shift < c:
            b = b + jnp.where(in_chunk >= shift, pltpu.roll(b, shift, axis=0), 0.0)
            shift *= 2
        b3 = b.reshape(n_c, c, LANES)
        b_last = b3[:, c - 1:c, :]
        k = k_ref[rows, :]
        q_in = q_ref[rows, :] * (GLA_DK ** -0.5) * jnp.exp(b)
        k_in = (k * jnp.exp(-b)).astype(BF16)
        k_st = (k.reshape(n_c, c, LANES) * jnp.exp(b_last - b3)).reshape(blk, LANES)
        decay = jnp.exp(b_last)
        v = v_ref[rows, :]

        q2 = jnp.concatenate([jnp.where(lane < GLA_DK, q_in, 0.0), jnp.where(lane >= GLA_DK, q_in, 0.0)],
                             axis=0).astype(BF16)
        a = jnp.where(intra, _dot_nt(q2, k_in), 0.0).astype(BF16)
        o_intra = _dot(a, v)
        k_cols = jnp.concatenate([jnp.where(chunk_of_row == i_c, k_st, 0.0) for i_c in range(n_c)], axis=1)
        inc = _dot_tn(v, k_cols.astype(BF16))

        o_inter = []
        for i_c in range(n_c):
            q_c = jnp.concatenate([q2[i_c * c:(i_c + 1) * c], q2[blk + i_c * c:blk + (i_c + 1) * c]], axis=0)
            o_inter.append(_dot_nt(q_c, st.astype(BF16)))
            cols = slice(i_c * LANES, (i_c + 1) * LANES)
            st = st * decay[i_c] + jnp.where(lane_s < GLA_DK, inc[:GLA_DV, cols], inc[GLA_DV:, cols])

        for h in range(2):
            cols = slice(h * GLA_DV, (h + 1) * GLA_DV)
            o = o_intra[h * blk:(h + 1) * blk, cols] + jnp.concatenate(
                [o_inter[i_c][h * c:(h + 1) * c] for i_c in range(n_c)], axis=0)
            gate = r_ref[rows, cols]
            o_ref[rows, cols] = (_rms(o, gn_ref[...]) * (gate * jax.nn.sigmoid(gate))).astype(BF16)


def _gla(gq, gk, gv, gr, misc, wa, ba, gn):
    n = gq.shape[0]
    b = n // SEQ
    return pl.pallas_call(
        _gla_kernel,
        grid=(b, GLA_HEADS // 2),
        in_specs=[
            pl.BlockSpec((SEQ, LANES), lambda i, p: (i, p)),
            pl.BlockSpec((SEQ, LANES), lambda i, p: (i, p)),
            pl.BlockSpec((SEQ, 2 * GLA_DV), lambda i, p: (i, p)),
            pl.BlockSpec((SEQ, 2 * GLA_DV), lambda i, p: (i, p)),
            pl.BlockSpec((SEQ, LANES), lambda i, p: (i, 0)),
            pl.BlockSpec((LANES, LANES), lambda i, p: (0, p)),
            pl.BlockSpec((1, LANES), lambda i, p: (0, p)),
            pl.BlockSpec((1, GLA_DV), lambda i, p: (0, 0)),
        ],
        out_specs=pl.BlockSpec((SEQ, 2 * GLA_DV), lambda i, p: (i, p)),
        out_shape=jax.ShapeDtypeStruct((n, GLA_HEADS * GLA_DV), BF16),
        compiler_params=_params(("parallel", "parallel"), 40 << 20),
        name="gla",
    )(gq, gk, gv, gr, misc, wa, ba, gn)


def _outproj_kernel(x_ref, a_ref, b_ref, wa_ref, wb_ref, o_ref):
    o_ref[...] = x_ref[...] + _dot(a_ref[...], wa_ref[...]) + _dot(b_ref[...], wb_ref[...])


def _outproj(x, o_nsa, o_gla, w_a, w_b, tm=512):
    n, d = x.shape
    ka, kb = o_nsa.shape[1], o_gla.shape[1]
    vmem = 2 * 2 * tm * d * 4 + 2 * tm * (ka + kb) * 2 + 2 * (ka + kb) * d * 2 + (8 << 20)
    return pl.pallas_call(
        _outproj_kernel,
        grid=(n // tm,),
        in_specs=[
            pl.BlockSpec((tm, d), lambda i: (i, 0)),
            pl.BlockSpec((tm, ka), lambda i: (i, 0)),
            pl.BlockSpec((tm, kb), lambda i: (i, 0)),
            pl.BlockSpec((ka, d), lambda i: (0, 0)),
            pl.BlockSpec((kb, d), lambda i: (0, 0)),
        ],
        out_specs=pl.BlockSpec((tm, d), lambda i: (i, 0)),
        out_shape=jax.ShapeDtypeStruct((n, d), F32),
        compiler_params=_params(("parallel",), vmem),
        name="outproj",
    )(x, o_nsa, o_gla, w_a, w_b)


def _permute_w_in(w):
    d = w.shape[0]
    col = lambda lo, hi: w[:, lo:hi]
    zeros = lambda k: jnp.zeros((d, k), w.dtype)
    hd = HEAD_DIM

    def keys_padded(k0):
        return [col(k0, k0 + hd), zeros(LANES - hd), col(k0 + hd, k0 + 2 * hd), zeros(LANES - hd)]

    n_gate = 3 * NSA_REP
    misc = [col(1280, 1280 + n_gate), col(2840, 2856), zeros(LANES - n_gate - GLA_LOWRANK),
            col(1280 + n_gate, 1304), zeros(LANES - n_gate)]
    parts = ([col(0, 512), col(512, 640), col(640, 768)] + keys_padded(768) + keys_padded(1024)
             + [col(896, 1024), col(1152, 1280)] + misc
             + [col(1304, 1560), col(1560, 1816), col(1816, 2328), col(2328, 2840)])
    out = jnp.concatenate(parts, axis=1)
    assert out.shape[1] == D_IN_PAD
    return out.astype(BF16)


def _compress_weights(pe, w1, w2, group_stride):
    hd, hid = HEAD_DIM, CMP_HIDDEN
    pe2 = jnp.concatenate([pe, pe], axis=1)
    w1 = w1.reshape(CMP_BLOCK, hd, hid)
    z1 = jnp.zeros_like(w1)
    w1e = jnp.concatenate([jnp.concatenate([w1, z1], axis=2), jnp.concatenate([z1, w1], axis=2)], axis=1)
    w2e = jnp.zeros((NSA_GROUPS * hid, NSA_GROUPS * group_stride), w2.dtype)
    for g in range(NSA_GROUPS):
        w2e = w2e.at[g * hid:(g + 1) * hid, g * group_stride:g * group_stride + hd].set(w2)
    return pe2, w1e.astype(BF16), w2e.astype(BF16)


def kernel(x, ffn1_norm, ffn1_w_gate, ffn1_w_up, ffn1_w_down, mix_norm, w_in, nsa_pe_k, nsa_w1_k, nsa_w2_k,
           nsa_pe_v, nsa_w1_v, nsa_w2_v, gla_w_a2, gla_b_a, gla_norm, w_out, ffn2_norm, ffn2_w_gate,
           ffn2_w_up, ffn2_w_down, final_norm):
    bsz, seq, d = x.shape
    assert (seq, d) == (SEQ, D_MODEL) and ffn1_norm.shape[0] == 1
    n = bsz * seq
    xf = x.reshape(n, d)
    row = lambda v: v.reshape(1, -1).astype(F32)
    bf = lambda w: w.astype(BF16)
    ones = jnp.ones((1, d), F32)

    x1 = _ffn(xf, row(ffn1_norm[0]), bf(ffn1_w_gate[0]), bf(ffn1_w_up[0]), bf(ffn1_w_down[0]), ones,
              final_norm=False)

    q, kc, vc, ksa, kwa, vs, vw, misc, gq, gk, gv, gr = _inproj(x1, row(mix_norm[0]), _permute_w_in(w_in[0]))

    pek, w1k, w2k = _compress_weights(nsa_pe_k[0], nsa_w1_k[0], nsa_w2_k[0], LANES)
    pev, w1v, w2v = _compress_weights(nsa_pe_v[0], nsa_w1_v[0], nsa_w2_v[0], HEAD_DIM)
    kca, vc2 = _compress(kc, vc, pek, pev, w1k, w1v, w2k, w2v)

    o_nsa = _nsa(q, misc, kca, vc2, ksa, vs, kwa, vw)

    wa = jnp.zeros((LANES, GLA_HEADS * GLA_DK), F32).at[MISC_GA_OFF:MISC_GA_OFF + GLA_LOWRANK].set(gla_w_a2[0])
    o_gla = _gla(gq, gk, gv, gr, misc, wa, row(gla_b_a[0]), row(gla_norm[0]))

    d_nsa = NSA_HEADS * HEAD_DIM
    x2 = _outproj(x1, o_nsa, o_gla, bf(w_out[0][:d_nsa]), bf(w_out[0][d_nsa:]))

    out = _ffn(x2, row(ffn2_norm[0]), bf(ffn2_w_gate[0]), bf(ffn2_w_up[0]), bf(ffn2_w_down[0]),
               row(final_norm), final_norm=True)
    return out.reshape(bsz, seq, d)
```

```python
import functools

import jax
import jax.numpy as jnp
from jax import lax
from jax.experimental import pallas as pl
from jax.experimental.pallas import tpu as pltpu

F32 = jnp.float32
BF16 = jnp.bfloat16
I32 = jnp.int32

D_MODEL = 1024
SEQ = 2048
D_FF = 2816
EPS = 1e-6
NEG_INF = -1e30
FORCE = 1e9

NSA_HEADS = 8
NSA_GROUPS = 2
NSA_REP = NSA_HEADS // NSA_GROUPS
HEAD_DIM = 64
CMP_BLOCK = 32
CMP_STRIDE = 16
CMP_HIDDEN = 128
N_CMP_PAD = SEQ // CMP_STRIDE
SEL_BLOCK = 64
SEL_SHIFT = 6
assert 1 << SEL_SHIFT == SEL_BLOCK
N_SEL = SEQ // SEL_BLOCK
SEL_TOPK = 8
WINDOW = 512
Q_BLOCK = 256
N_QB = SEQ // Q_BLOCK
KV_CHUNK = Q_BLOCK

GLA_HEADS = 4
GLA_DK = 64
GLA_DV = 128
GLA_CHUNK = 64
GLA_CHUNK_SHIFT = 6
assert 1 << GLA_CHUNK_SHIFT == GLA_CHUNK
GLA_BLOCK = 256
GLA_LOWRANK = 16
GLA_TAU = 16.0

LANES = 128
V7X_VMEM_BYTES = 64 * 1024 * 1024

C_Q = (0, 512)
C_KC = (512, 640)
C_VC = (640, 768)
C_KSA = (768, 1024)
C_KWA = (1024, 1280)
C_VS = (1280, 1408)
C_VW = (1408, 1536)
C_MISC = (1536, 1792)
C_GQ = (1792, 2048)
C_GK = (2048, 2304)
C_GV = (2304, 2816)
C_GR = (2816, 3328)
D_IN_PAD = 3328
MISC_GA_OFF = 12

X_SEL = HEAD_DIM
X_HI = HEAD_DIM + N_SEL
X_LO = X_HI + 1
POS_SPLIT = 64
POS_SHIFT = 6
assert 1 << POS_SHIFT == POS_SPLIT and X_LO < LANES


def _params(sem, vmem_bytes):
    return pltpu.CompilerParams(dimension_semantics=sem,
                                vmem_limit_bytes=min(int(vmem_bytes), V7X_VMEM_BYTES - (8 << 20)))


def _rms(x, g):
    return x * lax.rsqrt(jnp.mean(x * x, axis=-1, keepdims=True) + EPS) * g


def _dot(a, b, **kw):
    return jnp.dot(a, b, preferred_element_type=F32, **kw)


def _dot_nt(a, b, **kw):
    return lax.dot_general(a, b, (((1,), (1,)), ((), ())), preferred_element_type=F32, **kw)


def _dot_tn(a, b, **kw):
    return lax.dot_general(a, b, (((0,), (0,)), ((), ())), preferred_element_type=F32, **kw)


def _ffn_kernel(x_ref, g_ref, wg_ref, wu_ref, wd_ref, fg_ref, o_ref, act_ref, *, tf, final_norm):
    x = x_ref[...]
    h = _rms(x, g_ref[...]).astype(BF16)
    for c in range(D_FF // tf):
        cols = slice(c * tf, (c + 1) * tf)
        gate = _dot(h, wg_ref[:, cols])
        up = _dot(h, wu_ref[:, cols])
        act_ref[:, cols] = ((gate * jax.nn.sigmoid(gate)) * up).astype(BF16)
    y = x + 0.5 * _dot(act_ref[...], wd_ref[...])
    if final_norm:
        y = _rms(y, fg_ref[...])
    o_ref[...] = y


def _ffn(x, gain, wg, wu, wd, final_gain, *, final_norm, tm=512, tf=256):
    n, d = x.shape
    assert D_FF % tf == 0
    resident = lambda shape: pl.BlockSpec(shape, lambda i: (0, 0), pipeline_mode=pl.Buffered(1))
    vmem = 2 * 2 * tm * d * 4 + 3 * d * D_FF * 2 + tm * D_FF * 2 + tm * d * 2 + 4 * tm * tf * 4 + (8 << 20)
    return pl.pallas_call(
        functools.partial(_ffn_kernel, tf=tf, final_norm=final_norm),
        grid=(n // tm,),
        in_specs=[
            pl.BlockSpec((tm, d), lambda i: (i, 0)),
            resident((1, d)),
            resident((d, D_FF)),
            resident((d, D_FF)),
            resident((D_FF, d)),
            resident((1, d)),
        ],
        out_specs=pl.BlockSpec((tm, d), lambda i: (i, 0)),
        out_shape=jax.ShapeDtypeStruct((n, d), F32),
        scratch_shapes=[pltpu.VMEM((tm, D_FF), BF16)],
        compiler_params=_params(("parallel",), vmem),
        name="ffn",
    )(x, gain, wg, wu, wd, final_gain)


def _key_extras(pos, lane, with_block):
    lg = lane & (LANES - 1)
    ext = jnp.where(lg == X_HI, (pos >> POS_SHIFT).astype(F32),
                    jnp.where(lg == X_LO, (pos & (POS_SPLIT - 1)).astype(F32), 0.0))
    if with_block:
        ext = jnp.where((lg >= X_SEL) & (lg < X_HI) & ((pos >> SEL_SHIFT) == lg - X_SEL), 1.0, ext)
    return ext


def _inproj_kernel(x_ref, g_ref, w_ref, q_ref, kc_ref, vc_ref, ksa_ref, kwa_ref, vs_ref, vw_ref, misc_ref,
                   gq_ref, gk_ref, gv_ref, gr_ref, *, tm):
    h = _rms(x_ref[...], g_ref[...]).astype(BF16)

    def proj(c):
        return _dot(h, w_ref[:, c[0]:c[1]])

    shape = (tm, 2 * LANES)
    lane = lax.broadcasted_iota(I32, shape, 1)
    pos = (pl.program_id(0) * tm + lax.broadcasted_iota(I32, shape, 0)) & (SEQ - 1)
    is_key = (lane & (LANES - 1)) < HEAD_DIM

    q_ref[...] = (proj(C_Q) * (HEAD_DIM ** -0.5)).astype(BF16)
    kc_ref[...] = proj(C_KC)
    vc_ref[...] = proj(C_VC)
    ksa_ref[...] = jnp.where(is_key, proj(C_KSA), _key_extras(pos, lane, True)).astype(BF16)
    kwa_ref[...] = jnp.where(is_key, proj(C_KWA), _key_extras(pos, lane, False)).astype(BF16)
    vs_ref[...] = proj(C_VS).astype(BF16)
    vw_ref[...] = proj(C_VW).astype(BF16)
    misc_ref[...] = proj(C_MISC)
    gq_ref[...] = proj(C_GQ)
    gk_ref[...] = proj(C_GK)
    gv_ref[...] = proj(C_GV).astype(BF16)
    gr_ref[...] = proj(C_GR)


def _inproj(x, gain, w, tm=512):
    n, d = x.shape
    assert SEQ % tm == 0 and SEQ & (SEQ - 1) == 0
    outs = [(C_Q, BF16), (C_KC, F32), (C_VC, F32), (C_KSA, BF16), (C_KWA, BF16), (C_VS, BF16), (C_VW, BF16),
            (C_MISC, F32), (C_GQ, F32), (C_GK, F32), (C_GV, BF16), (C_GR, F32)]
    vmem = 2 * tm * d * 4 + 2 * d * D_IN_PAD * 2 + 2 * tm * D_IN_PAD * 4 + (8 << 20)
    return pl.pallas_call(
        functools.partial(_inproj_kernel, tm=tm),
        grid=(n // tm,),
        in_specs=[
            pl.BlockSpec((tm, d), lambda i: (i, 0)),
            pl.BlockSpec((1, d), lambda i: (0, 0)),
            pl.BlockSpec((d, D_IN_PAD), lambda i: (0, 0)),
        ],
        out_specs=[pl.BlockSpec((tm, c[1] - c[0]), lambda i: (i, 0)) for c, _ in outs],
        out_shape=[jax.ShapeDtypeStruct((n, c[1] - c[0]), dt) for c, dt in outs],
        compiler_params=_params(("parallel",), vmem),
        name="inproj",
    )(x, gain, w)


def _compress_kernel(kc_ref, vc_ref, pek_ref, pev_ref, w1k_ref, w1v_ref, w2k_ref, w2v_ref, kca_ref, vc2_ref):
    half = CMP_BLOCK // 2

    def hidden(x_ref, pe_ref, w1_ref):
        acc_a = jnp.zeros((N_CMP_PAD, 2 * CMP_HIDDEN), F32)
        acc_b = jnp.zeros((N_CMP_PAD, 2 * CMP_HIDDEN), F32)
        for l in range(half):
            rows = x_ref[pl.ds(l, N_CMP_PAD, stride=CMP_STRIDE), :]
            acc_a += _dot((rows + pe_ref[l:l + 1, :]).astype(BF16), w1_ref[l])
            acc_b += _dot((rows + pe_ref[half + l:half + l + 1, :]).astype(BF16), w1_ref[half + l])
        pre = acc_a + pltpu.roll(acc_b, N_CMP_PAD - 1, axis=0)
        return jax.nn.gelu(pre, approximate=True).astype(BF16)

    kc = _dot(hidden(kc_ref, pek_ref, w1k_ref), w2k_ref[...])
    vc = _dot(hidden(vc_ref, pev_ref, w1v_ref), w2v_ref[...])
    lane = lax.broadcasted_iota(I32, kc.shape, 1)
    row = lax.broadcasted_iota(I32, kc.shape, 0)
    kc = jnp.where((lane & (LANES - 1)) < HEAD_DIM, kc, _key_extras(2 * CMP_STRIDE * row + CMP_BLOCK - 1, lane, False))
    kca_ref[0] = jnp.where(row < N_CMP_PAD - 1, kc, 0.0).astype(BF16)
    vc2_ref[0] = jnp.where(row[:, :LANES] < N_CMP_PAD - 1, vc, 0.0).astype(BF16)


def _compress(kc, vc, pek, pev, w1k, w1v, w2k, w2v):
    n = kc.shape[0]
    b = n // SEQ
    full = lambda a: pl.BlockSpec(a.shape, lambda i: (0,) * a.ndim)
    return pl.pallas_call(
        _compress_kernel,
        grid=(b,),
        in_specs=[
            pl.BlockSpec((SEQ, LANES), lambda i: (i, 0)),
            pl.BlockSpec((SEQ, LANES), lambda i: (i, 0)),
            full(pek), full(pev), full(w1k), full(w1v), full(w2k), full(w2v),
        ],
        out_specs=[pl.BlockSpec((1, N_CMP_PAD, 2 * LANES), lambda i: (i, 0, 0)),
                   pl.BlockSpec((1, N_CMP_PAD, LANES), lambda i: (i, 0, 0))],
        out_shape=[jax.ShapeDtypeStruct((b, N_CMP_PAD, 2 * LANES), BF16),
                   jax.ShapeDtypeStruct((b, N_CMP_PAD, LANES), BF16)],
        compiler_params=_params(("parallel",), 32 << 20),
        name="compress",
    )(kc, vc, pek, pev, w1k, w1v, w2k, w2v)


def _attend(q4, k, v, chunk_masks):
    s_all = _dot_nt(q4, k)
    p_parts, inv = [], []
    for r in range(NSA_REP):
        cols = []
        for c, allowed in enumerate(chunk_masks):
            s = s_all[r * Q_BLOCK:(r + 1) * Q_BLOCK, c * KV_CHUNK:(c + 1) * KV_CHUNK]
            cols.append(s if allowed is None else jnp.where(allowed, s, NEG_INF))
        s = cols[0] if len(cols) == 1 else jnp.concatenate(cols, axis=1)
        p = jnp.exp(s - jnp.max(s, axis=-1, keepdims=True))
        inv.append(1.0 / jnp.sum(p, axis=-1, keepdims=True))
        p_parts.append(p.astype(BF16))
    pv = _dot(jnp.concatenate(p_parts, axis=0), v)
    return [pv[r * Q_BLOCK:(r + 1) * Q_BLOCK] * inv[r] for r in range(NSA_REP)]


def _nsa_kernel(q_ref, misc_ref, kca_ref, vc2_ref, ksa_ref, vs_ref, kwa_ref, vw_ref, o_ref):
    for qb in range(N_QB):
        @pl.when(pl.program_id(2) == qb)
        def _(qb=qb):
            _nsa_body(qb, q_ref, misc_ref, kca_ref, vc2_ref, ksa_ref, vs_ref, kwa_ref, vw_ref, o_ref)


def _nsa_body(qb, q_ref, misc_ref, kca_ref, vc2_ref, ksa_ref, vs_ref, kwa_ref, vw_ref, o_ref):
    g = pl.program_id(1)
    t0 = qb * Q_BLOCK
    slopes = [jnp.where(g == 0, 2.0 ** -(r + 1), 2.0 ** -(r + 1 + NSA_REP)).astype(F32)
              for r in range(NSA_REP)]

    lane = lax.broadcasted_iota(I32, (Q_BLOCK, LANES), 1)
    sub = lax.broadcasted_iota(I32, (Q_BLOCK, LANES), 0)
    low = lane < HEAD_DIM
    tq = t0 + sub

    qf = q_ref[...].astype(F32)
    q_tiles = []
    for r in range(NSA_REP):
        tile = qf[:, (r // 2) * LANES:(r // 2 + 1) * LANES]
        if r % 2:
            tile = pltpu.roll(tile, HEAD_DIM, axis=1)
        q_tiles.append(jnp.where(low, tile, 0.0))

    def stack_queries(slope_scale, sel_lanes):
        parts = []
        for r in range(NSA_REP):
            ext = jnp.where(lane == X_HI, POS_SPLIT * slope_scale * slopes[r],
                            jnp.where(lane == X_LO, slope_scale * slopes[r], sel_lanes))
            parts.append(jnp.where(low, q_tiles[r], ext).astype(BF16))
        return jnp.concatenate(parts, axis=0)

    kca = kca_ref[0]
    s_c = _dot_nt(stack_queries(0.5, 0.0), kca)
    valid_c = lane * CMP_STRIDE + (CMP_BLOCK - 1) <= tq
    p_sum = jnp.zeros((Q_BLOCK, LANES), F32)
    p_parts = []
    for r in range(NSA_REP):
        s = jnp.where(valid_c, s_c[r * Q_BLOCK:(r + 1) * Q_BLOCK], NEG_INF)
        e = jnp.exp(s - jnp.max(s, axis=-1, keepdims=True))
        p = jnp.where(valid_c, e / jnp.sum(e, axis=-1, keepdims=True), 0.0)
        p_sum += p
        p_parts.append(p.astype(BF16))
    o_cmp = _dot(jnp.concatenate(p_parts, axis=0), vc2_ref[0])

    ov_j = lax.broadcasted_iota(I32, (LANES, LANES), 0)
    ov_i = lax.broadcasted_iota(I32, (LANES, LANES), 1)
    ov_t = ((ov_i < 4 * ov_j + 4) & (ov_i > 4 * ov_j - 2) & (ov_j < N_SEL)).astype(F32)
    imp_t = _dot_nt(ov_t, p_sum, precision=lax.Precision.HIGHEST)[:N_SEL]
    j_blk = lax.broadcasted_iota(I32, (N_SEL, Q_BLOCK), 0)
    cur = (t0 + lax.broadcasted_iota(I32, (N_SEL, Q_BLOCK), 1)) >> SEL_SHIFT
    forced = (j_blk == 0) | (j_blk == cur) | (j_blk == cur - 1)
    imp_t = jnp.where(forced, FORCE, imp_t)
    imp_t = jnp.where(j_blk <= cur, imp_t, -FORCE)
    rank = jnp.zeros((N_SEL, Q_BLOCK), I32)
    for jp in range(N_SEL):
        row = imp_t[jp:jp + 1, :]
        beats = (row > imp_t) | ((row == imp_t) & (j_blk > jp))
        rank += beats.astype(I32)
    drop_t = jnp.where(rank < SEL_TOPK, 0.0, NEG_INF)
    drop_t = jnp.concatenate([drop_t, jnp.zeros((LANES - N_SEL, Q_BLOCK), F32)], axis=0)
    drop_q = jnp.concatenate([drop_t[:, i * LANES:(i + 1) * LANES].T for i in range(Q_BLOCK // LANES)], axis=0)
    drop_q = pltpu.roll(drop_q, X_SEL, axis=1)

    q4_plain = stack_queries(1.0, 0.0)
    q4_sel = stack_queries(1.0, drop_q)

    def dist_to(c):
        return (t0 + lax.broadcasted_iota(I32, (Q_BLOCK, KV_CHUNK), 0)
                - (c * KV_CHUNK + lax.broadcasted_iota(I32, (Q_BLOCK, KV_CHUNK), 1)))

    c_diag = qb
    dist_diag = dist_to(c_diag)
    n_slc = (c_diag + 1) * KV_CHUNK
    o_slc = _attend(q4_sel, ksa_ref[:n_slc, :], vs_ref[:n_slc, :], [None] * c_diag + [dist_diag >= 0])

    c_first = max(c_diag - WINDOW // KV_CHUNK, 0)
    masks = [None] * (c_diag - c_first + 1)
    masks[-1] = dist_diag >= 0
    if c_diag - c_first == WINDOW // KV_CHUNK:
        masks[0] = dist_to(c_first) < WINDOW
    win_rows = slice(c_first * KV_CHUNK, n_slc)
    o_win = _attend(q4_plain, kwa_ref[win_rows, :], vw_ref[win_rows, :], masks)

    gates = jax.nn.sigmoid(misc_ref[...])
    heads = []
    for r in range(NSA_REP):
        heads.append(gates[:, 3 * r:3 * r + 1] * o_cmp[r * Q_BLOCK:(r + 1) * Q_BLOCK]
                     + gates[:, 3 * r + 1:3 * r + 2] * o_slc[r]
                     + gates[:, 3 * r + 2:3 * r + 3] * o_win[r])
    for pair in range(NSA_REP // 2):
        even, odd = heads[2 * pair], heads[2 * pair + 1]
        left = jnp.where(g == 0, even, pltpu.roll(even, HEAD_DIM, axis=1))
        right = jnp.where(g == 0, pltpu.roll(odd, HEAD_DIM, axis=1), odd)
        o_ref[:, pair * LANES:(pair + 1) * LANES] = jnp.where(low, left, right).astype(BF16)


def _nsa(q, misc, kca, vc2, ksa, vs, kwa, vw):
    n = q.shape[0]
    b = n // SEQ
    gw = NSA_REP * HEAD_DIM
    assert WINDOW % KV_CHUNK == 0 and KV_CHUNK == Q_BLOCK
    n_cls = N_QB
    return pl.pallas_call(
        _nsa_kernel,
        grid=(b, NSA_GROUPS, n_cls),
        in_specs=[
            pl.BlockSpec((Q_BLOCK, gw), lambda i, g, c: (i * n_cls + c, g)),
            pl.BlockSpec((Q_BLOCK, LANES), lambda i, g, c: (i * n_cls + c, g)),
            pl.BlockSpec((1, N_CMP_PAD, LANES), lambda i, g, c: (i, 0, g)),
            pl.BlockSpec((1, N_CMP_PAD, LANES), lambda i, g, c: (i, 0, 0)),
            pl.BlockSpec((SEQ, LANES), lambda i, g, c: (i, g)),
            pl.BlockSpec((SEQ, LANES), lambda i, g, c: (i, 0)),
            pl.BlockSpec((SEQ, LANES), lambda i, g, c: (i, g)),
            pl.BlockSpec((SEQ, LANES), lambda i, g, c: (i, 0)),
        ],
        out_specs=pl.BlockSpec((Q_BLOCK, gw), lambda i, g, c: (i * n_cls + c, g)),
        out_shape=jax.ShapeDtypeStruct((n, NSA_GROUPS * gw), BF16),
        compiler_params=_params(("parallel", "parallel", "arbitrary"), 48 << 20),
        name="nsa",
    )(q, misc, kca, vc2, ksa, vs, kwa, vw)


def _gla_kernel(q_ref, k_ref, v_ref, r_ref, misc_ref, wa_ref, ba_ref, gn_ref, o_ref):
    c, blk = GLA_CHUNK, GLA_BLOCK
    n_c = blk // c
    lane = lax.broadcasted_iota(I32, (blk, LANES), 1)
    row = lax.broadcasted_iota(I32, (blk, LANES), 0)
    in_chunk = row & (c - 1)
    chunk_of_row = row >> GLA_CHUNK_SHIFT
    r2 = lax.broadcasted_iota(I32, (2 * blk, blk), 0) & (blk - 1)
    c2 = lax.broadcasted_iota(I32, (2 * blk, blk), 1)
    intra = (r2 >= c2) & ((r2 >> GLA_CHUNK_SHIFT) == (c2 >> GLA_CHUNK_SHIFT))
    lane_s = lax.broadcasted_iota(I32, (GLA_DV, LANES), 1)
    st = jnp.zeros((GLA_DV, LANES), F32)

    for i_blk in range(SEQ // blk):
        rows = slice(i_blk * blk, (i_blk + 1) * blk)
        la = _dot(misc_ref[rows, :], wa_ref[...], precision=lax.Precision.HIGHEST) + ba_ref[...]
        b = (jnp.minimum(la, 0.0) - jnp.log1p(jnp.exp(-jnp.abs(la)))) * (1.0 / GLA_TAU)
        shift = 1
        while shift < c:
            b = b + jnp.where(in_chunk >= shift, pltpu.roll(b, shift, axis=0), 0.0)
            shift *= 2
        b3 = b.reshape(n_c, c, LANES)
        b_last = b3[:, c - 1:c, :]
        k = k_ref[rows, :]
        q_in = q_ref[rows, :] * (GLA_DK ** -0.5) * jnp.exp(b)
        k_in = (k * jnp.exp(-b)).astype(BF16)
        k_st = (k.reshape(n_c, c, LANES) * jnp.exp(b_last - b3)).reshape(blk, LANES)
        decay = jnp.exp(b_last)
        v = v_ref[rows, :]

        q2 = jnp.concatenate([jnp.where(lane < GLA_DK, q_in, 0.0), jnp.where(lane >= GLA_DK, q_in, 0.0)],
                             axis=0).astype(BF16)
        a = jnp.where(intra, _dot_nt(q2, k_in), 0.0).astype(BF16)
        o_intra = _dot(a, v)
        k_cols = jnp.concatenate([jnp.where(chunk_of_row == i_c, k_st, 0.0) for i_c in range(n_c)], axis=1)
        inc = _dot_tn(v, k_cols.astype(BF16))

        o_inter = []
        for i_c in range(n_c):
            q_c = jnp.concatenate([q2[i_c * c:(i_c + 1) * c], q2[blk + i_c * c:blk + (i_c + 1) * c]], axis=0)
            o_inter.append(_dot_nt(q_c, st.astype(BF16)))
            cols = slice(i_c * LANES, (i_c + 1) * LANES)
            st = st * decay[i_c] + jnp.where(lane_s < GLA_DK, inc[:GLA_DV, cols], inc[GLA_DV:, cols])

        for h in range(2):
            cols = slice(h * GLA_DV, (h + 1) * GLA_DV)
            o = o_intra[h * blk:(h + 1) * blk, cols] + jnp.concatenate(
                [o_inter[i_c][h * c:(h + 1) * c] for i_c in range(n_c)], axis=0)
            gate = r_ref[rows, cols]
            o_ref[rows, cols] = (_rms(o, gn_ref[...]) * (gate * jax.nn.sigmoid(gate))).astype(BF16)


def _gla(gq, gk, gv, gr, misc, wa, ba, gn):
    n = gq.shape[0]
    b = n // SEQ
    return pl.pallas_call(
        _gla_kernel,
        grid=(b, GLA_HEADS // 2),
        in_specs=[
            pl.BlockSpec((SEQ, LANES), lambda i, p: (i, p)),
            pl.BlockSpec((SEQ, LANES), lambda i, p: (i, p)),
            pl.BlockSpec((SEQ, 2 * GLA_DV), lambda i, p: (i, p)),
            pl.BlockSpec((SEQ, 2 * GLA_DV), lambda i, p: (i, p)),
            pl.BlockSpec((SEQ, LANES), lambda i, p: (i, 0)),
            pl.BlockSpec((LANES, LANES), lambda i, p: (0, p)),
            pl.BlockSpec((1, LANES), lambda i, p: (0, p)),
            pl.BlockSpec((1, GLA_DV), lambda i, p: (0, 0)),
        ],
        out_specs=pl.BlockSpec((SEQ, 2 * GLA_DV), lambda i, p: (i, p)),
        out_shape=jax.ShapeDtypeStruct((n, GLA_HEADS * GLA_DV), BF16),
        compiler_params=_params(("parallel", "parallel"), 40 << 20),
        name="gla",
    )(gq, gk, gv, gr, misc, wa, ba, gn)


def _outproj_kernel(x_ref, a_ref, b_ref, wa_ref, wb_ref, o_ref):
    o_ref[...] = x_ref[...] + _dot(a_ref[...], wa_ref[...]) + _dot(b_ref[...], wb_ref[...])


def _outproj(x, o_nsa, o_gla, w_a, w_b, tm=512):
    n, d = x.shape
    ka, kb = o_nsa.shape[1], o_gla.shape[1]
    vmem = 2 * 2 * tm * d * 4 + 2 * tm * (ka + kb) * 2 + 2 * (ka + kb) * d * 2 + (8 << 20)
    return pl.pallas_call(
        _outproj_kernel,
        grid=(n // tm,),
        in_specs=[
            pl.BlockSpec((tm, d), lambda i: (i, 0)),
            pl.BlockSpec((tm, ka), lambda i: (i, 0)),
            pl.BlockSpec((tm, kb), lambda i: (i, 0)),
            pl.BlockSpec((ka, d), lambda i: (0, 0)),
            pl.BlockSpec((kb, d), lambda i: (0, 0)),
        ],
        out_specs=pl.BlockSpec((tm, d), lambda i: (i, 0)),
        out_shape=jax.ShapeDtypeStruct((n, d), F32),
        compiler_params=_params(("parallel",), vmem),
        name="outproj",
    )(x, o_nsa, o_gla, w_a, w_b)


def _permute_w_in(w):
    d = w.shape[0]
    col = lambda lo, hi: w[:, lo:hi]
    zeros = lambda k: jnp.zeros((d, k), w.dtype)
    hd = HEAD_DIM

    def keys_padded(k0):
        return [col(k0, k0 + hd), zeros(LANES - hd), col(k0 + hd, k0 + 2 * hd), zeros(LANES - hd)]

    n_gate = 3 * NSA_REP
    misc = [col(1280, 1280 + n_gate), col(2840, 2856), zeros(LANES - n_gate - GLA_LOWRANK),
            col(1280 + n_gate, 1304), zeros(LANES - n_gate)]
    parts = ([col(0, 512), col(512, 640), col(640, 768)] + keys_padded(768) + keys_padded(1024)
             + [col(896, 1024), col(1152, 1280)] + misc
             + [col(1304, 1560), col(1560, 1816), col(1816, 2328), col(2328, 2840)])
    out = jnp.concatenate(parts, axis=1)
    assert out.shape[1] == D_IN_PAD
    return out.astype(BF16)


def _compress_weights(pe, w1, w2, group_stride):
    hd, hid = HEAD_DIM, CMP_HIDDEN
    pe2 = jnp.concatenate([pe, pe], axis=1)
    w1 = w1.reshape(CMP_BLOCK, hd, hid)
    z1 = jnp.zeros_like(w1)
    w1e = jnp.concatenate([jnp.concatenate([w1, z1], axis=2), jnp.concatenate([z1, w1], axis=2)], axis=1)
    w2e = jnp.zeros((NSA_GROUPS * hid, NSA_GROUPS * group_stride), w2.dtype)
    for g in range(NSA_GROUPS):
        w2e = w2e.at[g * hid:(g + 1) * hid, g * group_stride:g * group_stride + hd].set(w2)
    return pe2, w1e.astype(BF16), w2e.astype(BF16)


def kernel(x, ffn1_norm, ffn1_w_gate, ffn1_w_up, ffn1_w_down, mix_norm, w_in, nsa_pe_k, nsa_w1_k, nsa_w2_k,
           nsa_pe_v, nsa_w1_v, nsa_w2_v, gla_w_a2, gla_b_a, gla_norm, w_out, ffn2_norm, ffn2_w_gate,
           ffn2_w_up, ffn2_w_down, final_norm):
    bsz, seq, d = x.shape
    assert (seq, d) == (SEQ, D_MODEL) and ffn1_norm.shape[0] == 1
    n = bsz * seq
    xf = x.reshape(n, d)
    row = lambda v: v.reshape(1, -1).astype(F32)
    bf = lambda w: w.astype(BF16)
    ones = jnp.ones((1, d), F32)

    x1 = _ffn(xf, row(ffn1_norm[0]), bf(ffn1_w_gate[0]), bf(ffn1_w_up[0]), bf(ffn1_w_down[0]), ones,
              final_norm=False)

    q, kc, vc, ksa, kwa, vs, vw, misc, gq, gk, gv, gr = _inproj(x1, row(mix_norm[0]), _permute_w_in(w_in[0]))

    pek, w1k, w2k = _compress_weights(nsa_pe_k[0], nsa_w1_k[0], nsa_w2_k[0], LANES)
    pev, w1v, w2v = _compress_weights(nsa_pe_v[0], nsa_w1_v[0], nsa_w2_v[0], HEAD_DIM)
    kca, vc2 = _compress(kc, vc, pek, pev, w1k, w1v, w2k, w2v)

    o_nsa = _nsa(q, misc, kca, vc2, ksa, vs, kwa, vw)

    wa = jnp.zeros((LANES, GLA_HEADS * GLA_DK), F32).at[MISC_GA_OFF:MISC_GA_OFF + GLA_LOWRANK].set(gla_w_a2[0])
    o_gla = _gla(gq, gk, gv, gr, misc, wa, row(gla_b_a[0]), row(gla_norm[0]))

    d_nsa = NSA_HEADS * HEAD_DIM
    x2 = _outproj(x1, o_nsa, o_gla, bf(w_out[0][:d_nsa]), bf(w_out[0][d_nsa:]))

    out = _ffn(x2, row(ffn2_norm[0]), bf(ffn2_w_gate[0]), bf(ffn2_w_up[0]), bf(ffn2_w_down[0]),
               row(final_norm), final_norm=True)
    return out.reshape(bsz, seq, d)
```

```python
import functools

import jax
import jax.numpy as jnp
from jax import lax
from jax.experimental import pallas as pl
from jax.experimental.pallas import tpu as pltpu

F32 = jnp.float32
BF16 = jnp.bfloat16
I32 = jnp.int32

D_MODEL = 1024
SEQ = 2048
D_FF = 2816
EPS = 1e-6
NEG_INF = -1e30
FORCE = 1e9

NSA_HEADS = 8
NSA_GROUPS = 2
NSA_REP = NSA_HEADS // NSA_GROUPS
HEAD_DIM = 64
CMP_BLOCK = 32
CMP_STRIDE = 16
CMP_HIDDEN = 128
N_CMP_PAD = SEQ // CMP_STRIDE
SEL_BLOCK = 64
SEL_SHIFT = 6
assert 1 << SEL_SHIFT == SEL_BLOCK
N_SEL = SEQ // SEL_BLOCK
SEL_TOPK = 8
WINDOW = 512
Q_BLOCK = 256
N_QB = SEQ // Q_BLOCK
KV_CHUNK = Q_BLOCK

GLA_HEADS = 4
GLA_DK = 64
GLA_DV = 128
GLA_CHUNK = 64
GLA_CHUNK_SHIFT = 6
assert 1 << GLA_CHUNK_SHIFT == GLA_CHUNK
GLA_BLOCK = 256
GLA_LOWRANK = 16
GLA_TAU = 16.0

LANES = 128
V7X_VMEM_BYTES = 64 * 1024 * 1024

C_Q = (0, 512)
C_KC = (512, 640)
C_VC = (640, 768)
C_KSA = (768, 1024)
C_KWA = (1024, 1280)
C_VS = (1280, 1408)
C_VW = (1408, 1536)
C_MISC = (1536, 1792)
C_GQ = (1792, 2048)
C_GK = (2048, 2304)
C_GV = (2304, 2816)
C_GR = (2816, 3328)
D_IN_PAD = 3328
MISC_GA_OFF = 12

X_SEL = HEAD_DIM
X_HI = HEAD_DIM + N_SEL
X_LO = X_HI + 1
POS_SPLIT = 64
POS_SHIFT = 6
assert 1 << POS_SHIFT == POS_SPLIT and X_LO < LANES


def _params(sem, vmem_bytes):
    return pltpu.CompilerParams(dimension_semantics=sem,
                                vmem_limit_bytes=min(int(vmem_bytes), V7X_VMEM_BYTES - (8 << 20)))


def _rms(x, g):
    return x * lax.rsqrt(jnp.mean(x * x, axis=-1, keepdims=True) + EPS) * g


def _dot(a, b, **kw):
    return jnp.dot(a, b, preferred_element_type=F32, **kw)


def _dot_nt(a, b, **kw):
    return lax.dot_general(a, b, (((1,), (1,)), ((), ())), preferred_element_type=F32, **kw)


def _dot_tn(a, b, **kw):
    return lax.dot_general(a, b, (((0,), (0,)), ((), ())), preferred_element_type=F32, **kw)


def _ffn_kernel(*refs, tf, final_norm, mixer_out):
    if mixer_out:
        x_ref, a_ref, b_ref, wa_ref, wb_ref, g_ref, wg_ref, wu_ref, wd_ref, fg_ref, o_ref, act_ref = refs
        x = x_ref[...] + _dot(a_ref[...], wa_ref[...]) + _dot(b_ref[...], wb_ref[...])
    else:
        x_ref, g_ref, wg_ref, wu_ref, wd_ref, fg_ref, o_ref, act_ref = refs
        x = x_ref[...]
    h = _rms(x, g_ref[...]).astype(BF16)
    for c in range(D_FF // tf):
        cols = slice(c * tf, (c + 1) * tf)
        gate = _dot(h, wg_ref[:, cols])
        up = _dot(h, wu_ref[:, cols])
        act_ref[:, cols] = ((gate * jax.nn.sigmoid(gate)) * up).astype(BF16)
    y = x + 0.5 * _dot(act_ref[...], wd_ref[...])
    if final_norm:
        y = _rms(y, fg_ref[...])
    o_ref[...] = y


def _ffn(x, mixer, gain, wg, wu, wd, final_gain, *, final_norm, tm=512, tf=256):
    n, d = x.shape
    assert D_FF % tf == 0
    resident = lambda shape: pl.BlockSpec(shape, lambda i: (0, 0), pipeline_mode=pl.Buffered(1))
    rows = lambda width: pl.BlockSpec((tm, width), lambda i: (i, 0))
    vmem = 2 * 2 * tm * d * 4 + 3 * d * D_FF * 2 + tm * D_FF * 2 + tm * d * 2 + 4 * tm * tf * 4 + (8 << 20)
    mix_specs = []
    if mixer:
        a, b, w_a, w_b = mixer
        mix_specs = [rows(a.shape[1]), rows(b.shape[1]), resident(w_a.shape), resident(w_b.shape)]
        vmem += 2 * tm * (a.shape[1] + b.shape[1]) * 2 + (w_a.size + w_b.size) * 2
    return pl.pallas_call(
        functools.partial(_ffn_kernel, tf=tf, final_norm=final_norm, mixer_out=bool(mixer)),
        grid=(n // tm,),
        in_specs=[rows(d)] + mix_specs + [
            resident((1, d)),
            resident((d, D_FF)),
            resident((d, D_FF)),
            resident((D_FF, d)),
            resident((1, d)),
        ],
        out_specs=rows(d),
        out_shape=jax.ShapeDtypeStruct((n, d), F32),
        scratch_shapes=[pltpu.VMEM((tm, D_FF), BF16)],
        compiler_params=_params(("parallel",), vmem),
        name="ffn",
    )(x, *mixer, gain, wg, wu, wd, final_gain)


def _key_extras(pos, lane, with_block):
    lg = lane & (LANES - 1)
    ext = jnp.where(lg == X_HI, (pos >> POS_SHIFT).astype(F32),
                    jnp.where(lg == X_LO, (pos & (POS_SPLIT - 1)).astype(F32), 0.0))
    if with_block:
        ext = jnp.where((lg >= X_SEL) & (lg < X_HI) & ((pos >> SEL_SHIFT) == lg - X_SEL), 1.0, ext)
    return ext


def _inproj_kernel(x_ref, g_ref, w_ref, q_ref, kc_ref, vc_ref, ksa_ref, kwa_ref, vs_ref, vw_ref, misc_ref,
                   gq_ref, gk_ref, gv_ref, gr_ref, *, tm):
    h = _rms(x_ref[...], g_ref[...]).astype(BF16)

    def proj(c):
        return _dot(h, w_ref[:, c[0]:c[1]])

    shape = (tm, 2 * LANES)
    lane = lax.broadcasted_iota(I32, shape, 1)
    pos = (pl.program_id(0) * tm + lax.broadcasted_iota(I32, shape, 0)) & (SEQ - 1)
    is_key = (lane & (LANES - 1)) < HEAD_DIM

    q_ref[...] = (proj(C_Q) * (HEAD_DIM ** -0.5)).astype(BF16)
    kc_ref[...] = proj(C_KC)
    vc_ref[...] = proj(C_VC)
    ksa_ref[...] = jnp.where(is_key, proj(C_KSA), _key_extras(pos, lane, True)).astype(BF16)
    kwa_ref[...] = jnp.where(is_key, proj(C_KWA), _key_extras(pos, lane, False)).astype(BF16)
    vs_ref[...] = proj(C_VS).astype(BF16)
    vw_ref[...] = proj(C_VW).astype(BF16)
    misc_ref[...] = proj(C_MISC)
    gq_ref[...] = proj(C_GQ)
    gk_ref[...] = proj(C_GK)
    gv_ref[...] = proj(C_GV).astype(BF16)
    gr_ref[...] = proj(C_GR)


def _inproj(x, gain, w, tm=512):
    n, d = x.shape
    assert SEQ % tm == 0 and SEQ & (SEQ - 1) == 0
    outs = [(C_Q, BF16), (C_KC, F32), (C_VC, F32), (C_KSA, BF16), (C_KWA, BF16), (C_VS, BF16), (C_VW, BF16),
            (C_MISC, F32), (C_GQ, F32), (C_GK, F32), (C_GV, BF16), (C_GR, F32)]
    vmem = 2 * tm * d * 4 + 2 * d * D_IN_PAD * 2 + 2 * tm * D_IN_PAD * 4 + (8 << 20)
    return pl.pallas_call(
        functools.partial(_inproj_kernel, tm=tm),
        grid=(n // tm,),
        in_specs=[
            pl.BlockSpec((tm, d), lambda i: (i, 0)),
            pl.BlockSpec((1, d), lambda i: (0, 0)),
            pl.BlockSpec((d, D_IN_PAD), lambda i: (0, 0)),
        ],
        out_specs=[pl.BlockSpec((tm, c[1] - c[0]), lambda i: (i, 0)) for c, _ in outs],
        out_shape=[jax.ShapeDtypeStruct((n, c[1] - c[0]), dt) for c, dt in outs],
        compiler_params=_params(("parallel",), vmem),
        name="inproj",
    )(x, gain, w)


def _compress_kernel(kc_ref, vc_ref, pek_ref, pev_ref, w1k_ref, w1v_ref, w2k_ref, w2v_ref, kca_ref, vc2_ref):
    half = CMP_BLOCK // 2

    def hidden(x_ref, pe_ref, w1_ref):
        acc_a = jnp.zeros((N_CMP_PAD, 2 * CMP_HIDDEN), F32)
        acc_b = jnp.zeros((N_CMP_PAD, 2 * CMP_HIDDEN), F32)
        for l in range(half):
            rows = x_ref[pl.ds(l, N_CMP_PAD, stride=CMP_STRIDE), :]
            acc_a += _dot((rows + pe_ref[l:l + 1, :]).astype(BF16), w1_ref[l])
            acc_b += _dot((rows + pe_ref[half + l:half + l + 1, :]).astype(BF16), w1_ref[half + l])
        pre = acc_a + pltpu.roll(acc_b, N_CMP_PAD - 1, axis=0)
        return jax.nn.gelu(pre, approximate=True).astype(BF16)

    kc = _dot(hidden(kc_ref, pek_ref, w1k_ref), w2k_ref[...])
    vc = _dot(hidden(vc_ref, pev_ref, w1v_ref), w2v_ref[...])
    lane = lax.broadcasted_iota(I32, kc.shape, 1)
    row = lax.broadcasted_iota(I32, kc.shape, 0)
    kc = jnp.where((lane & (LANES - 1)) < HEAD_DIM, kc, _key_extras(2 * CMP_STRIDE * row + CMP_BLOCK - 1, lane, False))
    kca_ref[0] = jnp.where(row < N_CMP_PAD - 1, kc, 0.0).astype(BF16)
    vc2_ref[0] = jnp.where(row[:, :LANES] < N_CMP_PAD - 1, vc, 0.0).astype(BF16)


def _compress(kc, vc, pek, pev, w1k, w1v, w2k, w2v):
    n = kc.shape[0]
    b = n // SEQ
    full = lambda a: pl.BlockSpec(a.shape, lambda i: (0,) * a.ndim)
    return pl.pallas_call(
        _compress_kernel,
        grid=(b,),
        in_specs=[
            pl.BlockSpec((SEQ, LANES), lambda i: (i, 0)),
            pl.BlockSpec((SEQ, LANES), lambda i: (i, 0)),
            full(pek), full(pev), full(w1k), full(w1v), full(w2k), full(w2v),
        ],
        out_specs=[pl.BlockSpec((1, N_CMP_PAD, 2 * LANES), lambda i: (i, 0, 0)),
                   pl.BlockSpec((1, N_CMP_PAD, LANES), lambda i: (i, 0, 0))],
        out_shape=[jax.ShapeDtypeStruct((b, N_CMP_PAD, 2 * LANES), BF16),
                   jax.ShapeDtypeStruct((b, N_CMP_PAD, LANES), BF16)],
        compiler_params=_params(("parallel",), 32 << 20),
        name="compress",
    )(kc, vc, pek, pev, w1k, w1v, w2k, w2v)


def _attend(q4, k, v, chunk_masks):
    s_all = _dot_nt(q4, k)
    p_parts, inv = [], []
    for r in range(NSA_REP):
        cols = []
        for c, allowed in enumerate(chunk_masks):
            s = s_all[r * Q_BLOCK:(r + 1) * Q_BLOCK, c * KV_CHUNK:(c + 1) * KV_CHUNK]
            cols.append(s if allowed is None else jnp.where(allowed, s, NEG_INF))
        s = cols[0] if len(cols) == 1 else jnp.concatenate(cols, axis=1)
        p = jnp.exp(s - jnp.max(s, axis=-1, keepdims=True))
        inv.append(1.0 / jnp.sum(p, axis=-1, keepdims=True))
        p_parts.append(p.astype(BF16))
    pv = _dot(jnp.concatenate(p_parts, axis=0), v)
    return [pv[r * Q_BLOCK:(r + 1) * Q_BLOCK] * inv[r] for r in range(NSA_REP)]


def _nsa_kernel(q_ref, misc_ref, kca_ref, vc2_ref, ksa_ref, vs_ref, kwa_ref, vw_ref, o_ref):
    for qb in range(N_QB):
        @pl.when(pl.program_id(2) == qb)
        def _(qb=qb):
            _nsa_body(qb, q_ref, misc_ref, kca_ref, vc2_ref, ksa_ref, vs_ref, kwa_ref, vw_ref, o_ref)


def _nsa_body(qb, q_ref, misc_ref, kca_ref, vc2_ref, ksa_ref, vs_ref, kwa_ref, vw_ref, o_ref):
    g = pl.program_id(1)
    t0 = qb * Q_BLOCK
    slopes = [jnp.where(g == 0, 2.0 ** -(r + 1), 2.0 ** -(r + 1 + NSA_REP)).astype(F32)
              for r in range(NSA_REP)]

    lane = lax.broadcasted_iota(I32, (Q_BLOCK, LANES), 1)
    sub = lax.broadcasted_iota(I32, (Q_BLOCK, LANES), 0)
    low = lane < HEAD_DIM
    tq = t0 + sub

    qf = q_ref[...].astype(F32)
    q_tiles = []
    for r in range(NSA_REP):
        tile = qf[:, (r // 2) * LANES:(r // 2 + 1) * LANES]
        if r % 2:
            tile = pltpu.roll(tile, HEAD_DIM, axis=1)
        q_tiles.append(jnp.where(low, tile, 0.0))

    def stack_queries(slope_scale, sel_lanes):
        parts = []
        for r in range(NSA_REP):
            ext = jnp.where(lane == X_HI, POS_SPLIT * slope_scale * slopes[r],
                            jnp.where(lane == X_LO, slope_scale * slopes[r], sel_lanes))
            parts.append(jnp.where(low, q_tiles[r], ext).astype(BF16))
        return jnp.concatenate(parts, axis=0)

    kca = kca_ref[0]
    s_c = _dot_nt(stack_queries(0.5, 0.0), kca)
    valid_c = lane * CMP_STRIDE + (CMP_BLOCK - 1) <= tq
    p_sum = jnp.zeros((Q_BLOCK, LANES), F32)
    p_parts = []
    for r in range(NSA_REP):
        s = jnp.where(valid_c, s_c[r * Q_BLOCK:(r + 1) * Q_BLOCK], NEG_INF)
        e = jnp.exp(s - jnp.max(s, axis=-1, keepdims=True))
        p = jnp.where(valid_c, e / jnp.sum(e, axis=-1, keepdims=True), 0.0)
        p_sum += p
        p_parts.append(p.astype(BF16))
    o_cmp = _dot(jnp.concatenate(p_parts, axis=0), vc2_ref[0])

    def dist_to(c):
        return (t0 + lax.broadcasted_iota(I32, (Q_BLOCK, KV_CHUNK), 0)
                - (c * KV_CHUNK + lax.broadcasted_iota(I32, (Q_BLOCK, KV_CHUNK), 1)))

    c_diag = qb
    dist_diag = dist_to(c_diag)
    n_slc = (c_diag + 1) * KV_CHUNK

    ov_j = lax.broadcasted_iota(I32, (LANES, LANES), 0)
    ov_i = lax.broadcasted_iota(I32, (LANES, LANES), 1)
    ov_t = ((ov_i < 4 * ov_j + 4) & (ov_i > 4 * ov_j - 2) & (ov_j < N_SEL)).astype(F32)
    imp_t = _dot_nt(ov_t, p_sum, precision=lax.Precision.HIGHEST)[:N_SEL]
    j_blk = lax.broadcasted_iota(I32, (N_SEL, Q_BLOCK), 0)
    cur = (t0 + lax.broadcasted_iota(I32, (N_SEL, Q_BLOCK), 1)) >> SEL_SHIFT
    forced = (j_blk == 0) | (j_blk == cur) | (j_blk == cur - 1)
    imp_t = jnp.where(forced, FORCE, imp_t)
    imp_t = jnp.where(j_blk <= cur, imp_t, -FORCE)
    rank = jnp.zeros((N_SEL, Q_BLOCK), I32)
    for jp in range(N_SEL):
        row = imp_t[jp:jp + 1, :]
        beats = (row > imp_t) | ((row == imp_t) & (j_blk > jp))
        rank += beats.astype(I32)
    drop_t = jnp.where(rank < SEL_TOPK, 0.0, NEG_INF)
    drop_t = jnp.concatenate([drop_t, jnp.zeros((LANES - N_SEL, Q_BLOCK), F32)], axis=0)
    drop_q = jnp.concatenate([drop_t[:, i * LANES:(i + 1) * LANES].T for i in range(Q_BLOCK // LANES)], axis=0)
    drop_q = pltpu.roll(drop_q, X_SEL, axis=1)

    o_slc = _attend(stack_queries(1.0, drop_q), ksa_ref[:n_slc, :], vs_ref[:n_slc, :],
                    [None] * c_diag + [dist_diag >= 0])

    c_first = max(c_diag - WINDOW // KV_CHUNK, 0)
    masks = [None] * (c_diag - c_first + 1)
    masks[-1] = dist_diag >= 0
    if c_diag - c_first == WINDOW // KV_CHUNK:
        masks[0] = dist_to(c_first) < WINDOW
    win_rows = slice(c_first * KV_CHUNK, n_slc)
    o_win = _attend(stack_queries(1.0, 0.0), kwa_ref[win_rows, :], vw_ref[win_rows, :], masks)

    gates = jax.nn.sigmoid(misc_ref[...])
    heads = []
    for r in range(NSA_REP):
        heads.append(gates[:, 3 * r:3 * r + 1] * o_cmp[r * Q_BLOCK:(r + 1) * Q_BLOCK]
                     + gates[:, 3 * r + 1:3 * r + 2] * o_slc[r]
                     + gates[:, 3 * r + 2:3 * r + 3] * o_win[r])
    for pair in range(NSA_REP // 2):
        even, odd = heads[2 * pair], heads[2 * pair + 1]
        left = jnp.where(g == 0, even, pltpu.roll(even, HEAD_DIM, axis=1))
        right = jnp.where(g == 0, pltpu.roll(odd, HEAD_DIM, axis=1), odd)
        o_ref[:, pair * LANES:(pair + 1) * LANES] = jnp.where(low, left, right).astype(BF16)


def _nsa(q, misc, kca, vc2, ksa, vs, kwa, vw):
    n = q.shape[0]
    b = n // SEQ
    gw = NSA_REP * HEAD_DIM
    assert WINDOW % KV_CHUNK == 0 and KV_CHUNK == Q_BLOCK
    n_cls = N_QB
    return pl.pallas_call(
        _nsa_kernel,
        grid=(b, NSA_GROUPS, n_cls),
        in_specs=[
            pl.BlockSpec((Q_BLOCK, gw), lambda i, g, c: (i * n_cls + c, g)),
            pl.BlockSpec((Q_BLOCK, LANES), lambda i, g, c: (i * n_cls + c, g)),
            pl.BlockSpec((1, N_CMP_PAD, LANES), lambda i, g, c: (i, 0, g)),
            pl.BlockSpec((1, N_CMP_PAD, LANES), lambda i, g, c: (i, 0, 0)),
            pl.BlockSpec((SEQ, LANES), lambda i, g, c: (i, g)),
            pl.BlockSpec((SEQ, LANES), lambda i, g, c: (i, 0)),
            pl.BlockSpec((SEQ, LANES), lambda i, g, c: (i, g)),
            pl.BlockSpec((SEQ, LANES), lambda i, g, c: (i, 0)),
        ],
        out_specs=pl.BlockSpec((Q_BLOCK, gw), lambda i, g, c: (i * n_cls + c, g)),
        out_shape=jax.ShapeDtypeStruct((n, NSA_GROUPS * gw), BF16),
        compiler_params=_params(("parallel", "parallel", "arbitrary"), 48 << 20),
        name="nsa",
    )(q, misc, kca, vc2, ksa, vs, kwa, vw)


def _gla_kernel(q_ref, k_ref, v_ref, r_ref, misc_ref, wa_ref, ba_ref, gn_ref, o_ref):
    c, blk = GLA_CHUNK, GLA_BLOCK
    n_c = blk // c
    lane = lax.broadcasted_iota(I32, (blk, LANES), 1)
    row = lax.broadcasted_iota(I32, (blk, LANES), 0)
    in_chunk = row & (c - 1)
    chunk_of_row = row >> GLA_CHUNK_SHIFT
    r2 = lax.broadcasted_iota(I32, (2 * blk, blk), 0) & (blk - 1)
    c2 = lax.broadcasted_iota(I32, (2 * blk, blk), 1)
    intra = (r2 >= c2) & ((r2 >> GLA_CHUNK_SHIFT) == (c2 >> GLA_CHUNK_SHIFT))
    lane_s = lax.broadcasted_iota(I32, (GLA_DV, LANES), 1)
    n_blk = SEQ // blk

    def prep(i_blk):
        rows = slice(i_blk * blk, (i_blk + 1) * blk)
        la = _dot(misc_ref[rows, :], wa_ref[...], precision=lax.Precision.HIGHEST) + ba_ref[...]
        b = (jnp.minimum(la, 0.0) - jnp.log(1.0 + jnp.exp(-jnp.abs(la)))) * (1.0 / GLA_TAU)
        shift = 1
        while shift < c:
            b = b + jnp.where(in_chunk >= shift, pltpu.roll(b, shift, axis=0), 0.0)
            shift *= 2
        b3 = b.reshape(n_c, c, LANES)
        b_last = b3[:, c - 1:c, :]
        k = k_ref[rows, :]
        q_in = q_ref[rows, :] * (GLA_DK ** -0.5) * jnp.exp(b)
        k_in = (k * jnp.exp(-b)).astype(BF16)
        k_st = (k.reshape(n_c, c, LANES) * jnp.exp(b_last - b3)).reshape(blk, LANES)
        decay = jnp.exp(b_last)
        v = v_ref[rows, :]

        q2 = jnp.concatenate([jnp.where(lane < GLA_DK, q_in, 0.0), jnp.where(lane >= GLA_DK, q_in, 0.0)],
                             axis=0).astype(BF16)
        k_cols = jnp.concatenate([jnp.where(chunk_of_row == i_c, k_st, 0.0) for i_c in range(n_c)], axis=1)
        return q2, k_in, k_cols.astype(BF16), v, decay

    def chunk_parallel(q2, k_in, k_cols, v, decay):
        a = jnp.where(intra, _dot_nt(q2, k_in), 0.0).astype(BF16)
        o_intra = _dot(a, v)
        inc = _dot_tn(v, k_cols)
        return q2, o_intra, inc, decay

    per_blk = []
    ready = prep(0)
    for i_blk in range(n_blk):
        upcoming = prep(i_blk + 1) if i_blk + 1 < n_blk else None
        per_blk.append(chunk_parallel(*ready))
        ready = upcoming

    st = jnp.zeros((GLA_DV, LANES), F32)
    state_before = []
    for _, _, inc, decay in per_blk:
        for i_c in range(n_c):
            state_before.append(st.astype(BF16))
            cols = slice(i_c * LANES, (i_c + 1) * LANES)
            st = st * decay[i_c] + jnp.where(lane_s < GLA_DK, inc[:GLA_DV, cols], inc[GLA_DV:, cols])

    for i_blk, (q2, o_intra, _, _) in enumerate(per_blk):
        rows = slice(i_blk * blk, (i_blk + 1) * blk)
        o_inter = []
        for i_c in range(n_c):
            q_c = jnp.concatenate([q2[i_c * c:(i_c + 1) * c], q2[blk + i_c * c:blk + (i_c + 1) * c]], axis=0)
            o_inter.append(_dot_nt(q_c, state_before[i_blk * n_c + i_c]))
        for h in range(2):
            cols = slice(h * GLA_DV, (h + 1) * GLA_DV)
            o = o_intra[h * blk:(h + 1) * blk, cols] + jnp.concatenate(
                [o_inter[i_c][h * c:(h + 1) * c] for i_c in range(n_c)], axis=0)
            gate = r_ref[rows, cols]
            o_ref[rows, cols] = (_rms(o, gn_ref[...]) * (gate * jax.nn.sigmoid(gate))).astype(BF16)


def _gla(gq, gk, gv, gr, misc, wa, ba, gn):
    n = gq.shape[0]
    b = n // SEQ
    return pl.pallas_call(
        _gla_kernel,
        grid=(b, GLA_HEADS // 2),
        in_specs=[
            pl.BlockSpec((SEQ, LANES), lambda i, p: (i, p)),
            pl.BlockSpec((SEQ, LANES), lambda i, p: (i, p)),
            pl.BlockSpec((SEQ, 2 * GLA_DV), lambda i, p: (i, p)),
            pl.BlockSpec((SEQ, 2 * GLA_DV), lambda i, p: (i, p)),
            pl.BlockSpec((SEQ, LANES), lambda i, p: (i, 0)),
            pl.BlockSpec((LANES, LANES), lambda i, p: (0, p)),
            pl.BlockSpec((1, LANES), lambda i, p: (0, p)),
            pl.BlockSpec((1, GLA_DV), lambda i, p: (0, 0)),
        ],
        out_specs=pl.BlockSpec((SEQ, 2 * GLA_DV), lambda i, p: (i, p)),
        out_shape=jax.ShapeDtypeStruct((n, GLA_HEADS * GLA_DV), BF16),
        compiler_params=_params(("parallel", "parallel"), 40 << 20),
        name="gla",
    )(gq, gk, gv, gr, misc, wa, ba, gn)


def _permute_w_in(w):
    w = w.astype(BF16)
    d = w.shape[0]
    col = lambda lo, hi: w[:, lo:hi]
    zeros = lambda k: jnp.zeros((d, k), w.dtype)
    hd = HEAD_DIM

    def keys_padded(k0):
        return [col(k0, k0 + hd), zeros(LANES - hd), col(k0 + hd, k0 + 2 * hd), zeros(LANES - hd)]

    n_gate = 3 * NSA_REP
    misc = [col(1280, 1280 + n_gate), col(2840, 2856), zeros(LANES - n_gate - GLA_LOWRANK),
            col(1280 + n_gate, 1304), zeros(LANES - n_gate)]
    parts = ([col(0, 512), col(512, 640), col(640, 768)] + keys_padded(768) + keys_padded(1024)
             + [col(896, 1024), col(1152, 1280)] + misc
             + [col(1304, 1560), col(1560, 1816), col(1816, 2328), col(2328, 2840)])
    out = jnp.concatenate(parts, axis=1)
    assert out.shape[1] == D_IN_PAD
    return out


def _compress_weights(pe, w1, w2, group_stride):
    hd, hid = HEAD_DIM, CMP_HIDDEN
    pe2 = jnp.concatenate([pe, pe], axis=1)
    w1 = w1.reshape(CMP_BLOCK, hd, hid)
    z1 = jnp.zeros_like(w1)
    w1e = jnp.concatenate([jnp.concatenate([w1, z1], axis=2), jnp.concatenate([z1, w1], axis=2)], axis=1)
    w2e = jnp.zeros((NSA_GROUPS * hid, NSA_GROUPS * group_stride), w2.dtype)
    for g in range(NSA_GROUPS):
        w2e = w2e.at[g * hid:(g + 1) * hid, g * group_stride:g * group_stride + hd].set(w2)
    return pe2, w1e.astype(BF16), w2e.astype(BF16)


def kernel(x, ffn1_norm, ffn1_w_gate, ffn1_w_up, ffn1_w_down, mix_norm, w_in, nsa_pe_k, nsa_w1_k, nsa_w2_k,
           nsa_pe_v, nsa_w1_v, nsa_w2_v, gla_w_a2, gla_b_a, gla_norm, w_out, ffn2_norm, ffn2_w_gate,
           ffn2_w_up, ffn2_w_down, final_norm):
    bsz, seq, d = x.shape
    assert (seq, d) == (SEQ, D_MODEL) and ffn1_norm.shape[0] == 1
    n = bsz * seq
    xf = x.reshape(n, d)
    row = lambda v: v.reshape(1, -1).astype(F32)
    bf = lambda w: w.astype(BF16)
    ones = jnp.ones((1, d), F32)

    x1 = _ffn(xf, (), row(ffn1_norm[0]), bf(ffn1_w_gate[0]), bf(ffn1_w_up[0]), bf(ffn1_w_down[0]), ones,
              final_norm=False)

    q, kc, vc, ksa, kwa, vs, vw, misc, gq, gk, gv, gr = _inproj(x1, row(mix_norm[0]), _permute_w_in(w_in[0]))

    pek, w1k, w2k = _compress_weights(nsa_pe_k[0], nsa_w1_k[0], nsa_w2_k[0], LANES)
    pev, w1v, w2v = _compress_weights(nsa_pe_v[0], nsa_w1_v[0], nsa_w2_v[0], HEAD_DIM)
    kca, vc2 = _compress(kc, vc, pek, pev, w1k, w1v, w2k, w2v)

    o_nsa = _nsa(q, misc, kca, vc2, ksa, vs, kwa, vw)

    wa = jnp.zeros((LANES, GLA_HEADS * GLA_DK), F32).at[MISC_GA_OFF:MISC_GA_OFF + GLA_LOWRANK].set(gla_w_a2[0])
    o_gla = _gla(gq, gk, gv, gr, misc, wa, row(gla_b_a[0]), row(gla_norm[0]))

    d_nsa = NSA_HEADS * HEAD_DIM
    w_o = bf(w_out[0])
    out = _ffn(x1, (o_nsa, o_gla, w_o[:d_nsa], w_o[d_nsa:]), row(ffn2_norm[0]), bf(ffn2_w_gate[0]),
               bf(ffn2_w_up[0]), bf(ffn2_w_down[0]), row(final_norm), final_norm=True)
    return out.reshape(bsz, seq, d)
```

```python
import functools

import jax
import jax.numpy as jnp
from jax import lax
from jax.experimental import pallas as pl
from jax.experimental.pallas import tpu as pltpu

F32 = jnp.float32
BF16 = jnp.bfloat16
I32 = jnp.int32

D_MODEL = 1024
SEQ = 2048
D_FF = 2816
EPS = 1e-6
NEG_INF = -1e30
FORCE = 1e9

NSA_HEADS = 8
NSA_GROUPS = 2
NSA_REP = NSA_HEADS // NSA_GROUPS
HEAD_DIM = 64
CMP_BLOCK = 32
CMP_STRIDE = 16
CMP_HIDDEN = 128
N_CMP_PAD = SEQ // CMP_STRIDE
SEL_BLOCK = 64
SEL_SHIFT = 6
assert 1 << SEL_SHIFT == SEL_BLOCK
N_SEL = SEQ // SEL_BLOCK
SEL_TOPK = 8
WINDOW = 512
Q_BLOCK = 256
N_QB = SEQ // Q_BLOCK
KV_CHUNK = Q_BLOCK

GLA_HEADS = 4
GLA_DK = 64
GLA_DV = 128
GLA_CHUNK = 64
GLA_CHUNK_SHIFT = 6
assert 1 << GLA_CHUNK_SHIFT == GLA_CHUNK
GLA_BLOCK = 256
GLA_LOWRANK = 16
GLA_TAU = 16.0

LANES = 128
V7X_VMEM_BYTES = 64 * 1024 * 1024

C_Q = (0, 512)
C_KC = (512, 640)
C_VC = (640, 768)
C_KSA = (768, 1024)
C_KWA = (1024, 1280)
C_VS = (1280, 1408)
C_VW = (1408, 1536)
C_MISC = (1536, 1792)
C_GQ = (1792, 2048)
C_GK = (2048, 2304)
C_GV = (2304, 2816)
C_GR = (2816, 3328)
D_IN_PAD = 3328
MISC_GA_OFF = 12

X_SEL = HEAD_DIM
X_HI = HEAD_DIM + N_SEL
X_LO = X_HI + 1
POS_SPLIT = 64
POS_SHIFT = 6
assert 1 << POS_SHIFT == POS_SPLIT and X_LO < LANES


def _params(sem, vmem_bytes):
    return pltpu.CompilerParams(dimension_semantics=sem,
                                vmem_limit_bytes=min(int(vmem_bytes), V7X_VMEM_BYTES - (8 << 20)))


def _rms(x, g):
    return x * lax.rsqrt(jnp.mean(x * x, axis=-1, keepdims=True) + EPS) * g


def _dot(a, b, **kw):
    return jnp.dot(a, b, preferred_element_type=F32, **kw)


def _dot_nt(a, b, **kw):
    return lax.dot_general(a, b, (((1,), (1,)), ((), ())), preferred_element_type=F32, **kw)


def _dot_tn(a, b, **kw):
    return lax.dot_general(a, b, (((0,), (0,)), ((), ())), preferred_element_type=F32, **kw)


def _ffn_kernel(*refs, tf, final_norm, mixer_out):
    if mixer_out:
        x_ref, a_ref, b_ref, wa_ref, wb_ref, g_ref, wg_ref, wu_ref, wd_ref, fg_ref, o_ref, act_ref = refs
        x = x_ref[...] + _dot(a_ref[...], wa_ref[...]) + _dot(b_ref[...], wb_ref[...])
    else:
        x_ref, g_ref, wg_ref, wu_ref, wd_ref, fg_ref, o_ref, act_ref = refs
        x = x_ref[...]
    h = _rms(x, g_ref[...]).astype(BF16)
    for c in range(D_FF // tf):
        cols = slice(c * tf, (c + 1) * tf)
        gate = _dot(h, wg_ref[:, cols])
        up = _dot(h, wu_ref[:, cols])
        act_ref[:, cols] = ((gate * jax.nn.sigmoid(gate)) * up).astype(BF16)
    y = x + 0.5 * _dot(act_ref[...], wd_ref[...])
    if final_norm:
        y = _rms(y, fg_ref[...])
    o_ref[...] = y


def _ffn(x, mixer, gain, wg, wu, wd, final_gain, *, final_norm, tm=512, tf=256):
    n, d = x.shape
    assert D_FF % tf == 0
    resident = lambda shape: pl.BlockSpec(shape, lambda i: (0, 0), pipeline_mode=pl.Buffered(1))
    rows = lambda width: pl.BlockSpec((tm, width), lambda i: (i, 0))
    vmem = 2 * 2 * tm * d * 4 + 3 * d * D_FF * 2 + tm * D_FF * 2 + tm * d * 2 + 4 * tm * tf * 4 + (8 << 20)
    mix_specs = []
    if mixer:
        a, b, w_a, w_b = mixer
        mix_specs = [rows(a.shape[1]), rows(b.shape[1]), resident(w_a.shape), resident(w_b.shape)]
        vmem += 2 * tm * (a.shape[1] + b.shape[1]) * 2 + (w_a.size + w_b.size) * 2
    return pl.pallas_call(
        functools.partial(_ffn_kernel, tf=tf, final_norm=final_norm, mixer_out=bool(mixer)),
        grid=(n // tm,),
        in_specs=[rows(d)] + mix_specs + [
            resident((1, d)),
            resident((d, D_FF)),
            resident((d, D_FF)),
            resident((D_FF, d)),
            resident((1, d)),
        ],
        out_specs=rows(d),
        out_shape=jax.ShapeDtypeStruct((n, d), F32),
        scratch_shapes=[pltpu.VMEM((tm, D_FF), BF16)],
        compiler_params=_params(("parallel",), vmem),
        name="ffn",
    )(x, *mixer, gain, wg, wu, wd, final_gain)


def _key_extras(pos, lane, with_block):
    lg = lane & (LANES - 1)
    ext = jnp.where(lg == X_HI, (pos >> POS_SHIFT).astype(F32),
                    jnp.where(lg == X_LO, (pos & (POS_SPLIT - 1)).astype(F32), 0.0))
    if with_block:
        ext = jnp.where((lg >= X_SEL) & (lg < X_HI) & ((pos >> SEL_SHIFT) == lg - X_SEL), 1.0, ext)
    return ext


def _inproj_kernel(x_ref, g_ref, w_ref, q_ref, kc_ref, vc_ref, ksa_ref, kwa_ref, vs_ref, vw_ref, misc_ref,
                   gq_ref, gk_ref, gv_ref, gr_ref, *, tm):
    h = _rms(x_ref[...], g_ref[...]).astype(BF16)

    def proj(c):
        return _dot(h, w_ref[:, c[0]:c[1]])

    shape = (tm, 2 * LANES)
    lane = lax.broadcasted_iota(I32, shape, 1)
    pos = (pl.program_id(0) * tm + lax.broadcasted_iota(I32, shape, 0)) & (SEQ - 1)
    is_key = (lane & (LANES - 1)) < HEAD_DIM

    q_ref[...] = (proj(C_Q) * (HEAD_DIM ** -0.5)).astype(BF16)
    kc_ref[...] = proj(C_KC)
    vc_ref[...] = proj(C_VC)
    ksa_ref[...] = jnp.where(is_key, proj(C_KSA), _key_extras(pos, lane, True)).astype(BF16)
    kwa_ref[...] = jnp.where(is_key, proj(C_KWA), _key_extras(pos, lane, False)).astype(BF16)
    vs_ref[...] = proj(C_VS).astype(BF16)
    vw_ref[...] = proj(C_VW).astype(BF16)
    misc_ref[...] = proj(C_MISC)
    gq_ref[...] = proj(C_GQ)
    gk_ref[...] = proj(C_GK)
    gv_ref[...] = proj(C_GV).astype(BF16)
    gr_ref[...] = proj(C_GR)


def _inproj(x, gain, w, tm=512):
    n, d = x.shape
    assert SEQ % tm == 0 and SEQ & (SEQ - 1) == 0
    outs = [(C_Q, BF16), (C_KC, F32), (C_VC, F32), (C_KSA, BF16), (C_KWA, BF16), (C_VS, BF16), (C_VW, BF16),
            (C_MISC, F32), (C_GQ, F32), (C_GK, F32), (C_GV, BF16), (C_GR, F32)]
    vmem = 2 * tm * d * 4 + 2 * d * D_IN_PAD * 2 + 2 * tm * D_IN_PAD * 4 + (8 << 20)
    return pl.pallas_call(
        functools.partial(_inproj_kernel, tm=tm),
        grid=(n // tm,),
        in_specs=[
            pl.BlockSpec((tm, d), lambda i: (i, 0)),
            pl.BlockSpec((1, d), lambda i: (0, 0)),
            pl.BlockSpec((d, D_IN_PAD), lambda i: (0, 0)),
        ],
        out_specs=[pl.BlockSpec((tm, c[1] - c[0]), lambda i: (i, 0)) for c, _ in outs],
        out_shape=[jax.ShapeDtypeStruct((n, c[1] - c[0]), dt) for c, dt in outs],
        compiler_params=_params(("parallel",), vmem),
        name="inproj",
    )(x, gain, w)


def _compress_kernel(kc_ref, vc_ref, pek_ref, pev_ref, w1k_ref, w1v_ref, w2k_ref, w2v_ref, kca_ref, vc2_ref):
    half = CMP_BLOCK // 2

    def hidden(x_ref, pe_ref, w1_ref):
        acc_a = jnp.zeros((N_CMP_PAD, 2 * CMP_HIDDEN), F32)
        acc_b = jnp.zeros((N_CMP_PAD, 2 * CMP_HIDDEN), F32)
        for l in range(half):
            rows = x_ref[pl.ds(l, N_CMP_PAD, stride=CMP_STRIDE), :]
            acc_a += _dot((rows + pe_ref[l:l + 1, :]).astype(BF16), w1_ref[l])
            acc_b += _dot((rows + pe_ref[half + l:half + l + 1, :]).astype(BF16), w1_ref[half + l])
        pre = acc_a + pltpu.roll(acc_b, N_CMP_PAD - 1, axis=0)
        return jax.nn.gelu(pre, approximate=True).astype(BF16)

    kc = _dot(hidden(kc_ref, pek_ref, w1k_ref), w2k_ref[...])
    vc = _dot(hidden(vc_ref, pev_ref, w1v_ref), w2v_ref[...])
    lane = lax.broadcasted_iota(I32, kc.shape, 1)
    row = lax.broadcasted_iota(I32, kc.shape, 0)
    kc = jnp.where((lane & (LANES - 1)) < HEAD_DIM, kc, _key_extras(2 * CMP_STRIDE * row + CMP_BLOCK - 1, lane, False))
    kca_ref[0] = jnp.where(row < N_CMP_PAD - 1, kc, 0.0).astype(BF16)
    vc2_ref[0] = jnp.where(row[:, :LANES] < N_CMP_PAD - 1, vc, 0.0).astype(BF16)


def _compress(kc, vc, pek, pev, w1k, w1v, w2k, w2v):
    n = kc.shape[0]
    b = n // SEQ
    full = lambda a: pl.BlockSpec(a.shape, lambda i: (0,) * a.ndim)
    return pl.pallas_call(
        _compress_kernel,
        grid=(b,),
        in_specs=[
            pl.BlockSpec((SEQ, LANES), lambda i: (i, 0)),
            pl.BlockSpec((SEQ, LANES), lambda i: (i, 0)),
            full(pek), full(pev), full(w1k), full(w1v), full(w2k), full(w2v),
        ],
        out_specs=[pl.BlockSpec((1, N_CMP_PAD, 2 * LANES), lambda i: (i, 0, 0)),
                   pl.BlockSpec((1, N_CMP_PAD, LANES), lambda i: (i, 0, 0))],
        out_shape=[jax.ShapeDtypeStruct((b, N_CMP_PAD, 2 * LANES), BF16),
                   jax.ShapeDtypeStruct((b, N_CMP_PAD, LANES), BF16)],
        compiler_params=_params(("parallel",), 32 << 20),
        name="compress",
    )(kc, vc, pek, pev, w1k, w1v, w2k, w2v)


def _attend(q4, k, v, chunk_masks):
    s_all = _dot_nt(q4, k)
    p_parts, inv = [], []
    for r in range(NSA_REP):
        cols = []
        for c, allowed in enumerate(chunk_masks):
            s = s_all[r * Q_BLOCK:(r + 1) * Q_BLOCK, c * KV_CHUNK:(c + 1) * KV_CHUNK]
            cols.append(s if allowed is None else jnp.where(allowed, s, NEG_INF))
        s = cols[0] if len(cols) == 1 else jnp.concatenate(cols, axis=1)
        p = jnp.exp(s - jnp.max(s, axis=-1, keepdims=True))
        inv.append(1.0 / jnp.sum(p, axis=-1, keepdims=True))
        p_parts.append(p.astype(BF16))
    pv = _dot(jnp.concatenate(p_parts, axis=0), v)
    return [pv[r * Q_BLOCK:(r + 1) * Q_BLOCK] * inv[r] for r in range(NSA_REP)]


def _nsa_kernel(q_ref, misc_ref, kca_ref, vc2_ref, ksa_ref, vs_ref, kwa_ref, vw_ref, o_ref):
    for qb in range(N_QB):
        @pl.when(pl.program_id(1) == qb)
        def _(qb=qb):
            refs = (q_ref, misc_ref, kca_ref, vc2_ref, ksa_ref, vs_ref, kwa_ref, vw_ref, o_ref)
            selected = [_nsa_select(qb, g, *refs) for g in range(NSA_GROUPS)]
            for g in range(NSA_GROUPS):
                _nsa_attend_and_store(qb, g, selected[g], *refs)


def _group_cols(g, width=LANES):
    return slice(g * width, (g + 1) * width)


def _nsa_select(qb, g, q_ref, misc_ref, kca_ref, vc2_ref, ksa_ref, vs_ref, kwa_ref, vw_ref, o_ref):
    t0 = qb * Q_BLOCK
    slopes = [2.0 ** -(g * NSA_REP + r + 1) for r in range(NSA_REP)]

    lane = lax.broadcasted_iota(I32, (Q_BLOCK, LANES), 1)
    sub = lax.broadcasted_iota(I32, (Q_BLOCK, LANES), 0)
    low = lane < HEAD_DIM
    tq = t0 + sub

    qf = q_ref[:, _group_cols(g, NSA_REP * HEAD_DIM)].astype(F32)
    q_tiles = []
    for r in range(NSA_REP):
        tile = qf[:, (r // 2) * LANES:(r // 2 + 1) * LANES]
        if r % 2:
            tile = pltpu.roll(tile, HEAD_DIM, axis=1)
        q_tiles.append(jnp.where(low, tile, 0.0))

    def stack_queries(slope_scale, sel_lanes):
        parts = []
        for r in range(NSA_REP):
            ext = jnp.where(lane == X_HI, POS_SPLIT * slope_scale * slopes[r],
                            jnp.where(lane == X_LO, slope_scale * slopes[r], sel_lanes))
            parts.append(jnp.where(low, q_tiles[r], ext).astype(BF16))
        return jnp.concatenate(parts, axis=0)

    kca = kca_ref[0][:, _group_cols(g)]
    s_c = _dot_nt(stack_queries(0.5, 0.0), kca)
    valid_c = lane * CMP_STRIDE + (CMP_BLOCK - 1) <= tq
    p_sum = jnp.zeros((Q_BLOCK, LANES), F32)
    p_parts = []
    for r in range(NSA_REP):
        s = jnp.where(valid_c, s_c[r * Q_BLOCK:(r + 1) * Q_BLOCK], NEG_INF)
        e = jnp.exp(s - jnp.max(s, axis=-1, keepdims=True))
        p = jnp.where(valid_c, e / jnp.sum(e, axis=-1, keepdims=True), 0.0)
        p_sum += p
        p_parts.append(p.astype(BF16))
    o_cmp = _dot(jnp.concatenate(p_parts, axis=0), vc2_ref[0])

    ov_j = lax.broadcasted_iota(I32, (LANES, LANES), 0)
    ov_i = lax.broadcasted_iota(I32, (LANES, LANES), 1)
    ov_t = ((ov_i < 4 * ov_j + 4) & (ov_i > 4 * ov_j - 2) & (ov_j < N_SEL)).astype(F32)
    imp_t = _dot_nt(ov_t, p_sum, precision=lax.Precision.HIGHEST)[:N_SEL]
    j_blk = lax.broadcasted_iota(I32, (N_SEL, Q_BLOCK), 0)
    cur = (t0 + lax.broadcasted_iota(I32, (N_SEL, Q_BLOCK), 1)) >> SEL_SHIFT
    forced = (j_blk == 0) | (j_blk == cur) | (j_blk == cur - 1)
    imp_t = jnp.where(forced, FORCE, imp_t)
    imp_t = jnp.where(j_blk <= cur, imp_t, -FORCE)
    rank = jnp.zeros((N_SEL, Q_BLOCK), I32)
    for jp in range(N_SEL):
        row = imp_t[jp:jp + 1, :]
        beats = (row > imp_t) | ((row == imp_t) & (j_blk > jp))
        rank += beats.astype(I32)
    drop_t = jnp.where(rank < SEL_TOPK, 0.0, NEG_INF)
    drop_t = jnp.concatenate([drop_t, jnp.zeros((LANES - N_SEL, Q_BLOCK), F32)], axis=0)
    drop_q = jnp.concatenate([drop_t[:, i * LANES:(i + 1) * LANES].T for i in range(Q_BLOCK // LANES)], axis=0)
    drop_q = pltpu.roll(drop_q, X_SEL, axis=1)
    return o_cmp, stack_queries(1.0, drop_q), stack_queries(1.0, 0.0)


def _nsa_attend_and_store(qb, g, selected, q_ref, misc_ref, kca_ref, vc2_ref, ksa_ref, vs_ref, kwa_ref, vw_ref,
                          o_ref):
    o_cmp, q4_sel, q4_plain = selected
    t0 = qb * Q_BLOCK

    def dist_to(c):
        return (t0 + lax.broadcasted_iota(I32, (Q_BLOCK, KV_CHUNK), 0)
                - (c * KV_CHUNK + lax.broadcasted_iota(I32, (Q_BLOCK, KV_CHUNK), 1)))

    c_diag = qb
    dist_diag = dist_to(c_diag)
    n_slc = (c_diag + 1) * KV_CHUNK

    o_slc = _attend(q4_sel, ksa_ref[:n_slc, _group_cols(g)], vs_ref[:n_slc, :], [None] * c_diag + [dist_diag >= 0])

    c_first = max(c_diag - WINDOW // KV_CHUNK, 0)
    masks = [None] * (c_diag - c_first + 1)
    masks[-1] = dist_diag >= 0
    if c_diag - c_first == WINDOW // KV_CHUNK:
        masks[0] = dist_to(c_first) < WINDOW
    win_rows = slice(c_first * KV_CHUNK, n_slc)
    o_win = _attend(q4_plain, kwa_ref[win_rows, _group_cols(g)], vw_ref[win_rows, :], masks)

    low = lax.broadcasted_iota(I32, (Q_BLOCK, LANES), 1) < HEAD_DIM
    gates = jax.nn.sigmoid(misc_ref[:, _group_cols(g)])
    gate_of = lambda k: gates[:, k:k + 1]
    heads = []
    for r in range(NSA_REP):
        heads.append(gate_of(3 * r) * o_cmp[r * Q_BLOCK:(r + 1) * Q_BLOCK]
                     + gate_of(3 * r + 1) * o_slc[r]
                     + gate_of(3 * r + 2) * o_win[r])
    for pair in range(NSA_REP // 2):
        even, odd = heads[2 * pair], heads[2 * pair + 1]
        left = even if g == 0 else pltpu.roll(even, HEAD_DIM, axis=1)
        right = pltpu.roll(odd, HEAD_DIM, axis=1) if g == 0 else odd
        cols = slice((g * NSA_REP // 2 + pair) * LANES, (g * NSA_REP // 2 + pair + 1) * LANES)
        o_ref[:, cols] = jnp.where(low, left, right).astype(BF16)


def _nsa(q, misc, kca, vc2, ksa, vs, kwa, vw):
    n = q.shape[0]
    b = n // SEQ
    assert WINDOW % KV_CHUNK == 0 and KV_CHUNK == Q_BLOCK
    q_rows = lambda width: pl.BlockSpec((Q_BLOCK, width), lambda i, c: (i * N_QB + c, 0))
    per_batch = lambda a: pl.BlockSpec((1,) + a.shape[1:], lambda i, c: (i, 0, 0))
    per_seq = lambda width: pl.BlockSpec((SEQ, width), lambda i, c: (i, 0))
    return pl.pallas_call(
        _nsa_kernel,
        grid=(b, N_QB),
        in_specs=[q_rows(q.shape[1]), q_rows(misc.shape[1]), per_batch(kca), per_batch(vc2),
                  per_seq(ksa.shape[1]), per_seq(vs.shape[1]), per_seq(kwa.shape[1]), per_seq(vw.shape[1])],
        out_specs=q_rows(q.shape[1]),
        out_shape=jax.ShapeDtypeStruct(q.shape, BF16),
        compiler_params=_params(("parallel", "arbitrary"), 56 << 20),
        name="nsa",
    )(q, misc, kca, vc2, ksa, vs, kwa, vw)


def _gla_kernel(q_ref, k_ref, v_ref, r_ref, misc_ref, wa_ref, ba_ref, gn_ref, o_ref):
    c, blk = GLA_CHUNK, GLA_BLOCK
    n_c = blk // c
    lane = lax.broadcasted_iota(I32, (blk, LANES), 1)
    row = lax.broadcasted_iota(I32, (blk, LANES), 0)
    in_chunk = row & (c - 1)
    chunk_of_row = row >> GLA_CHUNK_SHIFT
    r2 = lax.broadcasted_iota(I32, (2 * blk, blk), 0) & (blk - 1)
    c2 = lax.broadcasted_iota(I32, (2 * blk, blk), 1)
    intra = (r2 >= c2) & ((r2 >> GLA_CHUNK_SHIFT) == (c2 >> GLA_CHUNK_SHIFT))
    lane_s = lax.broadcasted_iota(I32, (GLA_DV, LANES), 1)
    n_blk = SEQ // blk

    def prep(i_blk):
        rows = slice(i_blk * blk, (i_blk + 1) * blk)
        la = _dot(misc_ref[rows, :], wa_ref[...], precision=lax.Precision.HIGHEST) + ba_ref[...]
        b = (jnp.minimum(la, 0.0) - jnp.log(1.0 + jnp.exp(-jnp.abs(la)))) * (1.0 / GLA_TAU)
        shift = 1
        while shift < c:
            b = b + jnp.where(in_chunk >= shift, pltpu.roll(b, shift, axis=0), 0.0)
            shift *= 2
        b3 = b.reshape(n_c, c, LANES)
        b_last = b3[:, c - 1:c, :]
        k = k_ref[rows, :]
        q_in = q_ref[rows, :] * (GLA_DK ** -0.5) * jnp.exp(b)
        k_in = (k * jnp.exp(-b)).astype(BF16)
        k_st = (k.reshape(n_c, c, LANES) * jnp.exp(b_last - b3)).reshape(blk, LANES)
        decay = jnp.exp(b_last)
        v = v_ref[rows, :]

        q2 = jnp.concatenate([jnp.where(lane < GLA_DK, q_in, 0.0), jnp.where(lane >= GLA_DK, q_in, 0.0)],
                             axis=0).astype(BF16)
        k_cols = jnp.concatenate([jnp.where(chunk_of_row == i_c, k_st, 0.0) for i_c in range(n_c)], axis=1)
        return q2, k_in, k_cols.astype(BF16), v, decay

    def chunk_parallel(q2, k_in, k_cols, v, decay):
        a = jnp.where(intra, _dot_nt(q2, k_in), 0.0).astype(BF16)
        o_intra = _dot(a, v)
        inc = _dot_tn(v, k_cols)
        return q2, o_intra, inc, decay

    per_blk = []
    ready = prep(0)
    for i_blk in range(n_blk):
        upcoming = prep(i_blk + 1) if i_blk + 1 < n_blk else None
        per_blk.append(chunk_parallel(*ready))
        ready = upcoming

    st = jnp.zeros((GLA_DV, LANES), F32)
    state_before = []
    for _, _, inc, decay in per_blk:
        for i_c in range(n_c):
            state_before.append(st.astype(BF16))
            cols = slice(i_c * LANES, (i_c + 1) * LANES)
            st = st * decay[i_c] + jnp.where(lane_s < GLA_DK, inc[:GLA_DV, cols], inc[GLA_DV:, cols])

    for i_blk, (q2, o_intra, _, _) in enumerate(per_blk):
        rows = slice(i_blk * blk, (i_blk + 1) * blk)
        o_inter = []
        for i_c in range(n_c):
            q_c = jnp.concatenate([q2[i_c * c:(i_c + 1) * c], q2[blk + i_c * c:blk + (i_c + 1) * c]], axis=0)
            o_inter.append(_dot_nt(q_c, state_before[i_blk * n_c + i_c]))
        for h in range(2):
            cols = slice(h * GLA_DV, (h + 1) * GLA_DV)
            o = o_intra[h * blk:(h + 1) * blk, cols] + jnp.concatenate(
                [o_inter[i_c][h * c:(h + 1) * c] for i_c in range(n_c)], axis=0)
            gate = r_ref[rows, cols]
            o_ref[rows, cols] = (_rms(o, gn_ref[...]) * (gate * jax.nn.sigmoid(gate))).astype(BF16)


def _gla(gq, gk, gv, gr, misc, wa, ba, gn):
    n = gq.shape[0]
    b = n // SEQ
    return pl.pallas_call(
        _gla_kernel,
        grid=(b, GLA_HEADS // 2),
        in_specs=[
            pl.BlockSpec((SEQ, LANES), lambda i, p: (i, p)),
            pl.BlockSpec((SEQ, LANES), lambda i, p: (i, p)),
            pl.BlockSpec((SEQ, 2 * GLA_DV), lambda i, p: (i, p)),
            pl.BlockSpec((SEQ, 2 * GLA_DV), lambda i, p: (i, p)),
            pl.BlockSpec((SEQ, LANES), lambda i, p: (i, 0)),
            pl.BlockSpec((LANES, LANES), lambda i, p: (0, p)),
            pl.BlockSpec((1, LANES), lambda i, p: (0, p)),
            pl.BlockSpec((1, GLA_DV), lambda i, p: (0, 0)),
        ],
        out_specs=pl.BlockSpec((SEQ, 2 * GLA_DV), lambda i, p: (i, p)),
        out_shape=jax.ShapeDtypeStruct((n, GLA_HEADS * GLA_DV), BF16),
        compiler_params=_params(("parallel", "parallel"), 40 << 20),
        name="gla",
    )(gq, gk, gv, gr, misc, wa, ba, gn)


def _permute_w_in(w):
    w = w.astype(BF16)
    d = w.shape[0]
    col = lambda lo, hi: w[:, lo:hi]
    zeros = lambda k: jnp.zeros((d, k), w.dtype)
    hd = HEAD_DIM

    def keys_padded(k0):
        return [col(k0, k0 + hd), zeros(LANES - hd), col(k0 + hd, k0 + 2 * hd), zeros(LANES - hd)]

    n_gate = 3 * NSA_REP
    misc = [col(1280, 1280 + n_gate), col(2840, 2856), zeros(LANES - n_gate - GLA_LOWRANK),
            col(1280 + n_gate, 1304), zeros(LANES - n_gate)]
    parts = ([col(0, 512), col(512, 640), col(640, 768)] + keys_padded(768) + keys_padded(1024)
             + [col(896, 1024), col(1152, 1280)] + misc
             + [col(1304, 1560), col(1560, 1816), col(1816, 2328), col(2328, 2840)])
    out = jnp.concatenate(parts, axis=1)
    assert out.shape[1] == D_IN_PAD
    return out


def _compress_weights(pe, w1, w2, group_stride):
    hd, hid = HEAD_DIM, CMP_HIDDEN
    pe2 = jnp.concatenate([pe, pe], axis=1)
    w1 = w1.reshape(CMP_BLOCK, hd, hid)
    z1 = jnp.zeros_like(w1)
    w1e = jnp.concatenate([jnp.concatenate([w1, z1], axis=2), jnp.concatenate([z1, w1], axis=2)], axis=1)
    w2e = jnp.zeros((NSA_GROUPS * hid, NSA_GROUPS * group_stride), w2.dtype)
    for g in range(NSA_GROUPS):
        w2e = w2e.at[g * hid:(g + 1) * hid, g * group_stride:g * group_stride + hd].set(w2)
    return pe2, w1e.astype(BF16), w2e.astype(BF16)


def kernel(x, ffn1_norm, ffn1_w_gate, ffn1_w_up, ffn1_w_down, mix_norm, w_in, nsa_pe_k, nsa_w1_k, nsa_w2_k,
           nsa_pe_v, nsa_w1_v, nsa_w2_v, gla_w_a2, gla_b_a, gla_norm, w_out, ffn2_norm, ffn2_w_gate,
           ffn2_w_up, ffn2_w_down, final_norm):
    bsz, seq, d = x.shape
    assert (seq, d) == (SEQ, D_MODEL) and ffn1_norm.shape[0] == 1
    n = bsz * seq
    xf = x.reshape(n, d)
    row = lambda v: v.reshape(1, -1).astype(F32)
    bf = lambda w: w.astype(BF16)
    ones = jnp.ones((1, d), F32)

    x1 = _ffn(xf, (), row(ffn1_norm[0]), bf(ffn1_w_gate[0]), bf(ffn1_w_up[0]), bf(ffn1_w_down[0]), ones,
              final_norm=False)

    q, kc, vc, ksa, kwa, vs, vw, misc, gq, gk, gv, gr = _inproj(x1, row(mix_norm[0]), _permute_w_in(w_in[0]))

    pek, w1k, w2k = _compress_weights(nsa_pe_k[0], nsa_w1_k[0], nsa_w2_k[0], LANES)
    pev, w1v, w2v = _compress_weights(nsa_pe_v[0], nsa_w1_v[0], nsa_w2_v[0], HEAD_DIM)
    kca, vc2 = _compress(kc, vc, pek, pev, w1k, w1v, w2k, w2v)

    o_nsa = _nsa(q, misc, kca, vc2, ksa, vs, kwa, vw)

    wa = jnp.zeros((LANES, GLA_HEADS * GLA_DK), F32).at[MISC_GA_OFF:MISC_GA_OFF + GLA_LOWRANK].set(gla_w_a2[0])
    o_gla = _gla(gq, gk, gv, gr, misc, wa, row(gla_b_a[0]), row(gla_norm[0]))

    d_nsa = NSA_HEADS * HEAD_DIM
    w_o = bf(w_out[0])
    out = _ffn(x1, (o_nsa, o_gla, w_o[:d_nsa], w_o[d_nsa:]), row(ffn2_norm[0]), bf(ffn2_w_gate[0]),
               bf(ffn2_w_up[0]), bf(ffn2_w_down[0]), row(final_norm), final_norm=True)
    return out.reshape(bsz, seq, d)
```

```python
import functools

import jax
import jax.numpy as jnp
from jax import lax
from jax.experimental import pallas as pl
from jax.experimental.pallas import tpu as pltpu

F32 = jnp.float32
BF16 = jnp.bfloat16
I32 = jnp.int32

D_MODEL = 1024
SEQ = 2048
D_FF = 2816
EPS = 1e-6
NEG_INF = -1e30
FORCE = 1e9

NSA_HEADS = 8
NSA_GROUPS = 2
NSA_REP = NSA_HEADS // NSA_GROUPS
HEAD_DIM = 64
CMP_BLOCK = 32
CMP_STRIDE = 16
CMP_HIDDEN = 128
N_CMP_PAD = SEQ // CMP_STRIDE
SEL_BLOCK = 64
SEL_SHIFT = 6
assert 1 << SEL_SHIFT == SEL_BLOCK
N_SEL = SEQ // SEL_BLOCK
SEL_TOPK = 8
WINDOW = 512
Q_BLOCK = 256
N_QB = SEQ // Q_BLOCK
KV_CHUNK = Q_BLOCK
SEL_ROWS = 512

GLA_HEADS = 4
GLA_DK = 64
GLA_DV = 128
GLA_CHUNK = 64
GLA_CHUNK_SHIFT = 6
assert 1 << GLA_CHUNK_SHIFT == GLA_CHUNK
GLA_BLOCK = 256
GLA_LOWRANK = 16
GLA_TAU = 16.0

LANES = 128
V7X_VMEM_BYTES = 64 * 1024 * 1024

C_Q = (0, 512)
C_KC = (512, 640)
C_VC = (640, 768)
C_KSA = (768, 1024)
C_KWA = (1024, 1280)
C_VS = (1280, 1408)
C_VW = (1408, 1536)
C_MISC = (1536, 1792)
C_GQ = (1792, 2048)
C_GK = (2048, 2304)
C_GV = (2304, 2816)
C_GR = (2816, 3328)
D_IN_PAD = 3328
MISC_GA_OFF = 12

X_SEL = HEAD_DIM
X_HI = HEAD_DIM + N_SEL
X_LO = X_HI + 1
POS_SPLIT = 64
POS_SHIFT = 6
assert 1 << POS_SHIFT == POS_SPLIT and X_LO < LANES


def _params(sem, vmem_bytes):
    return pltpu.CompilerParams(dimension_semantics=sem,
                                vmem_limit_bytes=min(int(vmem_bytes), V7X_VMEM_BYTES - (8 << 20)))


def _rms(x, g):
    return x * lax.rsqrt(jnp.mean(x * x, axis=-1, keepdims=True) + EPS) * g


def _dot(a, b, **kw):
    return jnp.dot(a, b, preferred_element_type=F32, **kw)


def _dot_nt(a, b, **kw):
    return lax.dot_general(a, b, (((1,), (1,)), ((), ())), preferred_element_type=F32, **kw)


def _dot_tn(a, b, **kw):
    return lax.dot_general(a, b, (((0,), (0,)), ((), ())), preferred_element_type=F32, **kw)


def _ffn_kernel(*refs, tf, final_norm, mixer_out):
    if mixer_out:
        x_ref, a_ref, b_ref, wa_ref, wb_ref, g_ref, wg_ref, wu_ref, wd_ref, fg_ref, o_ref, act_ref = refs
        x = x_ref[...] + _dot(a_ref[...], wa_ref[...]) + _dot(b_ref[...], wb_ref[...])
    else:
        x_ref, g_ref, wg_ref, wu_ref, wd_ref, fg_ref, o_ref, act_ref = refs
        x = x_ref[...]
    h = _rms(x, g_ref[...]).astype(BF16)
    for c in range(D_FF // tf):
        cols = slice(c * tf, (c + 1) * tf)
        gate = _dot(h, wg_ref[:, cols])
        up = _dot(h, wu_ref[:, cols])
        act_ref[:, cols] = ((gate * jax.nn.sigmoid(gate)) * up).astype(BF16)
    y = x + 0.5 * _dot(act_ref[...], wd_ref[...])
    if final_norm:
        y = _rms(y, fg_ref[...])
    o_ref[...] = y


def _ffn(x, mixer, gain, wg, wu, wd, final_gain, *, final_norm, tm=512, tf=256):
    n, d = x.shape
    assert D_FF % tf == 0
    resident = lambda shape: pl.BlockSpec(shape, lambda i: (0, 0), pipeline_mode=pl.Buffered(1))
    rows = lambda width: pl.BlockSpec((tm, width), lambda i: (i, 0))
    vmem = 2 * 2 * tm * d * 4 + 3 * d * D_FF * 2 + tm * D_FF * 2 + tm * d * 2 + 4 * tm * tf * 4 + (8 << 20)
    mix_specs = []
    if mixer:
        a, b, w_a, w_b = mixer
        mix_specs = [rows(a.shape[1]), rows(b.shape[1]), resident(w_a.shape), resident(w_b.shape)]
        vmem += 2 * tm * (a.shape[1] + b.shape[1]) * 2 + (w_a.size + w_b.size) * 2
    return pl.pallas_call(
        functools.partial(_ffn_kernel, tf=tf, final_norm=final_norm, mixer_out=bool(mixer)),
        grid=(n // tm,),
        in_specs=[rows(d)] + mix_specs + [
            resident((1, d)),
            resident((d, D_FF)),
            resident((d, D_FF)),
            resident((D_FF, d)),
            resident((1, d)),
        ],
        out_specs=rows(d),
        out_shape=jax.ShapeDtypeStruct((n, d), F32),
        scratch_shapes=[pltpu.VMEM((tm, D_FF), BF16)],
        compiler_params=_params(("parallel",), vmem),
        name="ffn",
    )(x, *mixer, gain, wg, wu, wd, final_gain)


def _key_extras(pos, lane, with_block):
    lg = lane & (LANES - 1)
    ext = jnp.where(lg == X_HI, (pos >> POS_SHIFT).astype(F32),
                    jnp.where(lg == X_LO, (pos & (POS_SPLIT - 1)).astype(F32), 0.0))
    if with_block:
        ext = jnp.where((lg >= X_SEL) & (lg < X_HI) & ((pos >> SEL_SHIFT) == lg - X_SEL), 1.0, ext)
    return ext


def _inproj_kernel(x_ref, g_ref, w_ref, q_ref, kc_ref, vc_ref, ksa_ref, kwa_ref, vs_ref, vw_ref, misc_ref,
                   gq_ref, gk_ref, gv_ref, gr_ref, *, tm):
    h = _rms(x_ref[...], g_ref[...]).astype(BF16)

    def proj(c):
        return _dot(h, w_ref[:, c[0]:c[1]])

    shape = (tm, 2 * LANES)
    lane = lax.broadcasted_iota(I32, shape, 1)
    pos = (pl.program_id(0) * tm + lax.broadcasted_iota(I32, shape, 0)) & (SEQ - 1)
    is_key = (lane & (LANES - 1)) < HEAD_DIM

    q_ref[...] = (proj(C_Q) * (HEAD_DIM ** -0.5)).astype(BF16)
    kc_ref[...] = proj(C_KC)
    vc_ref[...] = proj(C_VC)
    ksa_ref[...] = jnp.where(is_key, proj(C_KSA), _key_extras(pos, lane, True)).astype(BF16)
    kwa_ref[...] = jnp.where(is_key, proj(C_KWA), _key_extras(pos, lane, False)).astype(BF16)
    vs_ref[...] = proj(C_VS).astype(BF16)
    vw_ref[...] = proj(C_VW).astype(BF16)
    misc_ref[...] = proj(C_MISC)
    gq_ref[...] = proj(C_GQ)
    gk_ref[...] = proj(C_GK)
    gv_ref[...] = proj(C_GV).astype(BF16)
    gr_ref[...] = proj(C_GR)


def _inproj(x, gain, w, tm=512):
    n, d = x.shape
    assert SEQ % tm == 0 and SEQ & (SEQ - 1) == 0
    outs = [(C_Q, BF16), (C_KC, F32), (C_VC, F32), (C_KSA, BF16), (C_KWA, BF16), (C_VS, BF16), (C_VW, BF16),
            (C_MISC, F32), (C_GQ, F32), (C_GK, F32), (C_GV, BF16), (C_GR, F32)]
    vmem = 2 * tm * d * 4 + 2 * d * D_IN_PAD * 2 + 2 * tm * D_IN_PAD * 4 + (8 << 20)
    return pl.pallas_call(
        functools.partial(_inproj_kernel, tm=tm),
        grid=(n // tm,),
        in_specs=[
            pl.BlockSpec((tm, d), lambda i: (i, 0)),
            pl.BlockSpec((1, d), lambda i: (0, 0)),
            pl.BlockSpec((d, D_IN_PAD), lambda i: (0, 0)),
        ],
        out_specs=[pl.BlockSpec((tm, c[1] - c[0]), lambda i: (i, 0)) for c, _ in outs],
        out_shape=[jax.ShapeDtypeStruct((n, c[1] - c[0]), dt) for c, dt in outs],
        compiler_params=_params(("parallel",), vmem),
        name="inproj",
    )(x, gain, w)


def _compress_kernel(kc_ref, vc_ref, pek_ref, pev_ref, w1k_ref, w1v_ref, w2k_ref, w2v_ref, kca_ref, vc2_ref):
    half = CMP_BLOCK // 2

    def hidden(x_ref, pe_ref, w1_ref):
        acc_a = jnp.zeros((N_CMP_PAD, 2 * CMP_HIDDEN), F32)
        acc_b = jnp.zeros((N_CMP_PAD, 2 * CMP_HIDDEN), F32)
        for l in range(half):
            rows = x_ref[pl.ds(l, N_CMP_PAD, stride=CMP_STRIDE), :]
            acc_a += _dot((rows + pe_ref[l:l + 1, :]).astype(BF16), w1_ref[l])
            acc_b += _dot((rows + pe_ref[half + l:half + l + 1, :]).astype(BF16), w1_ref[half + l])
        pre = acc_a + pltpu.roll(acc_b, N_CMP_PAD - 1, axis=0)
        return jax.nn.gelu(pre, approximate=True).astype(BF16)

    kc = _dot(hidden(kc_ref, pek_ref, w1k_ref), w2k_ref[...])
    vc = _dot(hidden(vc_ref, pev_ref, w1v_ref), w2v_ref[...])
    lane = lax.broadcasted_iota(I32, kc.shape, 1)
    row = lax.broadcasted_iota(I32, kc.shape, 0)
    kc = jnp.where((lane & (LANES - 1)) < HEAD_DIM, kc, _key_extras(2 * CMP_STRIDE * row + CMP_BLOCK - 1, lane, False))
    kca_ref[0] = jnp.where(row < N_CMP_PAD - 1, kc, 0.0).astype(BF16)
    vc2_ref[0] = jnp.where(row[:, :LANES] < N_CMP_PAD - 1, vc, 0.0).astype(BF16)


def _compress(kc, vc, pek, pev, w1k, w1v, w2k, w2v):
    n = kc.shape[0]
    b = n // SEQ
    full = lambda a: pl.BlockSpec(a.shape, lambda i: (0,) * a.ndim)
    return pl.pallas_call(
        _compress_kernel,
        grid=(b,),
        in_specs=[
            pl.BlockSpec((SEQ, LANES), lambda i: (i, 0)),
            pl.BlockSpec((SEQ, LANES), lambda i: (i, 0)),
            full(pek), full(pev), full(w1k), full(w1v), full(w2k), full(w2v),
        ],
        out_specs=[pl.BlockSpec((1, N_CMP_PAD, 2 * LANES), lambda i: (i, 0, 0)),
                   pl.BlockSpec((1, N_CMP_PAD, LANES), lambda i: (i, 0, 0))],
        out_shape=[jax.ShapeDtypeStruct((b, N_CMP_PAD, 2 * LANES), BF16),
                   jax.ShapeDtypeStruct((b, N_CMP_PAD, LANES), BF16)],
        compiler_params=_params(("parallel",), 32 << 20),
        name="compress",
    )(kc, vc, pek, pev, w1k, w1v, w2k, w2v)


def _attend_pipelined(tasks):
    n = len(tasks)
    scores, probs, outs = [None] * n, [None] * n, [None] * n
    for i in range(n + 2):
        if i < n:
            q, k, _, _ = tasks[i]
            scores[i] = _dot_nt(q, k)
        if 1 <= i <= n:
            _, _, _, chunk_masks = tasks[i - 1]
            cols = []
            for c, allowed in enumerate(chunk_masks):
                s = scores[i - 1][:, c * KV_CHUNK:(c + 1) * KV_CHUNK]
                cols.append(s if allowed is None else jnp.where(allowed, s, NEG_INF))
            s = cols[0] if len(cols) == 1 else jnp.concatenate(cols, axis=1)
            p = jnp.exp(s - jnp.max(s, axis=-1, keepdims=True))
            probs[i - 1] = (p.astype(BF16), 1.0 / jnp.sum(p, axis=-1, keepdims=True))
            scores[i - 1] = None
        if 2 <= i:
            p, inv = probs[i - 2]
            outs[i - 2] = _dot(p, tasks[i - 2][2]) * inv
            probs[i - 2] = None
    return outs


def _head_queries(q_ref, g, slope_scale, sel_lanes):
    rows = q_ref.shape[0]
    lane = lax.broadcasted_iota(I32, (rows, LANES), 1)
    qf = q_ref[...].astype(F32)
    parts = []
    for r in range(NSA_REP):
        slope = slope_scale * jnp.where(g == 0, 2.0 ** -(r + 1), 2.0 ** -(r + 1 + NSA_REP)).astype(F32)
        tile = qf[:, (r // 2) * LANES:(r // 2 + 1) * LANES]
        if r % 2:
            tile = pltpu.roll(tile, HEAD_DIM, axis=1)
        ext = jnp.where(lane == X_HI, POS_SPLIT * slope, jnp.where(lane == X_LO, slope, sel_lanes))
        parts.append(jnp.where(lane < HEAD_DIM, tile, ext).astype(BF16))
    return parts


def _head_layout(heads, g):
    low = lax.broadcasted_iota(I32, heads[0].shape, 1) < HEAD_DIM
    tiles = []
    for pair in range(NSA_REP // 2):
        even, odd = heads[2 * pair], heads[2 * pair + 1]
        left = jnp.where(g == 0, even, pltpu.roll(even, HEAD_DIM, axis=1))
        right = jnp.where(g == 0, pltpu.roll(odd, HEAD_DIM, axis=1), odd)
        tiles.append(jnp.where(low, left, right))
    return tiles


def _nsa_select_kernel(q_ref, misc_ref, kca_ref, vc2_ref, ocmp_ref, drop_ref):
    g = pl.program_id(1)
    rows = SEL_ROWS
    t0 = pl.program_id(2) * rows
    lane = lax.broadcasted_iota(I32, (rows, LANES), 1)
    tq = t0 + lax.broadcasted_iota(I32, (rows, LANES), 0)

    q4 = jnp.concatenate(_head_queries(q_ref, g, 0.5, 0.0), axis=0)
    s_c = _dot_nt(q4, kca_ref[0])
    valid_c = lane * CMP_STRIDE + (CMP_BLOCK - 1) <= tq
    p_sum = jnp.zeros((rows, LANES), F32)
    p_parts = []
    for r in range(NSA_REP):
        s = jnp.where(valid_c, s_c[r * rows:(r + 1) * rows], NEG_INF)
        e = jnp.exp(s - jnp.max(s, axis=-1, keepdims=True))
        p = jnp.where(valid_c, e / jnp.sum(e, axis=-1, keepdims=True), 0.0)
        p_sum += p
        p_parts.append(p.astype(BF16))
    o_cmp = _dot(jnp.concatenate(p_parts, axis=0), vc2_ref[0])
    gates = jax.nn.sigmoid(misc_ref[...])
    tiles = _head_layout([gates[:, 3 * r:3 * r + 1] * o_cmp[r * rows:(r + 1) * rows] for r in range(NSA_REP)], g)
    for i, tile in enumerate(tiles):
        ocmp_ref[:, i * LANES:(i + 1) * LANES] = tile.astype(BF16)

    ov_j = lax.broadcasted_iota(I32, (LANES, LANES), 0)
    ov_i = lax.broadcasted_iota(I32, (LANES, LANES), 1)
    ov_t = ((ov_i < 4 * ov_j + 4) & (ov_i > 4 * ov_j - 2) & (ov_j < N_SEL)).astype(F32)
    imp_t = _dot_nt(ov_t, p_sum, precision=lax.Precision.HIGHEST)[:N_SEL]
    j_blk = lax.broadcasted_iota(I32, (N_SEL, rows), 0)
    cur = (t0 + lax.broadcasted_iota(I32, (N_SEL, rows), 1)) >> SEL_SHIFT
    forced = (j_blk == 0) | (j_blk == cur) | (j_blk == cur - 1)
    imp_t = jnp.where(forced, FORCE, imp_t)
    imp_t = jnp.where(j_blk <= cur, imp_t, -FORCE)
    rank = jnp.zeros((N_SEL, rows), I32)
    for jp in range(N_SEL):
        row = imp_t[jp:jp + 1, :]
        beats = (row > imp_t) | ((row == imp_t) & (j_blk > jp))
        rank += beats.astype(I32)
    drop_t = jnp.where(rank < SEL_TOPK, 0.0, NEG_INF)
    drop_t = jnp.concatenate([drop_t, jnp.zeros((LANES - N_SEL, rows), F32)], axis=0)
    drop_q = jnp.concatenate([drop_t[:, i * LANES:(i + 1) * LANES].T for i in range(rows // LANES)], axis=0)
    drop_ref[...] = pltpu.roll(drop_q, X_SEL, axis=1).astype(BF16)


def _nsa_kernel(q_ref, misc_ref, ocmp_ref, drop_ref, ksa_ref, vs_ref, kwa_ref, vw_ref, o_ref):
    for qb in range(N_QB):
        @pl.when(pl.program_id(2) == qb)
        def _(qb=qb):
            _nsa_body(qb, q_ref, misc_ref, ocmp_ref, drop_ref, ksa_ref, vs_ref, kwa_ref, vw_ref, o_ref)


def _nsa_body(qb, q_ref, misc_ref, ocmp_ref, drop_ref, ksa_ref, vs_ref, kwa_ref, vw_ref, o_ref):
    g = pl.program_id(1)
    t0 = qb * Q_BLOCK

    def dist_to(c):
        return (t0 + lax.broadcasted_iota(I32, (Q_BLOCK, KV_CHUNK), 0)
                - (c * KV_CHUNK + lax.broadcasted_iota(I32, (Q_BLOCK, KV_CHUNK), 1)))

    c_diag = qb
    dist_diag = dist_to(c_diag)
    n_slc = (c_diag + 1) * KV_CHUNK

    slc_masks = [None] * c_diag + [dist_diag >= 0]
    q_sel = _head_queries(q_ref, g, 1.0, drop_ref[...].astype(F32))

    c_first = max(c_diag - WINDOW // KV_CHUNK, 0)
    win_masks = [None] * (c_diag - c_first + 1)
    win_masks[-1] = dist_diag >= 0
    if c_diag - c_first == WINDOW // KV_CHUNK:
        win_masks[0] = dist_to(c_first) < WINDOW
    win_rows = slice(c_first * KV_CHUNK, n_slc)
    q_win = _head_queries(q_ref, g, 1.0, 0.0)

    both = lambda masks: [None if m is None else jnp.concatenate([m, m], axis=0) for m in masks]
    tasks = []
    for r in range(0, NSA_REP, 2):
        tasks.append((jnp.concatenate(q_sel[r:r + 2], axis=0), ksa_ref[:n_slc, :], vs_ref[:n_slc, :],
                      both(slc_masks)))
        tasks.append((jnp.concatenate(q_win[r:r + 2], axis=0), kwa_ref[win_rows, :], vw_ref[win_rows, :],
                      both(win_masks)))
    outs = _attend_pipelined(tasks)
    split = lambda o: [o[:Q_BLOCK], o[Q_BLOCK:]]
    o_slc = split(outs[0]) + split(outs[2])
    o_win = split(outs[1]) + split(outs[3])

    gates = jax.nn.sigmoid(misc_ref[...])
    heads = [gates[:, 3 * r + 1:3 * r + 2] * o_slc[r] + gates[:, 3 * r + 2:3 * r + 3] * o_win[r]
             for r in range(NSA_REP)]
    for i, tile in enumerate(_head_layout(heads, g)):
        cols = slice(i * LANES, (i + 1) * LANES)
        o_ref[:, cols] = (tile + ocmp_ref[:, cols].astype(F32)).astype(BF16)


def _nsa(q, misc, kca, vc2, ksa, vs, kwa, vw):
    n = q.shape[0]
    b = n // SEQ
    gw = NSA_REP * HEAD_DIM
    assert WINDOW % KV_CHUNK == 0 and KV_CHUNK == Q_BLOCK and SEQ % SEL_ROWS == 0
    n_sel = SEQ // SEL_ROWS
    sel_rows = lambda width: pl.BlockSpec((SEL_ROWS, width), lambda i, g, j: (i * n_sel + j, g))
    ocmp, drop = pl.pallas_call(
        _nsa_select_kernel,
        grid=(b, NSA_GROUPS, n_sel),
        in_specs=[sel_rows(gw), sel_rows(LANES),
                  pl.BlockSpec((1, N_CMP_PAD, LANES), lambda i, g, j: (i, 0, g)),
                  pl.BlockSpec((1, N_CMP_PAD, LANES), lambda i, g, j: (i, 0, 0))],
        out_specs=[sel_rows(gw), sel_rows(LANES)],
        out_shape=[jax.ShapeDtypeStruct((n, NSA_GROUPS * gw), BF16),
                   jax.ShapeDtypeStruct((n, NSA_GROUPS * LANES), BF16)],
        compiler_params=_params(("parallel", "parallel", "parallel"), 32 << 20),
        name="nsa_select",
    )(q, misc, kca, vc2)

    q_rows = lambda width: pl.BlockSpec((Q_BLOCK, width), lambda i, g, c: (i * N_QB + c, g))
    return pl.pallas_call(
        _nsa_kernel,
        grid=(b, NSA_GROUPS, N_QB),
        in_specs=[q_rows(gw), q_rows(LANES), q_rows(gw), q_rows(LANES),
                  pl.BlockSpec((SEQ, LANES), lambda i, g, c: (i, g)),
                  pl.BlockSpec((SEQ, LANES), lambda i, g, c: (i, 0)),
                  pl.BlockSpec((SEQ, LANES), lambda i, g, c: (i, g)),
                  pl.BlockSpec((SEQ, LANES), lambda i, g, c: (i, 0))],
        out_specs=q_rows(gw),
        out_shape=jax.ShapeDtypeStruct((n, NSA_GROUPS * gw), BF16),
        compiler_params=_params(("parallel", "parallel", "arbitrary"), 48 << 20),
        name="nsa",
    )(q, misc, ocmp, drop, ksa, vs, kwa, vw)


def _gla_kernel(q_ref, k_ref, v_ref, r_ref, misc_ref, wa_ref, ba_ref, gn_ref, o_ref):
    c, blk = GLA_CHUNK, GLA_BLOCK
    n_c = blk // c
    lane = lax.broadcasted_iota(I32, (blk, LANES), 1)
    row = lax.broadcasted_iota(I32, (blk, LANES), 0)
    in_chunk = row & (c - 1)
    chunk_of_row = row >> GLA_CHUNK_SHIFT
    r2 = lax.broadcasted_iota(I32, (2 * blk, blk), 0) & (blk - 1)
    c2 = lax.broadcasted_iota(I32, (2 * blk, blk), 1)
    intra = (r2 >= c2) & ((r2 >> GLA_CHUNK_SHIFT) == (c2 >> GLA_CHUNK_SHIFT))
    lane_s = lax.broadcasted_iota(I32, (GLA_DV, LANES), 1)
    n_blk = SEQ // blk

    def prep(i_blk):
        rows = slice(i_blk * blk, (i_blk + 1) * blk)
        la = _dot(misc_ref[rows, :], wa_ref[...], precision=lax.Precision.HIGHEST) + ba_ref[...]
        b = (jnp.minimum(la, 0.0) - jnp.log(1.0 + jnp.exp(-jnp.abs(la)))) * (1.0 / GLA_TAU)
        shift = 1
        while shift < c:
            b = b + jnp.where(in_chunk >= shift, pltpu.roll(b, shift, axis=0), 0.0)
            shift *= 2
        b3 = b.reshape(n_c, c, LANES)
        b_last = b3[:, c - 1:c, :]
        k = k_ref[rows, :]
        q_in = q_ref[rows, :] * (GLA_DK ** -0.5) * jnp.exp(b)
        k_in = (k * jnp.exp(-b)).astype(BF16)
        k_st = (k.reshape(n_c, c, LANES) * jnp.exp(b_last - b3)).reshape(blk, LANES)
        decay = jnp.exp(b_last)
        v = v_ref[rows, :]

        q2 = jnp.concatenate([jnp.where(lane < GLA_DK, q_in, 0.0), jnp.where(lane >= GLA_DK, q_in, 0.0)],
                             axis=0).astype(BF16)
        k_cols = jnp.concatenate([jnp.where(chunk_of_row == i_c, k_st, 0.0) for i_c in range(n_c)], axis=1)
        return q2, k_in, k_cols.astype(BF16), v, decay

    def chunk_parallel(q2, k_in, k_cols, v, decay):
        a = jnp.where(intra, _dot_nt(q2, k_in), 0.0).astype(BF16)
        o_intra = _dot(a, v)
        inc = _dot_tn(v, k_cols)
        return q2, o_intra, inc, decay

    per_blk = []
    ready = prep(0)
    for i_blk in range(n_blk):
        upcoming = prep(i_blk + 1) if i_blk + 1 < n_blk else None
        per_blk.append(chunk_parallel(*ready))
        ready = upcoming

    st = jnp.zeros((GLA_DV, LANES), F32)
    state_before = []
    for _, _, inc, decay in per_blk:
        for i_c in range(n_c):
            state_before.append(st.astype(BF16))
            cols = slice(i_c * LANES, (i_c + 1) * LANES)
            st = st * decay[i_c] + jnp.where(lane_s < GLA_DK, inc[:GLA_DV, cols], inc[GLA_DV:, cols])

    for i_blk, (q2, o_intra, _, _) in enumerate(per_blk):
        rows = slice(i_blk * blk, (i_blk + 1) * blk)
        o_inter = []
        for i_c in range(n_c):
            q_c = jnp.concatenate([q2[i_c * c:(i_c + 1) * c], q2[blk + i_c * c:blk + (i_c + 1) * c]], axis=0)
            o_inter.append(_dot_nt(q_c, state_before[i_blk * n_c + i_c]))
        for h in range(2):
            cols = slice(h * GLA_DV, (h + 1) * GLA_DV)
            o = o_intra[h * blk:(h + 1) * blk, cols] + jnp.concatenate(
                [o_inter[i_c][h * c:(h + 1) * c] for i_c in range(n_c)], axis=0)
            gate = r_ref[rows, cols]
            o_ref[rows, cols] = (_rms(o, gn_ref[...]) * (gate * jax.nn.sigmoid(gate))).astype(BF16)


def _gla(gq, gk, gv, gr, misc, wa, ba, gn):
    n = gq.shape[0]
    b = n // SEQ
    return pl.pallas_call(
        _gla_kernel,
        grid=(b, GLA_HEADS // 2),
        in_specs=[
            pl.BlockSpec((SEQ, LANES), lambda i, p: (i, p)),
            pl.BlockSpec((SEQ, LANES), lambda i, p: (i, p)),
            pl.BlockSpec((SEQ, 2 * GLA_DV), lambda i, p: (i, p)),
            pl.BlockSpec((SEQ, 2 * GLA_DV), lambda i, p: (i, p)),
            pl.BlockSpec((SEQ, LANES), lambda i, p: (i, 0)),
            pl.BlockSpec((LANES, LANES), lambda i, p: (0, p)),
            pl.BlockSpec((1, LANES), lambda i, p: (0, p)),
            pl.BlockSpec((1, GLA_DV), lambda i, p: (0, 0)),
        ],
        out_specs=pl.BlockSpec((SEQ, 2 * GLA_DV), lambda i, p: (i, p)),
        out_shape=jax.ShapeDtypeStruct((n, GLA_HEADS * GLA_DV), BF16),
        compiler_params=_params(("parallel", "parallel"), 40 << 20),
        name="gla",
    )(gq, gk, gv, gr, misc, wa, ba, gn)


def _permute_w_in(w):
    w = w.astype(BF16)
    d = w.shape[0]
    col = lambda lo, hi: w[:, lo:hi]
    zeros = lambda k: jnp.zeros((d, k), w.dtype)
    hd = HEAD_DIM

    def keys_padded(k0):
        return [col(k0, k0 + hd), zeros(LANES - hd), col(k0 + hd, k0 + 2 * hd), zeros(LANES - hd)]

    n_gate = 3 * NSA_REP
    misc = [col(1280, 1280 + n_gate), col(2840, 2856), zeros(LANES - n_gate - GLA_LOWRANK),
            col(1280 + n_gate, 1304), zeros(LANES - n_gate)]
    parts = ([col(0, 512), col(512, 640), col(640, 768)] + keys_padded(768) + keys_padded(1024)
             + [col(896, 1024), col(1152, 1280)] + misc
             + [col(1304, 1560), col(1560, 1816), col(1816, 2328), col(2328, 2840)])
    out = jnp.concatenate(parts, axis=1)
    assert out.shape[1] == D_IN_PAD
    return out


def _compress_weights(pe, w1, w2, group_stride):
    hd, hid = HEAD_DIM, CMP_HIDDEN
    pe2 = jnp.concatenate([pe, pe], axis=1)
    w1 = w1.reshape(CMP_BLOCK, hd, hid)
    z1 = jnp.zeros_like(w1)
    w1e = jnp.concatenate([jnp.concatenate([w1, z1], axis=2), jnp.concatenate([z1, w1], axis=2)], axis=1)
    w2e = jnp.zeros((NSA_GROUPS * hid, NSA_GROUPS * group_stride), w2.dtype)
    for g in range(NSA_GROUPS):
        w2e = w2e.at[g * hid:(g + 1) * hid, g * group_stride:g * group_stride + hd].set(w2)
    return pe2, w1e.astype(BF16), w2e.astype(BF16)


def kernel(x, ffn1_norm, ffn1_w_gate, ffn1_w_up, ffn1_w_down, mix_norm, w_in, nsa_pe_k, nsa_w1_k, nsa_w2_k,
           nsa_pe_v, nsa_w1_v, nsa_w2_v, gla_w_a2, gla_b_a, gla_norm, w_out, ffn2_norm, ffn2_w_gate,
           ffn2_w_up, ffn2_w_down, final_norm):
    bsz, seq, d = x.shape
    assert (seq, d) == (SEQ, D_MODEL) and ffn1_norm.shape[0] == 1
    n = bsz * seq
    xf = x.reshape(n, d)
    row = lambda v: v.reshape(1, -1).astype(F32)
    bf = lambda w: w.astype(BF16)
    ones = jnp.ones((1, d), F32)

    x1 = _ffn(xf, (), row(ffn1_norm[0]), bf(ffn1_w_gate[0]), bf(ffn1_w_up[0]), bf(ffn1_w_down[0]), ones,
              final_norm=False)

    q, kc, vc, ksa, kwa, vs, vw, misc, gq, gk, gv, gr = _inproj(x1, row(mix_norm[0]), _permute_w_in(w_in[0]))

    pek, w1k, w2k = _compress_weights(nsa_pe_k[0], nsa_w1_k[0], nsa_w2_k[0], LANES)
    pev, w1v, w2v = _compress_weights(nsa_pe_v[0], nsa_w1_v[0], nsa_w2_v[0], HEAD_DIM)
    kca, vc2 = _compress(kc, vc, pek, pev, w1k, w1v, w2k, w2v)

    o_nsa = _nsa(q, misc, kca, vc2, ksa, vs, kwa, vw)

    wa = jnp.zeros((LANES, GLA_HEADS * GLA_DK), F32).at[MISC_GA_OFF:MISC_GA_OFF + GLA_LOWRANK].set(gla_w_a2[0])
    o_gla = _gla(gq, gk, gv, gr, misc, wa, row(gla_b_a[0]), row(gla_norm[0]))

    d_nsa = NSA_HEADS * HEAD_DIM
    w_o = bf(w_out[0])
    out = _ffn(x1, (o_nsa, o_gla, w_o[:d_nsa], w_o[d_nsa:]), row(ffn2_norm[0]), bf(ffn2_w_gate[0]),
               bf(ffn2_w_up[0]), bf(ffn2_w_down[0]), row(final_norm), final_norm=True)
    return out.reshape(bsz, seq, d)
```

```python
import functools

import jax
import jax.numpy as jnp
from jax import lax
from jax.experimental import pallas as pl
from jax.experimental.pallas import tpu as pltpu

F32 = jnp.float32
BF16 = jnp.bfloat16
I32 = jnp.int32

D_MODEL = 1024
SEQ = 2048
D_FF = 2816
EPS = 1e-6
NEG_INF = -1e30
FORCE = 1e9

NSA_HEADS = 8
NSA_GROUPS = 2
NSA_REP = NSA_HEADS // NSA_GROUPS
HEAD_DIM = 64
CMP_BLOCK = 32
CMP_STRIDE = 16
CMP_HIDDEN = 128
N_CMP_PAD = SEQ // CMP_STRIDE
SEL_BLOCK = 64
SEL_SHIFT = 6
assert 1 << SEL_SHIFT == SEL_BLOCK
N_SEL = SEQ // SEL_BLOCK
SEL_TOPK = 8
WINDOW = 512
Q_BLOCK = 256
N_QB = SEQ // Q_BLOCK
KV_CHUNK = Q_BLOCK
SEL_ROWS = 512

GLA_HEADS = 4
GLA_DK = 64
GLA_DV = 128
GLA_CHUNK = 64
GLA_CHUNK_SHIFT = 6
assert 1 << GLA_CHUNK_SHIFT == GLA_CHUNK
GLA_BLOCK = 256
GLA_LOWRANK = 16
GLA_TAU = 16.0

LANES = 128
V7X_VMEM_BYTES = 64 * 1024 * 1024

C_Q = (0, 512)
C_KC = (512, 640)
C_VC = (640, 768)
C_KSA = (768, 1024)
C_KWA = (1024, 1280)
C_VS = (1280, 1408)
C_VW = (1408, 1536)
C_MISC = (1536, 1792)
C_GQ = (1792, 2048)
C_GK = (2048, 2304)
C_GV = (2304, 2816)
C_GR = (2816, 3328)
D_IN_PAD = 3328
MISC_GA_OFF = 12

X_SEL = HEAD_DIM
X_HI = HEAD_DIM + N_SEL
X_LO = X_HI + 1
POS_SPLIT = 64
POS_SHIFT = 6
assert 1 << POS_SHIFT == POS_SPLIT and X_LO < LANES


def _params(sem, vmem_bytes):
    return pltpu.CompilerParams(dimension_semantics=sem,
                                vmem_limit_bytes=min(int(vmem_bytes), V7X_VMEM_BYTES - (8 << 20)))


def _rms(x, g):
    return x * lax.rsqrt(jnp.mean(x * x, axis=-1, keepdims=True) + EPS) * g


def _dot(a, b, **kw):
    return jnp.dot(a, b, preferred_element_type=F32, **kw)


def _dot_nt(a, b, **kw):
    return lax.dot_general(a, b, (((1,), (1,)), ((), ())), preferred_element_type=F32, **kw)


def _dot_tn(a, b, **kw):
    return lax.dot_general(a, b, (((0,), (0,)), ((), ())), preferred_element_type=F32, **kw)


def _ffn_kernel(*refs, tf, final_norm, mixer_out):
    if mixer_out:
        x_ref, a_ref, b_ref, wa_ref, wb_ref, g_ref, wg_ref, wu_ref, wd_ref, fg_ref, o_ref, act_ref = refs
        x = x_ref[...] + _dot(a_ref[...], wa_ref[...]) + _dot(b_ref[...], wb_ref[...])
    else:
        x_ref, g_ref, wg_ref, wu_ref, wd_ref, fg_ref, o_ref, act_ref = refs
        x = x_ref[...]
    h = _rms(x, g_ref[...]).astype(BF16)
    for c in range(D_FF // tf):
        cols = slice(c * tf, (c + 1) * tf)
        gate = _dot(h, wg_ref[:, cols])
        up = _dot(h, wu_ref[:, cols])
        act_ref[:, cols] = ((gate * jax.nn.sigmoid(gate)) * up).astype(BF16)
    y = x + 0.5 * _dot(act_ref[...], wd_ref[...])
    if final_norm:
        y = _rms(y, fg_ref[...])
    o_ref[...] = y


def _ffn(x, mixer, gain, wg, wu, wd, final_gain, *, final_norm, tm=512, tf=256):
    n, d = x.shape
    assert D_FF % tf == 0
    resident = lambda shape: pl.BlockSpec(shape, lambda i: (0, 0), pipeline_mode=pl.Buffered(1))
    rows = lambda width: pl.BlockSpec((tm, width), lambda i: (i, 0))
    vmem = 2 * 2 * tm * d * 4 + 3 * d * D_FF * 2 + tm * D_FF * 2 + tm * d * 2 + 4 * tm * tf * 4 + (8 << 20)
    mix_specs = []
    if mixer:
        a, b, w_a, w_b = mixer
        mix_specs = [rows(a.shape[1]), rows(b.shape[1]), resident(w_a.shape), resident(w_b.shape)]
        vmem += 2 * tm * (a.shape[1] + b.shape[1]) * 2 + (w_a.size + w_b.size) * 2
    return pl.pallas_call(
        functools.partial(_ffn_kernel, tf=tf, final_norm=final_norm, mixer_out=bool(mixer)),
        grid=(n // tm,),
        in_specs=[rows(d)] + mix_specs + [
            resident((1, d)),
            resident((d, D_FF)),
            resident((d, D_FF)),
            resident((D_FF, d)),
            resident((1, d)),
        ],
        out_specs=rows(d),
        out_shape=jax.ShapeDtypeStruct((n, d), F32),
        scratch_shapes=[pltpu.VMEM((tm, D_FF), BF16)],
        compiler_params=_params(("parallel",), vmem),
        name="ffn",
    )(x, *mixer, gain, wg, wu, wd, final_gain)


def _key_extras(pos, lane, with_block):
    lg = lane & (LANES - 1)
    ext = jnp.where(lg == X_HI, (pos >> POS_SHIFT).astype(F32),
                    jnp.where(lg == X_LO, (pos & (POS_SPLIT - 1)).astype(F32), 0.0))
    if with_block:
        ext = jnp.where((lg >= X_SEL) & (lg < X_HI) & ((pos >> SEL_SHIFT) == lg - X_SEL), 1.0, ext)
    return ext


def _inproj_kernel(x_ref, g_ref, w_ref, q_ref, kc_ref, vc_ref, ksa_ref, kwa_ref, vs_ref, vw_ref, misc_ref,
                   gq_ref, gk_ref, gv_ref, gr_ref, *, tm):
    h = _rms(x_ref[...], g_ref[...]).astype(BF16)

    def proj(c):
        return _dot(h, w_ref[:, c[0]:c[1]])

    shape = (tm, 2 * LANES)
    lane = lax.broadcasted_iota(I32, shape, 1)
    pos = (pl.program_id(0) * tm + lax.broadcasted_iota(I32, shape, 0)) & (SEQ - 1)
    is_key = (lane & (LANES - 1)) < HEAD_DIM

    q_ref[...] = (proj(C_Q) * (HEAD_DIM ** -0.5)).astype(BF16)
    kc_ref[...] = proj(C_KC)
    vc_ref[...] = proj(C_VC)
    ksa_ref[...] = jnp.where(is_key, proj(C_KSA), _key_extras(pos, lane, True)).astype(BF16)
    kwa_ref[...] = jnp.where(is_key, proj(C_KWA), _key_extras(pos, lane, False)).astype(BF16)
    vs_ref[...] = proj(C_VS).astype(BF16)
    vw_ref[...] = proj(C_VW).astype(BF16)
    misc_ref[...] = proj(C_MISC)
    gq_ref[...] = proj(C_GQ)
    gk_ref[...] = proj(C_GK)
    gv_ref[...] = proj(C_GV).astype(BF16)
    gr_ref[...] = proj(C_GR)


def _inproj(x, gain, w, tm=512):
    n, d = x.shape
    assert SEQ % tm == 0 and SEQ & (SEQ - 1) == 0
    outs = [(C_Q, BF16), (C_KC, F32), (C_VC, F32), (C_KSA, BF16), (C_KWA, BF16), (C_VS, BF16), (C_VW, BF16),
            (C_MISC, F32), (C_GQ, F32), (C_GK, F32), (C_GV, BF16), (C_GR, F32)]
    vmem = 2 * tm * d * 4 + 2 * d * D_IN_PAD * 2 + 2 * tm * D_IN_PAD * 4 + (8 << 20)
    return pl.pallas_call(
        functools.partial(_inproj_kernel, tm=tm),
        grid=(n // tm,),
        in_specs=[
            pl.BlockSpec((tm, d), lambda i: (i, 0)),
            pl.BlockSpec((1, d), lambda i: (0, 0)),
            pl.BlockSpec((d, D_IN_PAD), lambda i: (0, 0)),
        ],
        out_specs=[pl.BlockSpec((tm, c[1] - c[0]), lambda i: (i, 0)) for c, _ in outs],
        out_shape=[jax.ShapeDtypeStruct((n, c[1] - c[0]), dt) for c, dt in outs],
        compiler_params=_params(("parallel",), vmem),
        name="inproj",
    )(x, gain, w)


def _compress_kernel(kc_ref, vc_ref, pek_ref, pev_ref, w1k_ref, w1v_ref, w2k_ref, w2v_ref, kca_ref, vc2_ref):
    half = CMP_BLOCK // 2

    def hidden(x_ref, pe_ref, w1_ref):
        acc_a = jnp.zeros((N_CMP_PAD, 2 * CMP_HIDDEN), F32)
        acc_b = jnp.zeros((N_CMP_PAD, 2 * CMP_HIDDEN), F32)
        for l in range(half):
            rows = x_ref[pl.ds(l, N_CMP_PAD, stride=CMP_STRIDE), :]
            acc_a += _dot((rows + pe_ref[l:l + 1, :]).astype(BF16), w1_ref[l])
            acc_b += _dot((rows + pe_ref[half + l:half + l + 1, :]).astype(BF16), w1_ref[half + l])
        pre = acc_a + pltpu.roll(acc_b, N_CMP_PAD - 1, axis=0)
        return jax.nn.gelu(pre, approximate=True).astype(BF16)

    kc = _dot(hidden(kc_ref, pek_ref, w1k_ref), w2k_ref[...])
    vc = _dot(hidden(vc_ref, pev_ref, w1v_ref), w2v_ref[...])
    lane = lax.broadcasted_iota(I32, kc.shape, 1)
    row = lax.broadcasted_iota(I32, kc.shape, 0)
    kc = jnp.where((lane & (LANES - 1)) < HEAD_DIM, kc, _key_extras(2 * CMP_STRIDE * row + CMP_BLOCK - 1, lane, False))
    kca_ref[0] = jnp.where(row < N_CMP_PAD - 1, kc, 0.0).astype(BF16)
    vc2_ref[0] = jnp.where(row[:, :LANES] < N_CMP_PAD - 1, vc, 0.0).astype(BF16)


def _compress(kc, vc, pek, pev, w1k, w1v, w2k, w2v):
    n = kc.shape[0]
    b = n // SEQ
    full = lambda a: pl.BlockSpec(a.shape, lambda i: (0,) * a.ndim)
    return pl.pallas_call(
        _compress_kernel,
        grid=(b,),
        in_specs=[
            pl.BlockSpec((SEQ, LANES), lambda i: (i, 0)),
            pl.BlockSpec((SEQ, LANES), lambda i: (i, 0)),
            full(pek), full(pev), full(w1k), full(w1v), full(w2k), full(w2v),
        ],
        out_specs=[pl.BlockSpec((1, N_CMP_PAD, 2 * LANES), lambda i: (i, 0, 0)),
                   pl.BlockSpec((1, N_CMP_PAD, LANES), lambda i: (i, 0, 0))],
        out_shape=[jax.ShapeDtypeStruct((b, N_CMP_PAD, 2 * LANES), BF16),
                   jax.ShapeDtypeStruct((b, N_CMP_PAD, LANES), BF16)],
        compiler_params=_params(("parallel",), 32 << 20),
        name="compress",
    )(kc, vc, pek, pev, w1k, w1v, w2k, w2v)


def _attend_pipelined(tasks):
    n = len(tasks)
    scores, probs, outs = [None] * n, [None] * n, [None] * n
    for i in range(n + 2):
        if i < n:
            q, k, _, _ = tasks[i]
            scores[i] = _dot_nt(q, k)
        if 1 <= i <= n:
            _, _, _, chunk_masks = tasks[i - 1]
            cols = []
            for c, allowed in enumerate(chunk_masks):
                s = scores[i - 1][:, c * KV_CHUNK:(c + 1) * KV_CHUNK]
                cols.append(s if allowed is None else jnp.where(allowed, s, NEG_INF))
            s = cols[0] if len(cols) == 1 else jnp.concatenate(cols, axis=1)
            p = jnp.exp(s - jnp.max(s, axis=-1, keepdims=True))
            probs[i - 1] = (p.astype(BF16), 1.0 / jnp.sum(p, axis=-1, keepdims=True))
            scores[i - 1] = None
        if 2 <= i:
            p, inv = probs[i - 2]
            outs[i - 2] = _dot(p, tasks[i - 2][2]) * inv
            probs[i - 2] = None
    return outs


def _head_queries(q_ref, g, slope_scale, sel_lanes):
    rows = q_ref.shape[0]
    lane = lax.broadcasted_iota(I32, (rows, LANES), 1)
    qf = q_ref[...].astype(F32)
    parts = []
    for r in range(NSA_REP):
        slope = slope_scale * jnp.where(g == 0, 2.0 ** -(r + 1), 2.0 ** -(r + 1 + NSA_REP)).astype(F32)
        tile = qf[:, (r // 2) * LANES:(r // 2 + 1) * LANES]
        if r % 2:
            tile = pltpu.roll(tile, HEAD_DIM, axis=1)
        ext = jnp.where(lane == X_HI, POS_SPLIT * slope, jnp.where(lane == X_LO, slope, sel_lanes))
        parts.append(jnp.where(lane < HEAD_DIM, tile, ext).astype(BF16))
    return parts


def _head_layout(heads, g):
    low = lax.broadcasted_iota(I32, heads[0].shape, 1) < HEAD_DIM
    tiles = []
    for pair in range(NSA_REP // 2):
        even, odd = heads[2 * pair], heads[2 * pair + 1]
        left = jnp.where(g == 0, even, pltpu.roll(even, HEAD_DIM, axis=1))
        right = jnp.where(g == 0, pltpu.roll(odd, HEAD_DIM, axis=1), odd)
        tiles.append(jnp.where(low, left, right))
    return tiles


def _nsa_select_kernel(q_ref, misc_ref, kca_ref, vc2_ref, ocmp_ref, drop_ref):
    g = pl.program_id(1)
    rows = SEL_ROWS
    t0 = pl.program_id(2) * rows
    lane = lax.broadcasted_iota(I32, (rows, LANES), 1)
    tq = t0 + lax.broadcasted_iota(I32, (rows, LANES), 0)

    q4 = jnp.concatenate(_head_queries(q_ref, g, 0.5, 0.0), axis=0)
    s_c = _dot_nt(q4, kca_ref[0])
    valid_c = lane * CMP_STRIDE + (CMP_BLOCK - 1) <= tq
    p_sum = jnp.zeros((rows, LANES), F32)
    p_parts = []
    for r in range(NSA_REP):
        s = jnp.where(valid_c, s_c[r * rows:(r + 1) * rows], NEG_INF)
        e = jnp.exp(s - jnp.max(s, axis=-1, keepdims=True))
        p = jnp.where(valid_c, e / jnp.sum(e, axis=-1, keepdims=True), 0.0)
        p_sum += p
        p_parts.append(p.astype(BF16))
    o_cmp = _dot(jnp.concatenate(p_parts, axis=0), vc2_ref[0])
    gates = jax.nn.sigmoid(misc_ref[...])
    tiles = _head_layout([gates[:, 3 * r:3 * r + 1] * o_cmp[r * rows:(r + 1) * rows] for r in range(NSA_REP)], g)
    for i, tile in enumerate(tiles):
        ocmp_ref[:, i * LANES:(i + 1) * LANES] = tile.astype(BF16)

    ov_j = lax.broadcasted_iota(I32, (LANES, LANES), 0)
    ov_i = lax.broadcasted_iota(I32, (LANES, LANES), 1)
    ov_t = ((ov_i < 4 * ov_j + 4) & (ov_i > 4 * ov_j - 2) & (ov_j < N_SEL)).astype(F32)
    imp_t = _dot_nt(ov_t, p_sum, precision=lax.Precision.HIGHEST)[:N_SEL]
    j_blk = lax.broadcasted_iota(I32, (N_SEL, rows), 0)
    cur = (t0 + lax.broadcasted_iota(I32, (N_SEL, rows), 1)) >> SEL_SHIFT
    forced = (j_blk == 0) | (j_blk == cur) | (j_blk == cur - 1)
    imp_t = jnp.where(forced, FORCE, imp_t)
    imp_t = jnp.where(j_blk <= cur, imp_t, -FORCE)
    rank = jnp.zeros((N_SEL, rows), I32)
    for jp in range(N_SEL):
        row = imp_t[jp:jp + 1, :]
        beats = (row > imp_t) | ((row == imp_t) & (j_blk > jp))
        rank += beats.astype(I32)
    drop_t = jnp.where(rank < SEL_TOPK, 0.0, NEG_INF)
    drop_t = jnp.concatenate([drop_t, jnp.zeros((LANES - N_SEL, rows), F32)], axis=0)
    drop_q = jnp.concatenate([drop_t[:, i * LANES:(i + 1) * LANES].T for i in range(rows // LANES)], axis=0)
    drop_ref[...] = pltpu.roll(drop_q, X_SEL, axis=1).astype(BF16)


def _nsa_kernel(q_ref, misc_ref, ocmp_ref, drop_ref, ksa_ref, vs_ref, kwa_ref, vw_ref, o_ref):
    for qb in range(N_QB):
        @pl.when(pl.program_id(2) == qb)
        def _(qb=qb):
            _nsa_body(qb, q_ref, misc_ref, ocmp_ref, drop_ref, ksa_ref, vs_ref, kwa_ref, vw_ref, o_ref)


def _nsa_body(qb, q_ref, misc_ref, ocmp_ref, drop_ref, ksa_ref, vs_ref, kwa_ref, vw_ref, o_ref):
    g = pl.program_id(1)
    t0 = qb * Q_BLOCK

    def dist_to(c):
        return (t0 + lax.broadcasted_iota(I32, (Q_BLOCK, KV_CHUNK), 0)
                - (c * KV_CHUNK + lax.broadcasted_iota(I32, (Q_BLOCK, KV_CHUNK), 1)))

    c_diag = qb
    dist_diag = dist_to(c_diag)
    n_slc = (c_diag + 1) * KV_CHUNK

    slc_masks = [None] * c_diag + [dist_diag >= 0]
    q_sel = _head_queries(q_ref, g, 1.0, drop_ref[...].astype(F32))

    c_first = max(c_diag - WINDOW // KV_CHUNK, 0)
    win_masks = [None] * (c_diag - c_first + 1)
    win_masks[-1] = dist_diag >= 0
    if c_diag - c_first == WINDOW // KV_CHUNK:
        win_masks[0] = dist_to(c_first) < WINDOW
    win_rows = slice(c_first * KV_CHUNK, n_slc)
    q_win = _head_queries(q_ref, g, 1.0, 0.0)

    both = lambda masks: [None if m is None else jnp.concatenate([m, m], axis=0) for m in masks]
    tasks = []
    for r in range(0, NSA_REP, 2):
        tasks.append((jnp.concatenate(q_win[r:r + 2], axis=0), kwa_ref[win_rows, :], vw_ref[win_rows, :],
                      both(win_masks)))
    for r in range(0, NSA_REP, 2):
        tasks.append((jnp.concatenate(q_sel[r:r + 2], axis=0), ksa_ref[:n_slc, :], vs_ref[:n_slc, :],
                      both(slc_masks)))
    outs = _attend_pipelined(tasks)
    split = lambda o: [o[:Q_BLOCK], o[Q_BLOCK:]]
    o_win = split(outs[0]) + split(outs[1])
    o_slc = split(outs[2]) + split(outs[3])

    gates = jax.nn.sigmoid(misc_ref[...])
    heads = [gates[:, 3 * r + 1:3 * r + 2] * o_slc[r] + gates[:, 3 * r + 2:3 * r + 3] * o_win[r]
             for r in range(NSA_REP)]
    for i, tile in enumerate(_head_layout(heads, g)):
        cols = slice(i * LANES, (i + 1) * LANES)
        o_ref[:, cols] = (tile + ocmp_ref[:, cols].astype(F32)).astype(BF16)


def _nsa(q, misc, kca, vc2, ksa, vs, kwa, vw):
    n = q.shape[0]
    b = n // SEQ
    gw = NSA_REP * HEAD_DIM
    assert WINDOW % KV_CHUNK == 0 and KV_CHUNK == Q_BLOCK and SEQ % SEL_ROWS == 0
    n_sel = SEQ // SEL_ROWS
    sel_rows = lambda width: pl.BlockSpec((SEL_ROWS, width), lambda i, g, j: (i * n_sel + j, g))
    ocmp, drop = pl.pallas_call(
        _nsa_select_kernel,
        grid=(b, NSA_GROUPS, n_sel),
        in_specs=[sel_rows(gw), sel_rows(LANES),
                  pl.BlockSpec((1, N_CMP_PAD, LANES), lambda i, g, j: (i, 0, g)),
                  pl.BlockSpec((1, N_CMP_PAD, LANES), lambda i, g, j: (i, 0, 0))],
        out_specs=[sel_rows(gw), sel_rows(LANES)],
        out_shape=[jax.ShapeDtypeStruct((n, NSA_GROUPS * gw), BF16),
                   jax.ShapeDtypeStruct((n, NSA_GROUPS * LANES), BF16)],
        compiler_params=_params(("parallel", "parallel", "parallel"), 32 << 20),
        name="nsa_select",
    )(q, misc, kca, vc2)

    q_rows = lambda width: pl.BlockSpec((Q_BLOCK, width), lambda i, g, c: (i * N_QB + c, g))
    return pl.pallas_call(
        _nsa_kernel,
        grid=(b, NSA_GROUPS, N_QB),
        in_specs=[q_rows(gw), q_rows(LANES), q_rows(gw), q_rows(LANES),
                  pl.BlockSpec((SEQ, LANES), lambda i, g, c: (i, g)),
                  pl.BlockSpec((SEQ, LANES), lambda i, g, c: (i, 0)),
                  pl.BlockSpec((SEQ, LANES), lambda i, g, c: (i, g)),
                  pl.BlockSpec((SEQ, LANES), lambda i, g, c: (i, 0))],
        out_specs=q_rows(gw),
        out_shape=jax.ShapeDtypeStruct((n, NSA_GROUPS * gw), BF16),
        compiler_params=_params(("parallel", "parallel", "arbitrary"), 48 << 20),
        name="nsa",
    )(q, misc, ocmp, drop, ksa, vs, kwa, vw)


def _gla_kernel(q_ref, k_ref, v_ref, r_ref, misc_ref, wa_ref, ba_ref, gn_ref, o_ref):
    c, blk = GLA_CHUNK, GLA_BLOCK
    n_c = blk // c
    lane = lax.broadcasted_iota(I32, (blk, LANES), 1)
    row = lax.broadcasted_iota(I32, (blk, LANES), 0)
    in_chunk = row & (c - 1)
    chunk_of_row = row >> GLA_CHUNK_SHIFT
    r2 = lax.broadcasted_iota(I32, (2 * blk, blk), 0) & (blk - 1)
    c2 = lax.broadcasted_iota(I32, (2 * blk, blk), 1)
    intra = (r2 >= c2) & ((r2 >> GLA_CHUNK_SHIFT) == (c2 >> GLA_CHUNK_SHIFT))
    lane_s = lax.broadcasted_iota(I32, (GLA_DV, LANES), 1)
    n_blk = SEQ // blk

    def prep(i_blk):
        rows = slice(i_blk * blk, (i_blk + 1) * blk)
        la = _dot(misc_ref[rows, :], wa_ref[...], precision=lax.Precision.HIGHEST) + ba_ref[...]
        b = (jnp.minimum(la, 0.0) - jnp.log(1.0 + jnp.exp(-jnp.abs(la)))) * (1.0 / GLA_TAU)
        shift = 1
        while shift < c:
            b = b + jnp.where(in_chunk >= shift, pltpu.roll(b, shift, axis=0), 0.0)
            shift *= 2
        b3 = b.reshape(n_c, c, LANES)
        b_last = b3[:, c - 1:c, :]
        k = k_ref[rows, :]
        q_in = q_ref[rows, :] * (GLA_DK ** -0.5) * jnp.exp(b)
        k_in = (k * jnp.exp(-b)).astype(BF16)
        k_st = (k.reshape(n_c, c, LANES) * jnp.exp(b_last - b3)).reshape(blk, LANES)
        decay = jnp.exp(b_last)
        v = v_ref[rows, :]

        q2 = jnp.concatenate([jnp.where(lane < GLA_DK, q_in, 0.0), jnp.where(lane >= GLA_DK, q_in, 0.0)],
                             axis=0).astype(BF16)
        k_cols = jnp.concatenate([jnp.where(chunk_of_row == i_c, k_st, 0.0) for i_c in range(n_c)], axis=1)
        return q2, k_in, k_cols.astype(BF16), v, decay

    def chunk_parallel(q2, k_in, k_cols, v, decay):
        a = jnp.where(intra, _dot_nt(q2, k_in), 0.0).astype(BF16)
        o_intra = _dot(a, v)
        inc = _dot_tn(v, k_cols)
        return q2, o_intra, inc, decay

    per_blk = []
    ready = prep(0)
    for i_blk in range(n_blk):
        upcoming = prep(i_blk + 1) if i_blk + 1 < n_blk else None
        per_blk.append(chunk_parallel(*ready))
        ready = upcoming

    st = jnp.zeros((GLA_DV, LANES), F32)
    state_before = []
    for _, _, inc, decay in per_blk:
        for i_c in range(n_c):
            state_before.append(st.astype(BF16))
            cols = slice(i_c * LANES, (i_c + 1) * LANES)
            st = st * decay[i_c] + jnp.where(lane_s < GLA_DK, inc[:GLA_DV, cols], inc[GLA_DV:, cols])

    for i_blk, (q2, o_intra, _, _) in enumerate(per_blk):
        rows = slice(i_blk * blk, (i_blk + 1) * blk)
        o_inter = []
        for i_c in range(n_c):
            q_c = jnp.concatenate([q2[i_c * c:(i_c + 1) * c], q2[blk + i_c * c:blk + (i_c + 1) * c]], axis=0)
            o_inter.append(_dot_nt(q_c, state_before[i_blk * n_c + i_c]))
        for h in range(2):
            cols = slice(h * GLA_DV, (h + 1) * GLA_DV)
            o = o_intra[h * blk:(h + 1) * blk, cols] + jnp.concatenate(
                [o_inter[i_c][h * c:(h + 1) * c] for i_c in range(n_c)], axis=0)
            gate = r_ref[rows, cols]
            o_ref[rows, cols] = (_rms(o, gn_ref[...]) * (gate * jax.nn.sigmoid(gate))).astype(BF16)


def _gla(gq, gk, gv, gr, misc, wa, ba, gn):
    n = gq.shape[0]
    b = n // SEQ
    return pl.pallas_call(
        _gla_kernel,
        grid=(b, GLA_HEADS // 2),
        in_specs=[
            pl.BlockSpec((SEQ, LANES), lambda i, p: (i, p)),
            pl.BlockSpec((SEQ, LANES), lambda i, p: (i, p)),
            pl.BlockSpec((SEQ, 2 * GLA_DV), lambda i, p: (i, p)),
            pl.BlockSpec((SEQ, 2 * GLA_DV), lambda i, p: (i, p)),
            pl.BlockSpec((SEQ, LANES), lambda i, p: (i, 0)),
            pl.BlockSpec((LANES, LANES), lambda i, p: (0, p)),
            pl.BlockSpec((1, LANES), lambda i, p: (0, p)),
            pl.BlockSpec((1, GLA_DV), lambda i, p: (0, 0)),
        ],
        out_specs=pl.BlockSpec((SEQ, 2 * GLA_DV), lambda i, p: (i, p)),
        out_shape=jax.ShapeDtypeStruct((n, GLA_HEADS * GLA_DV), BF16),
        compiler_params=_params(("parallel", "parallel"), 40 << 20),
        name="gla",
    )(gq, gk, gv, gr, misc, wa, ba, gn)


def _permute_w_in(w):
    w = w.astype(BF16)
    d = w.shape[0]
    col = lambda lo, hi: w[:, lo:hi]
    zeros = lambda k: jnp.zeros((d, k), w.dtype)
    hd = HEAD_DIM

    def keys_padded(k0):
        return [col(k0, k0 + hd), zeros(LANES - hd), col(k0 + hd, k0 + 2 * hd), zeros(LANES - hd)]

    n_gate = 3 * NSA_REP
    misc = [col(1280, 1280 + n_gate), col(2840, 2856), zeros(LANES - n_gate - GLA_LOWRANK),
            col(1280 + n_gate, 1304), zeros(LANES - n_gate)]
    parts = ([col(0, 512), col(512, 640), col(640, 768)] + keys_padded(768) + keys_padded(1024)
             + [col(896, 1024), col(1152, 1280)] + misc
             + [col(1304, 1560), col(1560, 1816), col(1816, 2328), col(2328, 2840)])
    out = jnp.concatenate(parts, axis=1)
    assert out.shape[1] == D_IN_PAD
    return out


def _compress_weights(pe, w1, w2, group_stride):
    hd, hid = HEAD_DIM, CMP_HIDDEN
    pe2 = jnp.concatenate([pe, pe], axis=1)
    w1 = w1.reshape(CMP_BLOCK, hd, hid)
    z1 = jnp.zeros_like(w1)
    w1e = jnp.concatenate([jnp.concatenate([w1, z1], axis=2), jnp.concatenate([z1, w1], axis=2)], axis=1)
    w2e = jnp.zeros((NSA_GROUPS * hid, NSA_GROUPS * group_stride), w2.dtype)
    for g in range(NSA_GROUPS):
        w2e = w2e.at[g * hid:(g + 1) * hid, g * group_stride:g * group_stride + hd].set(w2)
    return pe2, w1e.astype(BF16), w2e.astype(BF16)


def kernel(x, ffn1_norm, ffn1_w_gate, ffn1_w_up, ffn1_w_down, mix_norm, w_in, nsa_pe_k, nsa_w1_k, nsa_w2_k,
           nsa_pe_v, nsa_w1_v, nsa_w2_v, gla_w_a2, gla_b_a, gla_norm, w_out, ffn2_norm, ffn2_w_gate,
           ffn2_w_up, ffn2_w_down, final_norm):
    bsz, seq, d = x.shape
    assert (seq, d) == (SEQ, D_MODEL) and ffn1_norm.shape[0] == 1
    n = bsz * seq
    xf = x.reshape(n, d)
    row = lambda v: v.reshape(1, -1).astype(F32)
    bf = lambda w: w.astype(BF16)
    ones = jnp.ones((1, d), F32)

    x1 = _ffn(xf, (), row(ffn1_norm[0]), bf(ffn1_w_gate[0]), bf(ffn1_w_up[0]), bf(ffn1_w_down[0]), ones,
              final_norm=False)

    q, kc, vc, ksa, kwa, vs, vw, misc, gq, gk, gv, gr = _inproj(x1, row(mix_norm[0]), _permute_w_in(w_in[0]))

    pek, w1k, w2k = _compress_weights(nsa_pe_k[0], nsa_w1_k[0], nsa_w2_k[0], LANES)
    pev, w1v, w2v = _compress_weights(nsa_pe_v[0], nsa_w1_v[0], nsa_w2_v[0], HEAD_DIM)
    kca, vc2 = _compress(kc, vc, pek, pev, w1k, w1v, w2k, w2v)

    o_nsa = _nsa(q, misc, kca, vc2, ksa, vs, kwa, vw)

    wa = jnp.zeros((LANES, GLA_HEADS * GLA_DK), F32).at[MISC_GA_OFF:MISC_GA_OFF + GLA_LOWRANK].set(gla_w_a2[0])
    o_gla = _gla(gq, gk, gv, gr, misc, wa, row(gla_b_a[0]), row(gla_norm[0]))

    d_nsa = NSA_HEADS * HEAD_DIM
    w_o = bf(w_out[0])
    out = _ffn(x1, (o_nsa, o_gla, w_o[:d_nsa], w_o[d_nsa:]), row(ffn2_norm[0]), bf(ffn2_w_gate[0]),
               bf(ffn2_w_up[0]), bf(ffn2_w_down[0]), row(final_norm), final_norm=True)
    return out.reshape(bsz, seq, d)
```

```python
import functools

import jax
import jax.numpy as jnp
from jax import lax
from jax.experimental import pallas as pl
from jax.experimental.pallas import tpu as pltpu

F32 = jnp.float32
BF16 = jnp.bfloat16
I32 = jnp.int32

D_MODEL = 1024
SEQ = 2048
D_FF = 2816
EPS = 1e-6
NEG_INF = -1e30
FORCE = 1e9

NSA_HEADS = 8
NSA_GROUPS = 2
NSA_REP = NSA_HEADS // NSA_GROUPS
HEAD_DIM = 64
CMP_BLOCK = 32
CMP_STRIDE = 16
CMP_HIDDEN = 128
N_CMP_PAD = SEQ // CMP_STRIDE
SEL_BLOCK = 64
SEL_SHIFT = 6
assert 1 << SEL_SHIFT == SEL_BLOCK
N_SEL = SEQ // SEL_BLOCK
SEL_TOPK = 8
WINDOW = 512
Q_BLOCK = 256
N_QB = SEQ // Q_BLOCK
KV_CHUNK = Q_BLOCK
SEL_ROWS = 1024

GLA_HEADS = 4
GLA_DK = 64
GLA_DV = 128
GLA_CHUNK = 64
GLA_CHUNK_SHIFT = 6
assert 1 << GLA_CHUNK_SHIFT == GLA_CHUNK
GLA_BLOCK = 256
GLA_LOWRANK = 16
GLA_TAU = 16.0

LANES = 128
V7X_VMEM_BYTES = 64 * 1024 * 1024

C_Q = (0, 512)
C_KC = (512, 640)
C_VC = (640, 768)
C_KSA = (768, 1024)
C_KWA = (1024, 1280)
C_VS = (1280, 1408)
C_VW = (1408, 1536)
C_MISC = (1536, 1792)
C_GQ = (1792, 2048)
C_GK = (2048, 2304)
C_GV = (2304, 2816)
C_GR = (2816, 3328)
D_IN_PAD = 3328
MISC_GA_OFF = 12

X_SEL = HEAD_DIM
X_HI = HEAD_DIM + N_SEL
X_LO = X_HI + 1
POS_SPLIT = 64
POS_SHIFT = 6
assert 1 << POS_SHIFT == POS_SPLIT and X_LO < LANES


def _params(sem, vmem_bytes):
    return pltpu.CompilerParams(dimension_semantics=sem,
                                vmem_limit_bytes=min(int(vmem_bytes), V7X_VMEM_BYTES - (8 << 20)))


def _rms(x, g):
    return x * lax.rsqrt(jnp.mean(x * x, axis=-1, keepdims=True) + EPS) * g


def _dot(a, b, **kw):
    return jnp.dot(a, b, preferred_element_type=F32, **kw)


def _dot_nt(a, b, **kw):
    return lax.dot_general(a, b, (((1,), (1,)), ((), ())), preferred_element_type=F32, **kw)


def _dot_3pass(a, b):
    a_hi, b_hi = a.astype(BF16), b.astype(BF16)
    a_lo = (a - a_hi.astype(F32)).astype(BF16)
    b_lo = (b - b_hi.astype(F32)).astype(BF16)
    return _dot(a_hi, b_hi) + _dot(a_lo, b_hi) + _dot(a_hi, b_lo)


def _dot_tn(a, b, **kw):
    return lax.dot_general(a, b, (((0,), (0,)), ((), ())), preferred_element_type=F32, **kw)


def _ffn_kernel(*refs, tf, final_norm, mixer_out):
    if mixer_out:
        x_ref, a_ref, b_ref, wa_ref, wb_ref, g_ref, wg_ref, wu_ref, wd_ref, fg_ref, o_ref, act_ref = refs
        x = x_ref[...] + _dot(a_ref[...], wa_ref[...]) + _dot(b_ref[...], wb_ref[...])
    else:
        x_ref, g_ref, wg_ref, wu_ref, wd_ref, fg_ref, o_ref, act_ref = refs
        x = x_ref[...]
    h = _rms(x, g_ref[...]).astype(BF16)
    for c in range(D_FF // tf):
        cols = slice(c * tf, (c + 1) * tf)
        gate = _dot(h, wg_ref[:, cols])
        up = _dot(h, wu_ref[:, cols])
        act_ref[:, cols] = ((gate * jax.nn.sigmoid(gate)) * up).astype(BF16)
    y = x + 0.5 * _dot(act_ref[...], wd_ref[...])
    if final_norm:
        y = _rms(y, fg_ref[...])
    o_ref[...] = y


def _ffn(x, mixer, gain, wg, wu, wd, final_gain, *, final_norm, tm=512, tf=256):
    n, d = x.shape
    assert D_FF % tf == 0
    resident = lambda shape: pl.BlockSpec(shape, lambda i: (0, 0), pipeline_mode=pl.Buffered(1))
    rows = lambda width: pl.BlockSpec((tm, width), lambda i: (i, 0))
    vmem = 2 * 2 * tm * d * 4 + 3 * d * D_FF * 2 + tm * D_FF * 2 + tm * d * 2 + 4 * tm * tf * 4 + (8 << 20)
    mix_specs = []
    if mixer:
        a, b, w_a, w_b = mixer
        mix_specs = [rows(a.shape[1]), rows(b.shape[1]), resident(w_a.shape), resident(w_b.shape)]
        vmem += 2 * tm * (a.shape[1] + b.shape[1]) * 2 + (w_a.size + w_b.size) * 2
    return pl.pallas_call(
        functools.partial(_ffn_kernel, tf=tf, final_norm=final_norm, mixer_out=bool(mixer)),
        grid=(n // tm,),
        in_specs=[rows(d)] + mix_specs + [
            resident((1, d)),
            resident((d, D_FF)),
            resident((d, D_FF)),
            resident((D_FF, d)),
            resident((1, d)),
        ],
        out_specs=rows(d),
        out_shape=jax.ShapeDtypeStruct((n, d), F32),
        scratch_shapes=[pltpu.VMEM((tm, D_FF), BF16)],
        compiler_params=_params(("parallel",), vmem),
        name="ffn",
    )(x, *mixer, gain, wg, wu, wd, final_gain)


def _key_extras(pos, lane, with_block):
    lg = lane & (LANES - 1)
    ext = jnp.where(lg == X_HI, (pos >> POS_SHIFT).astype(F32),
                    jnp.where(lg == X_LO, (pos & (POS_SPLIT - 1)).astype(F32), 0.0))
    if with_block:
        ext = jnp.where((lg >= X_SEL) & (lg < X_HI) & ((pos >> SEL_SHIFT) == lg - X_SEL), 1.0, ext)
    return ext


def _inproj_kernel(x_ref, g_ref, w_ref, q_ref, kc_ref, vc_ref, ksa_ref, kwa_ref, vs_ref, vw_ref, misc_ref,
                   gq_ref, gk_ref, gv_ref, gr_ref, *, tm):
    h = _rms(x_ref[...], g_ref[...]).astype(BF16)

    def proj(c):
        return _dot(h, w_ref[:, c[0]:c[1]])

    shape = (tm, 2 * LANES)
    lane = lax.broadcasted_iota(I32, shape, 1)
    pos = (pl.program_id(0) * tm + lax.broadcasted_iota(I32, shape, 0)) & (SEQ - 1)
    is_key = (lane & (LANES - 1)) < HEAD_DIM

    q_ref[...] = (proj(C_Q) * (HEAD_DIM ** -0.5)).astype(BF16)
    kc_ref[...] = proj(C_KC)
    vc_ref[...] = proj(C_VC)
    ksa_ref[...] = jnp.where(is_key, proj(C_KSA), _key_extras(pos, lane, True)).astype(BF16)
    kwa_ref[...] = jnp.where(is_key, proj(C_KWA), _key_extras(pos, lane, False)).astype(BF16)
    vs_ref[...] = proj(C_VS).astype(BF16)
    vw_ref[...] = proj(C_VW).astype(BF16)
    misc_ref[...] = proj(C_MISC)
    gq_ref[...] = proj(C_GQ)
    gk_ref[...] = proj(C_GK)
    gv_ref[...] = proj(C_GV).astype(BF16)
    gr_ref[...] = proj(C_GR)


def _inproj(x, gain, w, tm=512):
    n, d = x.shape
    assert SEQ % tm == 0 and SEQ & (SEQ - 1) == 0
    outs = [(C_Q, BF16), (C_KC, F32), (C_VC, F32), (C_KSA, BF16), (C_KWA, BF16), (C_VS, BF16), (C_VW, BF16),
            (C_MISC, F32), (C_GQ, F32), (C_GK, F32), (C_GV, BF16), (C_GR, F32)]
    vmem = 2 * tm * d * 4 + 2 * d * D_IN_PAD * 2 + 2 * tm * D_IN_PAD * 4 + (8 << 20)
    return pl.pallas_call(
        functools.partial(_inproj_kernel, tm=tm),
        grid=(n // tm,),
        in_specs=[
            pl.BlockSpec((tm, d), lambda i: (i, 0)),
            pl.BlockSpec((1, d), lambda i: (0, 0)),
            pl.BlockSpec((d, D_IN_PAD), lambda i: (0, 0)),
        ],
        out_specs=[pl.BlockSpec((tm, c[1] - c[0]), lambda i: (i, 0)) for c, _ in outs],
        out_shape=[jax.ShapeDtypeStruct((n, c[1] - c[0]), dt) for c, dt in outs],
        compiler_params=_params(("parallel",), vmem),
        name="inproj",
    )(x, gain, w)


def _compress_kernel(kc_ref, vc_ref, pek_ref, pev_ref, w1k_ref, w1v_ref, w2k_ref, w2v_ref, kca_ref, vc2_ref):
    half = CMP_BLOCK // 2

    def hidden(x_ref, pe_ref, w1_ref):
        acc_a = jnp.zeros((N_CMP_PAD, 2 * CMP_HIDDEN), F32)
        acc_b = jnp.zeros((N_CMP_PAD, 2 * CMP_HIDDEN), F32)
        for l in range(half):
            rows = x_ref[pl.ds(l, N_CMP_PAD, stride=CMP_STRIDE), :]
            acc_a += _dot((rows + pe_ref[l:l + 1, :]).astype(BF16), w1_ref[l])
            acc_b += _dot((rows + pe_ref[half + l:half + l + 1, :]).astype(BF16), w1_ref[half + l])
        pre = acc_a + pltpu.roll(acc_b, N_CMP_PAD - 1, axis=0)
        return jax.nn.gelu(pre, approximate=True).astype(BF16)

    kc = _dot(hidden(kc_ref, pek_ref, w1k_ref), w2k_ref[...])
    vc = _dot(hidden(vc_ref, pev_ref, w1v_ref), w2v_ref[...])
    lane = lax.broadcasted_iota(I32, kc.shape, 1)
    row = lax.broadcasted_iota(I32, kc.shape, 0)
    kc = jnp.where((lane & (LANES - 1)) < HEAD_DIM, kc, _key_extras(2 * CMP_STRIDE * row + CMP_BLOCK - 1, lane, False))
    kca_ref[0] = jnp.where(row < N_CMP_PAD - 1, kc, 0.0).astype(BF16)
    vc2_ref[0] = jnp.where(row[:, :LANES] < N_CMP_PAD - 1, vc, 0.0).astype(BF16)


def _compress(kc, vc, pek, pev, w1k, w1v, w2k, w2v):
    n = kc.shape[0]
    b = n // SEQ
    full = lambda a: pl.BlockSpec(a.shape, lambda i: (0,) * a.ndim)
    return pl.pallas_call(
        _compress_kernel,
        grid=(b,),
        in_specs=[
            pl.BlockSpec((SEQ, LANES), lambda i: (i, 0)),
            pl.BlockSpec((SEQ, LANES), lambda i: (i, 0)),
            full(pek), full(pev), full(w1k), full(w1v), full(w2k), full(w2v),
        ],
        out_specs=[pl.BlockSpec((1, N_CMP_PAD, 2 * LANES), lambda i: (i, 0, 0)),
                   pl.BlockSpec((1, N_CMP_PAD, LANES), lambda i: (i, 0, 0))],
        out_shape=[jax.ShapeDtypeStruct((b, N_CMP_PAD, 2 * LANES), BF16),
                   jax.ShapeDtypeStruct((b, N_CMP_PAD, LANES), BF16)],
        compiler_params=_params(("parallel",), 32 << 20),
        name="compress",
    )(kc, vc, pek, pev, w1k, w1v, w2k, w2v)


def _attend_pipelined(tasks):
    n = len(tasks)
    scores, probs, outs = [None] * n, [None] * n, [None] * n
    for i in range(n + 2):
        if i < n:
            q, k, _, _ = tasks[i]
            scores[i] = _dot_nt(q, k)
        if 1 <= i <= n:
            _, _, _, chunk_masks = tasks[i - 1]
            cols = []
            for c, allowed in enumerate(chunk_masks):
                s = scores[i - 1][:, c * KV_CHUNK:(c + 1) * KV_CHUNK]
                cols.append(s if allowed is None else jnp.where(allowed, s, NEG_INF))
            s = cols[0] if len(cols) == 1 else jnp.concatenate(cols, axis=1)
            p = jnp.exp(s - jnp.max(s, axis=-1, keepdims=True))
            probs[i - 1] = (p.astype(BF16), 1.0 / jnp.sum(p, axis=-1, keepdims=True))
            scores[i - 1] = None
        if 2 <= i:
            p, inv = probs[i - 2]
            outs[i - 2] = _dot(p, tasks[i - 2][2]) * inv
            probs[i - 2] = None
    return outs


def _head_queries(q_ref, g, slope_scale, sel_lanes):
    rows = q_ref.shape[0]
    lane = lax.broadcasted_iota(I32, (rows, LANES), 1)
    qf = q_ref[...].astype(F32)
    parts = []
    for r in range(NSA_REP):
        slope = slope_scale * jnp.where(g == 0, 2.0 ** -(r + 1), 2.0 ** -(r + 1 + NSA_REP)).astype(F32)
        tile = qf[:, (r // 2) * LANES:(r // 2 + 1) * LANES]
        if r % 2:
            tile = pltpu.roll(tile, HEAD_DIM, axis=1)
        ext = jnp.where(lane == X_HI, POS_SPLIT * slope, jnp.where(lane == X_LO, slope, sel_lanes))
        parts.append(jnp.where(lane < HEAD_DIM, tile, ext).astype(BF16))
    return parts


def _head_layout(heads, g):
    low = lax.broadcasted_iota(I32, heads[0].shape, 1) < HEAD_DIM
    tiles = []
    for pair in range(NSA_REP // 2):
        even, odd = heads[2 * pair], heads[2 * pair + 1]
        left = jnp.where(g == 0, even, pltpu.roll(even, HEAD_DIM, axis=1))
        right = jnp.where(g == 0, pltpu.roll(odd, HEAD_DIM, axis=1), odd)
        tiles.append(jnp.where(low, left, right))
    return tiles


def _nsa_select_kernel(q_ref, misc_ref, kca_ref, vc2_ref, ocmp_ref, drop_ref):
    g = pl.program_id(1)
    rows = SEL_ROWS
    t0 = pl.program_id(2) * rows
    lane = lax.broadcasted_iota(I32, (rows, LANES), 1)
    tq = t0 + lax.broadcasted_iota(I32, (rows, LANES), 0)

    q4 = jnp.concatenate(_head_queries(q_ref, g, 0.5, 0.0), axis=0)
    s_c = _dot_nt(q4, kca_ref[0])
    valid_c = lane * CMP_STRIDE + (CMP_BLOCK - 1) <= tq
    p_sum = jnp.zeros((rows, LANES), F32)
    p_parts = []
    for r in range(NSA_REP):
        s = jnp.where(valid_c, s_c[r * rows:(r + 1) * rows], NEG_INF)
        e = jnp.exp(s - jnp.max(s, axis=-1, keepdims=True))
        p = jnp.where(valid_c, e / jnp.sum(e, axis=-1, keepdims=True), 0.0)
        p_sum += p
        p_parts.append(p.astype(BF16))
    o_cmp = _dot(jnp.concatenate(p_parts, axis=0), vc2_ref[0])
    gates = jax.nn.sigmoid(misc_ref[...])
    tiles = _head_layout([gates[:, 3 * r:3 * r + 1] * o_cmp[r * rows:(r + 1) * rows] for r in range(NSA_REP)], g)
    for i, tile in enumerate(tiles):
        ocmp_ref[:, i * LANES:(i + 1) * LANES] = tile.astype(BF16)

    ov_j = lax.broadcasted_iota(I32, (LANES, LANES), 0)
    ov_i = lax.broadcasted_iota(I32, (LANES, LANES), 1)
    ov_t = ((ov_i < 4 * ov_j + 4) & (ov_i > 4 * ov_j - 2) & (ov_j < N_SEL)).astype(F32)
    imp_t = _dot_nt(ov_t, p_sum, precision=lax.Precision.HIGHEST)[:N_SEL]
    j_blk = lax.broadcasted_iota(I32, (N_SEL, rows), 0)
    cur = (t0 + lax.broadcasted_iota(I32, (N_SEL, rows), 1)) >> SEL_SHIFT
    forced = (j_blk == 0) | (j_blk == cur) | (j_blk == cur - 1)
    imp_t = jnp.where(forced, FORCE, imp_t)
    imp_t = jnp.where(j_blk <= cur, imp_t, -FORCE)
    rank = jnp.zeros((N_SEL, rows), I32)
    for jp in range(N_SEL):
        row = imp_t[jp:jp + 1, :]
        beats = (row > imp_t) | ((row == imp_t) & (j_blk > jp))
        rank += beats.astype(I32)
    drop_t = jnp.where(rank < SEL_TOPK, 0.0, NEG_INF)
    drop_t = jnp.concatenate([drop_t, jnp.zeros((LANES - N_SEL, rows), F32)], axis=0)
    drop_q = jnp.concatenate([drop_t[:, i * LANES:(i + 1) * LANES].T for i in range(rows // LANES)], axis=0)
    drop_ref[...] = pltpu.roll(drop_q, X_SEL, axis=1).astype(BF16)


def _nsa_kernel(q_ref, misc_ref, ocmp_ref, drop_ref, ksa_ref, vs_ref, kwa_ref, vw_ref, o_ref):
    for qb in range(N_QB):
        @pl.when(pl.program_id(2) == qb)
        def _(qb=qb):
            _nsa_body(qb, q_ref, misc_ref, ocmp_ref, drop_ref, ksa_ref, vs_ref, kwa_ref, vw_ref, o_ref)


def _nsa_body(qb, q_ref, misc_ref, ocmp_ref, drop_ref, ksa_ref, vs_ref, kwa_ref, vw_ref, o_ref):
    g = pl.program_id(1)
    t0 = qb * Q_BLOCK

    def dist_to(c):
        return (t0 + lax.broadcasted_iota(I32, (Q_BLOCK, KV_CHUNK), 0)
                - (c * KV_CHUNK + lax.broadcasted_iota(I32, (Q_BLOCK, KV_CHUNK), 1)))

    c_diag = qb
    dist_diag = dist_to(c_diag)
    n_slc = (c_diag + 1) * KV_CHUNK

    slc_masks = [None] * c_diag + [dist_diag >= 0]
    q_sel = _head_queries(q_ref, g, 1.0, drop_ref[...].astype(F32))

    c_first = max(c_diag - WINDOW // KV_CHUNK, 0)
    win_masks = [None] * (c_diag - c_first + 1)
    win_masks[-1] = dist_diag >= 0
    if c_diag - c_first == WINDOW // KV_CHUNK:
        win_masks[0] = dist_to(c_first) < WINDOW
    win_rows = slice(c_first * KV_CHUNK, n_slc)
    q_win = _head_queries(q_ref, g, 1.0, 0.0)

    both = lambda masks: [None if m is None else jnp.concatenate([m, m], axis=0) for m in masks]
    tasks = []
    for r in range(0, NSA_REP, 2):
        tasks.append((jnp.concatenate(q_win[r:r + 2], axis=0), kwa_ref[win_rows, :], vw_ref[win_rows, :],
                      both(win_masks)))
    for r in range(0, NSA_REP, 2):
        tasks.append((jnp.concatenate(q_sel[r:r + 2], axis=0), ksa_ref[:n_slc, :], vs_ref[:n_slc, :],
                      both(slc_masks)))
    outs = _attend_pipelined(tasks)
    split = lambda o: [o[:Q_BLOCK], o[Q_BLOCK:]]
    o_win = split(outs[0]) + split(outs[1])
    o_slc = split(outs[2]) + split(outs[3])

    gates = jax.nn.sigmoid(misc_ref[...])
    heads = [gates[:, 3 * r + 1:3 * r + 2] * o_slc[r] + gates[:, 3 * r + 2:3 * r + 3] * o_win[r]
             for r in range(NSA_REP)]
    for i, tile in enumerate(_head_layout(heads, g)):
        cols = slice(i * LANES, (i + 1) * LANES)
        o_ref[:, cols] = (tile + ocmp_ref[:, cols].astype(F32)).astype(BF16)


def _nsa(q, misc, kca, vc2, ksa, vs, kwa, vw):
    n = q.shape[0]
    b = n // SEQ
    gw = NSA_REP * HEAD_DIM
    assert WINDOW % KV_CHUNK == 0 and KV_CHUNK == Q_BLOCK and SEQ % SEL_ROWS == 0
    n_sel = SEQ // SEL_ROWS
    sel_rows = lambda width: pl.BlockSpec((SEL_ROWS, width), lambda i, g, j: (i * n_sel + j, g))
    ocmp, drop = pl.pallas_call(
        _nsa_select_kernel,
        grid=(b, NSA_GROUPS, n_sel),
        in_specs=[sel_rows(gw), sel_rows(LANES),
                  pl.BlockSpec((1, N_CMP_PAD, LANES), lambda i, g, j: (i, 0, g)),
                  pl.BlockSpec((1, N_CMP_PAD, LANES), lambda i, g, j: (i, 0, 0))],
        out_specs=[sel_rows(gw), sel_rows(LANES)],
        out_shape=[jax.ShapeDtypeStruct((n, NSA_GROUPS * gw), BF16),
                   jax.ShapeDtypeStruct((n, NSA_GROUPS * LANES), BF16)],
        compiler_params=_params(("parallel", "parallel", "parallel"), 32 << 20),
        name="nsa_select",
    )(q, misc, kca, vc2)

    q_rows = lambda width: pl.BlockSpec((Q_BLOCK, width), lambda i, g, c: (i * N_QB + c, g))
    return pl.pallas_call(
        _nsa_kernel,
        grid=(b, NSA_GROUPS, N_QB),
        in_specs=[q_rows(gw), q_rows(LANES), q_rows(gw), q_rows(LANES),
                  pl.BlockSpec((SEQ, LANES), lambda i, g, c: (i, g)),
                  pl.BlockSpec((SEQ, LANES), lambda i, g, c: (i, 0)),
                  pl.BlockSpec((SEQ, LANES), lambda i, g, c: (i, g)),
                  pl.BlockSpec((SEQ, LANES), lambda i, g, c: (i, 0))],
        out_specs=q_rows(gw),
        out_shape=jax.ShapeDtypeStruct((n, NSA_GROUPS * gw), BF16),
        compiler_params=_params(("parallel", "parallel", "arbitrary"), 48 << 20),
        name="nsa",
    )(q, misc, ocmp, drop, ksa, vs, kwa, vw)


def _gla_kernel(q_ref, k_ref, v_ref, r_ref, misc_ref, wa_ref, ba_ref, gn_ref, o_ref):
    c, blk = GLA_CHUNK, GLA_BLOCK
    n_c = blk // c
    lane = lax.broadcasted_iota(I32, (blk, LANES), 1)
    row = lax.broadcasted_iota(I32, (blk, LANES), 0)
    in_chunk = row & (c - 1)
    chunk_of_row = row >> GLA_CHUNK_SHIFT
    r2 = lax.broadcasted_iota(I32, (2 * blk, blk), 0) & (blk - 1)
    c2 = lax.broadcasted_iota(I32, (2 * blk, blk), 1)
    intra = (r2 >= c2) & ((r2 >> GLA_CHUNK_SHIFT) == (c2 >> GLA_CHUNK_SHIFT))
    lane_s = lax.broadcasted_iota(I32, (GLA_DV, LANES), 1)
    n_blk = SEQ // blk

    def prep(i_blk):
        rows = slice(i_blk * blk, (i_blk + 1) * blk)
        la = _dot_3pass(misc_ref[rows, :], wa_ref[...]) + ba_ref[...]
        b = (jnp.minimum(la, 0.0) - jnp.log(1.0 + jnp.exp(-jnp.abs(la)))) * (1.0 / GLA_TAU)
        shift = 1
        while shift < c:
            b = b + jnp.where(in_chunk >= shift, pltpu.roll(b, shift, axis=0), 0.0)
            shift *= 2
        b3 = b.reshape(n_c, c, LANES)
        b_last = b3[:, c - 1:c, :]
        k = k_ref[rows, :]
        q_in = q_ref[rows, :] * (GLA_DK ** -0.5) * jnp.exp(b)
        k_in = (k * jnp.exp(-b)).astype(BF16)
        k_st = (k.reshape(n_c, c, LANES) * jnp.exp(b_last - b3)).reshape(blk, LANES)
        decay = jnp.exp(b_last)
        v = v_ref[rows, :]

        q2 = jnp.concatenate([jnp.where(lane < GLA_DK, q_in, 0.0), jnp.where(lane >= GLA_DK, q_in, 0.0)],
                             axis=0).astype(BF16)
        k_cols = jnp.concatenate([jnp.where(chunk_of_row == i_c, k_st, 0.0) for i_c in range(n_c)], axis=1)
        return q2, k_in, k_cols.astype(BF16), v, decay

    def chunk_parallel(q2, k_in, k_cols, v, decay):
        a = jnp.where(intra, _dot_nt(q2, k_in), 0.0).astype(BF16)
        o_intra = _dot(a, v)
        inc = _dot_tn(v, k_cols)
        return q2, o_intra, inc, decay

    per_blk = []
    ready = prep(0)
    for i_blk in range(n_blk):
        upcoming = prep(i_blk + 1) if i_blk + 1 < n_blk else None
        per_blk.append(chunk_parallel(*ready))
        ready = upcoming

    st = jnp.zeros((GLA_DV, LANES), F32)
    state_before = []
    for _, _, inc, decay in per_blk:
        for i_c in range(n_c):
            state_before.append(st.astype(BF16))
            cols = slice(i_c * LANES, (i_c + 1) * LANES)
            st = st * decay[i_c] + jnp.where(lane_s < GLA_DK, inc[:GLA_DV, cols], inc[GLA_DV:, cols])

    for i_blk, (q2, o_intra, _, _) in enumerate(per_blk):
        rows = slice(i_blk * blk, (i_blk + 1) * blk)
        o_inter = []
        for i_c in range(n_c):
            q_c = jnp.concatenate([q2[i_c * c:(i_c + 1) * c], q2[blk + i_c * c:blk + (i_c + 1) * c]], axis=0)
            o_inter.append(_dot_nt(q_c, state_before[i_blk * n_c + i_c]))
        for h in range(2):
            cols = slice(h * GLA_DV, (h + 1) * GLA_DV)
            o = o_intra[h * blk:(h + 1) * blk, cols] + jnp.concatenate(
                [o_inter[i_c][h * c:(h + 1) * c] for i_c in range(n_c)], axis=0)
            gate = r_ref[rows, cols]
            o_ref[rows, cols] = (_rms(o, gn_ref[...]) * (gate * jax.nn.sigmoid(gate))).astype(BF16)


def _gla(gq, gk, gv, gr, misc, wa, ba, gn):
    n = gq.shape[0]
    b = n // SEQ
    return pl.pallas_call(
        _gla_kernel,
        grid=(b, GLA_HEADS // 2),
        in_specs=[
            pl.BlockSpec((SEQ, LANES), lambda i, p: (i, p)),
            pl.BlockSpec((SEQ, LANES), lambda i, p: (i, p)),
            pl.BlockSpec((SEQ, 2 * GLA_DV), lambda i, p: (i, p)),
            pl.BlockSpec((SEQ, 2 * GLA_DV), lambda i, p: (i, p)),
            pl.BlockSpec((SEQ, LANES), lambda i, p: (i, 0)),
            pl.BlockSpec((LANES, LANES), lambda i, p: (0, p)),
            pl.BlockSpec((1, LANES), lambda i, p: (0, p)),
            pl.BlockSpec((1, GLA_DV), lambda i, p: (0, 0)),
        ],
        out_specs=pl.BlockSpec((SEQ, 2 * GLA_DV), lambda i, p: (i, p)),
        out_shape=jax.ShapeDtypeStruct((n, GLA_HEADS * GLA_DV), BF16),
        compiler_params=_params(("parallel", "parallel"), 40 << 20),
        name="gla",
    )(gq, gk, gv, gr, misc, wa, ba, gn)


def _permute_w_in(w):
    w = w.astype(BF16)
    d = w.shape[0]
    col = lambda lo, hi: w[:, lo:hi]
    zeros = lambda k: jnp.zeros((d, k), w.dtype)
    hd = HEAD_DIM

    def keys_padded(k0):
        return [col(k0, k0 + hd), zeros(LANES - hd), col(k0 + hd, k0 + 2 * hd), zeros(LANES - hd)]

    n_gate = 3 * NSA_REP
    misc = [col(1280, 1280 + n_gate), col(2840, 2856), zeros(LANES - n_gate - GLA_LOWRANK),
            col(1280 + n_gate, 1304), zeros(LANES - n_gate)]
    parts = ([col(0, 512), col(512, 640), col(640, 768)] + keys_padded(768) + keys_padded(1024)
             + [col(896, 1024), col(1152, 1280)] + misc
             + [col(1304, 1560), col(1560, 1816), col(1816, 2328), col(2328, 2840)])
    out = jnp.concatenate(parts, axis=1)
    assert out.shape[1] == D_IN_PAD
    return out


def _compress_weights(pe, w1, w2, group_stride):
    hd, hid = HEAD_DIM, CMP_HIDDEN
    pe2 = jnp.concatenate([pe, pe], axis=1)
    w1 = w1.reshape(CMP_BLOCK, hd, hid)
    z1 = jnp.zeros_like(w1)
    w1e = jnp.concatenate([jnp.concatenate([w1, z1], axis=2), jnp.concatenate([z1, w1], axis=2)], axis=1)
    w2e = jnp.zeros((NSA_GROUPS * hid, NSA_GROUPS * group_stride), w2.dtype)
    for g in range(NSA_GROUPS):
        w2e = w2e.at[g * hid:(g + 1) * hid, g * group_stride:g * group_stride + hd].set(w2)
    return pe2, w1e.astype(BF16), w2e.astype(BF16)


def kernel(x, ffn1_norm, ffn1_w_gate, ffn1_w_up, ffn1_w_down, mix_norm, w_in, nsa_pe_k, nsa_w1_k, nsa_w2_k,
           nsa_pe_v, nsa_w1_v, nsa_w2_v, gla_w_a2, gla_b_a, gla_norm, w_out, ffn2_norm, ffn2_w_gate,
           ffn2_w_up, ffn2_w_down, final_norm):
    bsz, seq, d = x.shape
    assert (seq, d) == (SEQ, D_MODEL) and ffn1_norm.shape[0] == 1
    n = bsz * seq
    xf = x.reshape(n, d)
    row = lambda v: v.reshape(1, -1).astype(F32)
    bf = lambda w: w.astype(BF16)
    ones = jnp.ones((1, d), F32)

    x1 = _ffn(xf, (), row(ffn1_norm[0]), bf(ffn1_w_gate[0]), bf(ffn1_w_up[0]), bf(ffn1_w_down[0]), ones,
              final_norm=False)

    q, kc, vc, ksa, kwa, vs, vw, misc, gq, gk, gv, gr = _inproj(x1, row(mix_norm[0]), _permute_w_in(w_in[0]))

    pek, w1k, w2k = _compress_weights(nsa_pe_k[0], nsa_w1_k[0], nsa_w2_k[0], LANES)
    pev, w1v, w2v = _compress_weights(nsa_pe_v[0], nsa_w1_v[0], nsa_w2_v[0], HEAD_DIM)
    kca, vc2 = _compress(kc, vc, pek, pev, w1k, w1v, w2k, w2v)

    o_nsa = _nsa(q, misc, kca, vc2, ksa, vs, kwa, vw)

    wa = jnp.zeros((LANES, GLA_HEADS * GLA_DK), F32).at[MISC_GA_OFF:MISC_GA_OFF + GLA_LOWRANK].set(gla_w_a2[0])
    o_gla = _gla(gq, gk, gv, gr, misc, wa, row(gla_b_a[0]), row(gla_norm[0]))

    d_nsa = NSA_HEADS * HEAD_DIM
    w_o = bf(w_out[0])
    out = _ffn(x1, (o_nsa, o_gla, w_o[:d_nsa], w_o[d_nsa:]), row(ffn2_norm[0]), bf(ffn2_w_gate[0]),
               bf(ffn2_w_up[0]), bf(ffn2_w_down[0]), row(final_norm), final_norm=True)
    return out.reshape(bsz, seq, d)
```

```python
import functools

import jax
import jax.numpy as jnp
from jax import lax
from jax.experimental import pallas as pl
from jax.experimental.pallas import tpu as pltpu

F32 = jnp.float32
BF16 = jnp.bfloat16
I32 = jnp.int32

D_MODEL = 1024
SEQ = 2048
D_FF = 2816
EPS = 1e-6
NEG_INF = -1e30
FORCE = 1e9

NSA_HEADS = 8
NSA_GROUPS = 2
NSA_REP = NSA_HEADS // NSA_GROUPS
HEAD_DIM = 64
CMP_BLOCK = 32
CMP_STRIDE = 16
CMP_HIDDEN = 128
N_CMP_PAD = SEQ // CMP_STRIDE
SEL_BLOCK = 64
SEL_SHIFT = 6
assert 1 << SEL_SHIFT == SEL_BLOCK
N_SEL = SEQ // SEL_BLOCK
SEL_TOPK = 8
WINDOW = 512
Q_BLOCK = 256
N_QB = SEQ // Q_BLOCK
KV_CHUNK = Q_BLOCK
SEL_ROWS = 1024

GLA_HEADS = 4
GLA_DK = 64
GLA_DV = 128
GLA_CHUNK = 64
GLA_CHUNK_SHIFT = 6
assert 1 << GLA_CHUNK_SHIFT == GLA_CHUNK
GLA_BLOCK = 256
GLA_LOWRANK = 16
GLA_TAU = 16.0

LANES = 128
V7X_VMEM_BYTES = 64 * 1024 * 1024

C_Q = (0, 512)
C_KC = (512, 640)
C_VC = (640, 768)
C_KSW = (768, 1024)
C_VS = (1024, 1152)
C_VW = (1152, 1280)
C_MISC = (1280, 1536)
C_GQ = (1536, 1792)
C_GK = (1792, 2048)
C_GV = (2048, 2560)
C_GR = (2560, 3072)
D_IN_PAD = 3072
KEY_OUT_WIDTH = NSA_GROUPS * LANES
MISC_GA_OFF = 12

X_SEL = HEAD_DIM
X_HI = HEAD_DIM + N_SEL
X_LO = X_HI + 1
POS_SPLIT = 64
POS_SHIFT = 6
assert 1 << POS_SHIFT == POS_SPLIT and X_LO < LANES


def _params(sem, vmem_bytes):
    return pltpu.CompilerParams(dimension_semantics=sem,
                                vmem_limit_bytes=min(int(vmem_bytes), V7X_VMEM_BYTES - (8 << 20)))


def _rms(x, g):
    return x * lax.rsqrt(jnp.mean(x * x, axis=-1, keepdims=True) + EPS) * g


def _dot(a, b, **kw):
    return jnp.dot(a, b, preferred_element_type=F32, **kw)


def _dot_nt(a, b, **kw):
    return lax.dot_general(a, b, (((1,), (1,)), ((), ())), preferred_element_type=F32, **kw)


def _dot_3pass(a, b):
    a_hi, b_hi = a.astype(BF16), b.astype(BF16)
    a_lo = (a - a_hi.astype(F32)).astype(BF16)
    b_lo = (b - b_hi.astype(F32)).astype(BF16)
    return _dot(a_hi, b_hi) + _dot(a_lo, b_hi) + _dot(a_hi, b_lo)


def _dot_tn(a, b, **kw):
    return lax.dot_general(a, b, (((0,), (0,)), ((), ())), preferred_element_type=F32, **kw)


def _ffn_kernel(*refs, tf, final_norm, mixer_out):
    if mixer_out:
        x_ref, a_ref, b_ref, wa_ref, wb_ref, g_ref, wg_ref, wu_ref, wd_ref, fg_ref, o_ref, act_ref = refs
        x = x_ref[...] + _dot(a_ref[...], wa_ref[...]) + _dot(b_ref[...], wb_ref[...])
    else:
        x_ref, g_ref, wg_ref, wu_ref, wd_ref, fg_ref, o_ref, act_ref = refs
        x = x_ref[...]
    h = _rms(x, g_ref[...]).astype(BF16)
    for c in range(D_FF // tf):
        cols = slice(c * tf, (c + 1) * tf)
        gate = _dot(h, wg_ref[:, cols])
        up = _dot(h, wu_ref[:, cols])
        act_ref[:, cols] = ((gate * jax.nn.sigmoid(gate)) * up).astype(BF16)
    y = x + 0.5 * _dot(act_ref[...], wd_ref[...])
    if final_norm:
        y = _rms(y, fg_ref[...])
    o_ref[...] = y


def _ffn(x, mixer, gain, wg, wu, wd, final_gain, *, final_norm, tm=512, tf=256):
    n, d = x.shape
    assert D_FF % tf == 0
    resident = lambda shape: pl.BlockSpec(shape, lambda i: (0, 0), pipeline_mode=pl.Buffered(1))
    rows = lambda width: pl.BlockSpec((tm, width), lambda i: (i, 0))
    vmem = 2 * 2 * tm * d * 4 + 3 * d * D_FF * 2 + tm * D_FF * 2 + tm * d * 2 + 4 * tm * tf * 4 + (8 << 20)
    mix_specs = []
    if mixer:
        a, b, w_a, w_b = mixer
        mix_specs = [rows(a.shape[1]), rows(b.shape[1]), resident(w_a.shape), resident(w_b.shape)]
        vmem += 2 * tm * (a.shape[1] + b.shape[1]) * 2 + (w_a.size + w_b.size) * 2
    return pl.pallas_call(
        functools.partial(_ffn_kernel, tf=tf, final_norm=final_norm, mixer_out=bool(mixer)),
        grid=(n // tm,),
        in_specs=[rows(d)] + mix_specs + [
            resident((1, d)),
            resident((d, D_FF)),
            resident((d, D_FF)),
            resident((D_FF, d)),
            resident((1, d)),
        ],
        out_specs=rows(d),
        out_shape=jax.ShapeDtypeStruct((n, d), F32),
        scratch_shapes=[pltpu.VMEM((tm, D_FF), BF16)],
        compiler_params=_params(("parallel",), vmem),
        name="ffn",
    )(x, *mixer, gain, wg, wu, wd, final_gain)


def _key_extras(pos, lane, with_block):
    lg = lane & (LANES - 1)
    ext = jnp.where(lg == X_HI, (pos >> POS_SHIFT).astype(F32),
                    jnp.where(lg == X_LO, (pos & (POS_SPLIT - 1)).astype(F32), 0.0))
    if with_block:
        ext = jnp.where((lg >= X_SEL) & (lg < X_HI) & ((pos >> SEL_SHIFT) == lg - X_SEL), 1.0, ext)
    return ext


def _inproj_kernel(x_ref, g_ref, w_ref, q_ref, kc_ref, vc_ref, ksa_ref, kwa_ref, vs_ref, vw_ref, misc_ref,
                   gq_ref, gk_ref, gv_ref, gr_ref, *, tm):
    h = _rms(x_ref[...], g_ref[...]).astype(BF16)

    def proj(c):
        return _dot(h, w_ref[:, c[0]:c[1]])

    shape = (tm, 2 * LANES)
    lane = lax.broadcasted_iota(I32, shape, 1)
    pos = (pl.program_id(0) * tm + lax.broadcasted_iota(I32, shape, 0)) & (SEQ - 1)
    is_key = (lane & (LANES - 1)) < HEAD_DIM

    q_ref[...] = (proj(C_Q) * (HEAD_DIM ** -0.5)).astype(BF16)
    assert C_KC[1] == C_VC[0] and C_VS[1] == C_VW[0]
    kvc = proj((C_KC[0], C_VC[1]))
    kc_ref[...] = kvc[:, :LANES]
    vc_ref[...] = kvc[:, LANES:]
    ksw = proj(C_KSW)
    kws = jnp.concatenate([pltpu.roll(ksw[:, g * LANES:(g + 1) * LANES], HEAD_DIM, axis=1)
                           for g in range(NSA_GROUPS)], axis=1)
    ksa_ref[...] = jnp.where(is_key, ksw, _key_extras(pos, lane, True)).astype(BF16)
    kwa_ref[...] = jnp.where(is_key, kws, _key_extras(pos, lane, False)).astype(BF16)
    vsw = proj((C_VS[0], C_VW[1])).astype(BF16)
    vs_ref[...] = vsw[:, :LANES]
    vw_ref[...] = vsw[:, LANES:]
    misc_ref[...] = proj(C_MISC)
    gq_ref[...] = proj(C_GQ)
    gk_ref[...] = proj(C_GK)
    gv_ref[...] = proj(C_GV).astype(BF16)
    gr_ref[...] = proj(C_GR)


def _inproj(x, gain, w, tm=512):
    n, d = x.shape
    assert SEQ % tm == 0 and SEQ & (SEQ - 1) == 0
    width = lambda c: c[1] - c[0]
    outs = [(width(C_Q), BF16), (width(C_KC), F32), (width(C_VC), F32), (KEY_OUT_WIDTH, BF16),
            (KEY_OUT_WIDTH, BF16), (width(C_VS), BF16), (width(C_VW), BF16), (width(C_MISC), F32),
            (width(C_GQ), F32), (width(C_GK), F32), (width(C_GV), BF16), (width(C_GR), F32)]
    vmem = 2 * tm * d * 4 + 2 * d * D_IN_PAD * 2 + 2 * tm * (D_IN_PAD + KEY_OUT_WIDTH) * 4 + (8 << 20)
    return pl.pallas_call(
        functools.partial(_inproj_kernel, tm=tm),
        grid=(n // tm,),
        in_specs=[
            pl.BlockSpec((tm, d), lambda i: (i, 0)),
            pl.BlockSpec((1, d), lambda i: (0, 0)),
            pl.BlockSpec((d, D_IN_PAD), lambda i: (0, 0)),
        ],
        out_specs=[pl.BlockSpec((tm, w_out), lambda i: (i, 0)) for w_out, _ in outs],
        out_shape=[jax.ShapeDtypeStruct((n, w_out), dt) for w_out, dt in outs],
        compiler_params=_params(("parallel",), vmem),
        name="inproj",
    )(x, gain, w)


def _compress_kernel(kc_ref, vc_ref, pek_ref, pev_ref, w1k_ref, w1v_ref, w2k_ref, w2v_ref, kca_ref, vc2_ref):
    half = CMP_BLOCK // 2

    def hidden(x_ref, pe_ref, w1_ref):
        acc_a = jnp.zeros((N_CMP_PAD, 2 * CMP_HIDDEN), F32)
        acc_b = jnp.zeros((N_CMP_PAD, 2 * CMP_HIDDEN), F32)
        for l in range(half):
            rows = x_ref[pl.ds(l, N_CMP_PAD, stride=CMP_STRIDE), :]
            acc_a += _dot((rows + pe_ref[l:l + 1, :]).astype(BF16), w1_ref[l])
            acc_b += _dot((rows + pe_ref[half + l:half + l + 1, :]).astype(BF16), w1_ref[half + l])
        pre = acc_a + pltpu.roll(acc_b, N_CMP_PAD - 1, axis=0)
        return jax.nn.gelu(pre, approximate=True).astype(BF16)

    kc = _dot(hidden(kc_ref, pek_ref, w1k_ref), w2k_ref[...])
    vc = _dot(hidden(vc_ref, pev_ref, w1v_ref), w2v_ref[...])
    lane = lax.broadcasted_iota(I32, kc.shape, 1)
    row = lax.broadcasted_iota(I32, kc.shape, 0)
    kc = jnp.where((lane & (LANES - 1)) < HEAD_DIM, kc, _key_extras(2 * CMP_STRIDE * row + CMP_BLOCK - 1, lane, False))
    kca_ref[0] = jnp.where(row < N_CMP_PAD - 1, kc, 0.0).astype(BF16)
    vc2_ref[0] = jnp.where(row[:, :LANES] < N_CMP_PAD - 1, vc, 0.0).astype(BF16)


def _compress(kc, vc, pek, pev, w1k, w1v, w2k, w2v):
    n = kc.shape[0]
    b = n // SEQ
    full = lambda a: pl.BlockSpec(a.shape, lambda i: (0,) * a.ndim)
    return pl.pallas_call(
        _compress_kernel,
        grid=(b,),
        in_specs=[
            pl.BlockSpec((SEQ, LANES), lambda i: (i, 0)),
            pl.BlockSpec((SEQ, LANES), lambda i: (i, 0)),
            full(pek), full(pev), full(w1k), full(w1v), full(w2k), full(w2v),
        ],
        out_specs=[pl.BlockSpec((1, N_CMP_PAD, 2 * LANES), lambda i: (i, 0, 0)),
                   pl.BlockSpec((1, N_CMP_PAD, LANES), lambda i: (i, 0, 0))],
        out_shape=[jax.ShapeDtypeStruct((b, N_CMP_PAD, 2 * LANES), BF16),
                   jax.ShapeDtypeStruct((b, N_CMP_PAD, LANES), BF16)],
        compiler_params=_params(("parallel",), 32 << 20),
        name="compress",
    )(kc, vc, pek, pev, w1k, w1v, w2k, w2v)


def _attend_pipelined(tasks):
    n = len(tasks)
    scores, probs, outs = [None] * n, [None] * n, [None] * n
    for i in range(n + 2):
        if i < n:
            q, k, _, _ = tasks[i]
            scores[i] = _dot_nt(q, k)
        if 1 <= i <= n:
            _, _, _, chunk_masks = tasks[i - 1]
            cols = []
            for c, allowed in enumerate(chunk_masks):
                s = scores[i - 1][:, c * KV_CHUNK:(c + 1) * KV_CHUNK]
                cols.append(s if allowed is None else jnp.where(allowed, s, NEG_INF))
            s = cols[0] if len(cols) == 1 else jnp.concatenate(cols, axis=1)
            p = jnp.exp(s - jnp.max(s, axis=-1, keepdims=True))
            probs[i - 1] = (p.astype(BF16), 1.0 / jnp.sum(p, axis=-1, keepdims=True))
            scores[i - 1] = None
        if 2 <= i:
            p, inv = probs[i - 2]
            outs[i - 2] = _dot(p, tasks[i - 2][2]) * inv
            probs[i - 2] = None
    return outs


def _head_queries(q_ref, g, slope_scale, sel_lanes):
    rows = q_ref.shape[0]
    lane = lax.broadcasted_iota(I32, (rows, LANES), 1)
    qf = q_ref[...].astype(F32)
    parts = []
    for r in range(NSA_REP):
        slope = slope_scale * jnp.where(g == 0, 2.0 ** -(r + 1), 2.0 ** -(r + 1 + NSA_REP)).astype(F32)
        tile = qf[:, (r // 2) * LANES:(r // 2 + 1) * LANES]
        if r % 2:
            tile = pltpu.roll(tile, HEAD_DIM, axis=1)
        ext = jnp.where(lane == X_HI, POS_SPLIT * slope, jnp.where(lane == X_LO, slope, sel_lanes))
        parts.append(jnp.where(lane < HEAD_DIM, tile, ext).astype(BF16))
    return parts


def _head_layout(heads, g):
    low = lax.broadcasted_iota(I32, heads[0].shape, 1) < HEAD_DIM
    tiles = []
    for pair in range(NSA_REP // 2):
        even, odd = heads[2 * pair], heads[2 * pair + 1]
        left = jnp.where(g == 0, even, pltpu.roll(even, HEAD_DIM, axis=1))
        right = jnp.where(g == 0, pltpu.roll(odd, HEAD_DIM, axis=1), odd)
        tiles.append(jnp.where(low, left, right))
    return tiles


def _nsa_select_kernel(q_ref, misc_ref, kca_ref, vc2_ref, ocmp_ref, drop_ref):
    g = pl.program_id(1)
    rows = SEL_ROWS
    t0 = pl.program_id(2) * rows
    lane = lax.broadcasted_iota(I32, (rows, LANES), 1)
    tq = t0 + lax.broadcasted_iota(I32, (rows, LANES), 0)

    q4 = jnp.concatenate(_head_queries(q_ref, g, 0.5, 0.0), axis=0)
    s_c = _dot_nt(q4, kca_ref[0])
    valid_c = lane * CMP_STRIDE + (CMP_BLOCK - 1) <= tq
    p_sum = jnp.zeros((rows, LANES), F32)
    p_parts = []
    for r in range(NSA_REP):
        s = jnp.where(valid_c, s_c[r * rows:(r + 1) * rows], NEG_INF)
        e = jnp.exp(s - jnp.max(s, axis=-1, keepdims=True))
        p = jnp.where(valid_c, e / jnp.sum(e, axis=-1, keepdims=True), 0.0)
        p_sum += p
        p_parts.append(p.astype(BF16))
    o_cmp = _dot(jnp.concatenate(p_parts, axis=0), vc2_ref[0])
    gates = jax.nn.sigmoid(misc_ref[...])
    tiles = _head_layout([gates[:, 3 * r:3 * r + 1] * o_cmp[r * rows:(r + 1) * rows] for r in range(NSA_REP)], g)
    for i, tile in enumerate(tiles):
        ocmp_ref[:, i * LANES:(i + 1) * LANES] = tile.astype(BF16)

    ov_j = lax.broadcasted_iota(I32, (LANES, LANES), 0)
    ov_i = lax.broadcasted_iota(I32, (LANES, LANES), 1)
    ov_t = ((ov_i < 4 * ov_j + 4) & (ov_i > 4 * ov_j - 2) & (ov_j < N_SEL)).astype(F32)
    imp_t = _dot_nt(ov_t, p_sum, precision=lax.Precision.HIGHEST)[:N_SEL]
    j_blk = lax.broadcasted_iota(I32, (N_SEL, rows), 0)
    cur = (t0 + lax.broadcasted_iota(I32, (N_SEL, rows), 1)) >> SEL_SHIFT
    forced = (j_blk == 0) | (j_blk == cur) | (j_blk == cur - 1)
    imp_t = jnp.where(forced, FORCE, imp_t)
    imp_t = jnp.where(j_blk <= cur, imp_t, -FORCE)
    rank = jnp.zeros((N_SEL, rows), I32)
    for jp in range(N_SEL):
        row = imp_t[jp:jp + 1, :]
        beats = (row > imp_t) | ((row == imp_t) & (j_blk > jp))
        rank += beats.astype(I32)
    drop_t = jnp.where(rank < SEL_TOPK, 0.0, NEG_INF)
    drop_t = jnp.concatenate([drop_t, jnp.zeros((LANES - N_SEL, rows), F32)], axis=0)
    drop_q = jnp.concatenate([drop_t[:, i * LANES:(i + 1) * LANES].T for i in range(rows // LANES)], axis=0)
    drop_ref[...] = pltpu.roll(drop_q, X_SEL, axis=1).astype(BF16)


def _nsa_kernel(q_ref, misc_ref, ocmp_ref, drop_ref, ksa_ref, vs_ref, kwa_ref, vw_ref, o_ref):
    for qb in range(N_QB):
        @pl.when(pl.program_id(2) == qb)
        def _(qb=qb):
            _nsa_body(qb, q_ref, misc_ref, ocmp_ref, drop_ref, ksa_ref, vs_ref, kwa_ref, vw_ref, o_ref)


def _nsa_body(qb, q_ref, misc_ref, ocmp_ref, drop_ref, ksa_ref, vs_ref, kwa_ref, vw_ref, o_ref):
    g = pl.program_id(1)
    t0 = qb * Q_BLOCK

    def dist_to(c):
        return (t0 + lax.broadcasted_iota(I32, (Q_BLOCK, KV_CHUNK), 0)
                - (c * KV_CHUNK + lax.broadcasted_iota(I32, (Q_BLOCK, KV_CHUNK), 1)))

    c_diag = qb
    dist_diag = dist_to(c_diag)
    n_slc = (c_diag + 1) * KV_CHUNK

    slc_masks = [None] * c_diag + [dist_diag >= 0]
    q_sel = _head_queries(q_ref, g, 1.0, drop_ref[...].astype(F32))

    c_first = max(c_diag - WINDOW // KV_CHUNK, 0)
    win_masks = [None] * (c_diag - c_first + 1)
    win_masks[-1] = dist_diag >= 0
    if c_diag - c_first == WINDOW // KV_CHUNK:
        win_masks[0] = dist_to(c_first) < WINDOW
    win_rows = slice(c_first * KV_CHUNK, n_slc)
    q_win = _head_queries(q_ref, g, 1.0, 0.0)

    both = lambda masks: [None if m is None else jnp.concatenate([m, m], axis=0) for m in masks]
    tasks = []
    for r in range(0, NSA_REP, 2):
        tasks.append((jnp.concatenate(q_win[r:r + 2], axis=0), kwa_ref[win_rows, :], vw_ref[win_rows, :],
                      both(win_masks)))
    for r in range(0, NSA_REP, 2):
        tasks.append((jnp.concatenate(q_sel[r:r + 2], axis=0), ksa_ref[:n_slc, :], vs_ref[:n_slc, :],
                      both(slc_masks)))
    outs = _attend_pipelined(tasks)
    split = lambda o: [o[:Q_BLOCK], o[Q_BLOCK:]]
    o_win = split(outs[0]) + split(outs[1])
    o_slc = split(outs[2]) + split(outs[3])

    gates = jax.nn.sigmoid(misc_ref[...])
    heads = [gates[:, 3 * r + 1:3 * r + 2] * o_slc[r] + gates[:, 3 * r + 2:3 * r + 3] * o_win[r]
             for r in range(NSA_REP)]
    for i, tile in enumerate(_head_layout(heads, g)):
        cols = slice(i * LANES, (i + 1) * LANES)
        o_ref[:, cols] = (tile + ocmp_ref[:, cols].astype(F32)).astype(BF16)


def _nsa(q, misc, kca, vc2, ksa, vs, kwa, vw):
    n = q.shape[0]
    b = n // SEQ
    gw = NSA_REP * HEAD_DIM
    assert WINDOW % KV_CHUNK == 0 and KV_CHUNK == Q_BLOCK and SEQ % SEL_ROWS == 0
    n_sel = SEQ // SEL_ROWS
    sel_rows = lambda width: pl.BlockSpec((SEL_ROWS, width), lambda i, g, j: (i * n_sel + j, g))
    ocmp, drop = pl.pallas_call(
        _nsa_select_kernel,
        grid=(b, NSA_GROUPS, n_sel),
        in_specs=[sel_rows(gw), sel_rows(LANES),
                  pl.BlockSpec((1, N_CMP_PAD, LANES), lambda i, g, j: (i, 0, g)),
                  pl.BlockSpec((1, N_CMP_PAD, LANES), lambda i, g, j: (i, 0, 0))],
        out_specs=[sel_rows(gw), sel_rows(LANES)],
        out_shape=[jax.ShapeDtypeStruct((n, NSA_GROUPS * gw), BF16),
                   jax.ShapeDtypeStruct((n, NSA_GROUPS * LANES), BF16)],
        compiler_params=_params(("parallel", "parallel", "parallel"), 32 << 20),
        name="nsa_select",
    )(q, misc, kca, vc2)

    q_rows = lambda width: pl.BlockSpec((Q_BLOCK, width), lambda i, g, c: (i * N_QB + c, g))
    return pl.pallas_call(
        _nsa_kernel,
        grid=(b, NSA_GROUPS, N_QB),
        in_specs=[q_rows(gw), q_rows(LANES), q_rows(gw), q_rows(LANES),
                  pl.BlockSpec((SEQ, LANES), lambda i, g, c: (i, g)),
                  pl.BlockSpec((SEQ, LANES), lambda i, g, c: (i, 0)),
                  pl.BlockSpec((SEQ, LANES), lambda i, g, c: (i, g)),
                  pl.BlockSpec((SEQ, LANES), lambda i, g, c: (i, 0))],
        out_specs=q_rows(gw),
        out_shape=jax.ShapeDtypeStruct((n, NSA_GROUPS * gw), BF16),
        compiler_params=_params(("parallel", "parallel", "arbitrary"), 48 << 20),
        name="nsa",
    )(q, misc, ocmp, drop, ksa, vs, kwa, vw)


def _gla_kernel(q_ref, k_ref, v_ref, r_ref, misc_ref, wa_ref, ba_ref, gn_ref, o_ref):
    c, blk = GLA_CHUNK, GLA_BLOCK
    n_c = blk // c
    lane = lax.broadcasted_iota(I32, (blk, LANES), 1)
    row = lax.broadcasted_iota(I32, (blk, LANES), 0)
    in_chunk = row & (c - 1)
    chunk_of_row = row >> GLA_CHUNK_SHIFT
    r2 = lax.broadcasted_iota(I32, (2 * blk, blk), 0) & (blk - 1)
    c2 = lax.broadcasted_iota(I32, (2 * blk, blk), 1)
    intra = (r2 >= c2) & ((r2 >> GLA_CHUNK_SHIFT) == (c2 >> GLA_CHUNK_SHIFT))
    lane_s = lax.broadcasted_iota(I32, (GLA_DV, LANES), 1)
    n_blk = SEQ // blk

    def prep(i_blk):
        rows = slice(i_blk * blk, (i_blk + 1) * blk)
        la = _dot_3pass(misc_ref[rows, :], wa_ref[...]) + ba_ref[...]
        b = (jnp.minimum(la, 0.0) - jnp.log(1.0 + jnp.exp(-jnp.abs(la)))) * (1.0 / GLA_TAU)
        shift = 1
        while shift < c:
            b = b + jnp.where(in_chunk >= shift, pltpu.roll(b, shift, axis=0), 0.0)
            shift *= 2
        b3 = b.reshape(n_c, c, LANES)
        b_last = b3[:, c - 1:c, :]
        k = k_ref[rows, :]
        q_in = q_ref[rows, :] * (GLA_DK ** -0.5) * jnp.exp(b)
        k_in = (k * jnp.exp(-b)).astype(BF16)
        k_st = (k.reshape(n_c, c, LANES) * jnp.exp(b_last - b3)).reshape(blk, LANES)
        decay = jnp.exp(b_last)
        v = v_ref[rows, :]

        q2 = jnp.concatenate([jnp.where(lane < GLA_DK, q_in, 0.0), jnp.where(lane >= GLA_DK, q_in, 0.0)],
                             axis=0).astype(BF16)
        k_cols = jnp.concatenate([jnp.where(chunk_of_row == i_c, k_st, 0.0) for i_c in range(n_c)], axis=1)
        return q2, k_in, k_cols.astype(BF16), v, decay

    def chunk_parallel(q2, k_in, k_cols, v, decay):
        a = jnp.where(intra, _dot_nt(q2, k_in), 0.0).astype(BF16)
        o_intra = _dot(a, v)
        inc = _dot_tn(v, k_cols)
        return q2, o_intra, inc, decay

    per_blk = []
    ready = prep(0)
    for i_blk in range(n_blk):
        upcoming = prep(i_blk + 1) if i_blk + 1 < n_blk else None
        per_blk.append(chunk_parallel(*ready))
        ready = upcoming

    st = jnp.zeros((GLA_DV, LANES), F32)
    state_before = []
    for _, _, inc, decay in per_blk:
        for i_c in range(n_c):
            state_before.append(st.astype(BF16))
            cols = slice(i_c * LANES, (i_c + 1) * LANES)
            st = st * decay[i_c] + jnp.where(lane_s < GLA_DK, inc[:GLA_DV, cols], inc[GLA_DV:, cols])

    for i_blk, (q2, o_intra, _, _) in enumerate(per_blk):
        rows = slice(i_blk * blk, (i_blk + 1) * blk)
        o_inter = []
        for i_c in range(n_c):
            q_c = jnp.concatenate([q2[i_c * c:(i_c + 1) * c], q2[blk + i_c * c:blk + (i_c + 1) * c]], axis=0)
            o_inter.append(_dot_nt(q_c, state_before[i_blk * n_c + i_c]))
        for h in range(2):
            cols = slice(h * GLA_DV, (h + 1) * GLA_DV)
            o = o_intra[h * blk:(h + 1) * blk, cols] + jnp.concatenate(
                [o_inter[i_c][h * c:(h + 1) * c] for i_c in range(n_c)], axis=0)
            gate = r_ref[rows, cols]
            o_ref[rows, cols] = (_rms(o, gn_ref[...]) * (gate * jax.nn.sigmoid(gate))).astype(BF16)


def _gla(gq, gk, gv, gr, misc, wa, ba, gn):
    n = gq.shape[0]
    b = n // SEQ
    return pl.pallas_call(
        _gla_kernel,
        grid=(b, GLA_HEADS // 2),
        in_specs=[
            pl.BlockSpec((SEQ, LANES), lambda i, p: (i, p)),
            pl.BlockSpec((SEQ, LANES), lambda i, p: (i, p)),
            pl.BlockSpec((SEQ, 2 * GLA_DV), lambda i, p: (i, p)),
            pl.BlockSpec((SEQ, 2 * GLA_DV), lambda i, p: (i, p)),
            pl.BlockSpec((SEQ, LANES), lambda i, p: (i, 0)),
            pl.BlockSpec((LANES, LANES), lambda i, p: (0, p)),
            pl.BlockSpec((1, LANES), lambda i, p: (0, p)),
            pl.BlockSpec((1, GLA_DV), lambda i, p: (0, 0)),
        ],
        out_specs=pl.BlockSpec((SEQ, 2 * GLA_DV), lambda i, p: (i, p)),
        out_shape=jax.ShapeDtypeStruct((n, GLA_HEADS * GLA_DV), BF16),
        compiler_params=_params(("parallel", "parallel"), 40 << 20),
        name="gla",
    )(gq, gk, gv, gr, misc, wa, ba, gn)


def _permute_w_in(w):
    w = w.astype(BF16)
    d = w.shape[0]
    col = lambda lo, hi: w[:, lo:hi]
    zeros = lambda k: jnp.zeros((d, k), w.dtype)
    hd = HEAD_DIM

    keys = [col(768, 768 + hd), col(1024, 1024 + hd), col(768 + hd, 768 + 2 * hd), col(1024 + hd, 1024 + 2 * hd)]
    n_gate = 3 * NSA_REP
    misc = [col(1280, 1280 + n_gate), col(2840, 2856), zeros(LANES - n_gate - GLA_LOWRANK),
            col(1280 + n_gate, 1304), zeros(LANES - n_gate)]
    parts = ([col(0, 512), col(512, 640), col(640, 768)] + keys
             + [col(896, 1024), col(1152, 1280)] + misc
             + [col(1304, 1560), col(1560, 1816), col(1816, 2328), col(2328, 2840)])
    out = jnp.concatenate(parts, axis=1)
    assert out.shape[1] == D_IN_PAD
    return out


def _compress_weights(pe, w1, w2, group_stride):
    hd, hid = HEAD_DIM, CMP_HIDDEN
    pe2 = jnp.concatenate([pe, pe], axis=1)
    w1 = w1.reshape(CMP_BLOCK, hd, hid)
    z1 = jnp.zeros_like(w1)
    w1e = jnp.concatenate([jnp.concatenate([w1, z1], axis=2), jnp.concatenate([z1, w1], axis=2)], axis=1)
    w2e = jnp.zeros((NSA_GROUPS * hid, NSA_GROUPS * group_stride), w2.dtype)
    for g in range(NSA_GROUPS):
        w2e = w2e.at[g * hid:(g + 1) * hid, g * group_stride:g * group_stride + hd].set(w2)
    return pe2, w1e.astype(BF16), w2e.astype(BF16)


def kernel(x, ffn1_norm, ffn1_w_gate, ffn1_w_up, ffn1_w_down, mix_norm, w_in, nsa_pe_k, nsa_w1_k, nsa_w2_k,
           nsa_pe_v, nsa_w1_v, nsa_w2_v, gla_w_a2, gla_b_a, gla_norm, w_out, ffn2_norm, ffn2_w_gate,
           ffn2_w_up, ffn2_w_down, final_norm):
    bsz, seq, d = x.shape
    assert (seq, d) == (SEQ, D_MODEL) and ffn1_norm.shape[0] == 1
    n = bsz * seq
    xf = x.reshape(n, d)
    row = lambda v: v.reshape(1, -1).astype(F32)
    bf = lambda w: w.astype(BF16)
    ones = jnp.ones((1, d), F32)

    x1 = _ffn(xf, (), row(ffn1_norm[0]), bf(ffn1_w_gate[0]), bf(ffn1_w_up[0]), bf(ffn1_w_down[0]), ones,
              final_norm=False)

    q, kc, vc, ksa, kwa, vs, vw, misc, gq, gk, gv, gr = _inproj(x1, row(mix_norm[0]), _permute_w_in(w_in[0]))

    pek, w1k, w2k = _compress_weights(nsa_pe_k[0], nsa_w1_k[0], nsa_w2_k[0], LANES)
    pev, w1v, w2v = _compress_weights(nsa_pe_v[0], nsa_w1_v[0], nsa_w2_v[0], HEAD_DIM)
    kca, vc2 = _compress(kc, vc, pek, pev, w1k, w1v, w2k, w2v)

    o_nsa = _nsa(q, misc, kca, vc2, ksa, vs, kwa, vw)

    wa = jnp.zeros((LANES, GLA_HEADS * GLA_DK), F32).at[MISC_GA_OFF:MISC_GA_OFF + GLA_LOWRANK].set(gla_w_a2[0])
    o_gla = _gla(gq, gk, gv, gr, misc, wa, row(gla_b_a[0]), row(gla_norm[0]))

    d_nsa = NSA_HEADS * HEAD_DIM
    w_o = bf(w_out[0])
    out = _ffn(x1, (o_nsa, o_gla, w_o[:d_nsa], w_o[d_nsa:]), row(ffn2_norm[0]), bf(ffn2_w_gate[0]),
               bf(ffn2_w_up[0]), bf(ffn2_w_down[0]), row(final_norm), final_norm=True)
    return out.reshape(bsz, seq, d)
```

```python
import functools

import jax
import jax.numpy as jnp
from jax import lax
from jax.experimental import pallas as pl
from jax.experimental.pallas import tpu as pltpu

F32 = jnp.float32
BF16 = jnp.bfloat16
I32 = jnp.int32

D_MODEL = 1024
SEQ = 2048
D_FF = 2816
EPS = 1e-6
NEG_INF = -1e30
FORCE = 1e9

NSA_HEADS = 8
NSA_GROUPS = 2
NSA_REP = NSA_HEADS // NSA_GROUPS
HEAD_DIM = 64
CMP_BLOCK = 32
CMP_STRIDE = 16
CMP_HIDDEN = 128
N_CMP_PAD = SEQ // CMP_STRIDE
SEL_BLOCK = 64
SEL_SHIFT = 6
assert 1 << SEL_SHIFT == SEL_BLOCK
N_SEL = SEQ // SEL_BLOCK
SEL_TOPK = 8
WINDOW = 512
Q_BLOCK = 256
N_QB = SEQ // Q_BLOCK
KV_CHUNK = Q_BLOCK
SEL_ROWS = 1024

GLA_HEADS = 4
GLA_DK = 64
GLA_DV = 128
GLA_CHUNK = 64
GLA_CHUNK_SHIFT = 6
assert 1 << GLA_CHUNK_SHIFT == GLA_CHUNK
GLA_BLOCK = 256
GLA_LOWRANK = 16
GLA_TAU = 16.0

LANES = 128
V7X_VMEM_BYTES = 64 * 1024 * 1024

C_Q = (0, 512)
C_KC = (512, 640)
C_VC = (640, 768)
C_KSW = (768, 1024)
C_VS = (1024, 1152)
C_VW = (1152, 1280)
C_MISC = (1280, 1536)
C_GQ = (1536, 1792)
C_GK = (1792, 2048)
C_GV = (2048, 2560)
C_GR = (2560, 3072)
D_IN_PAD = 3072
KEY_OUT_WIDTH = NSA_GROUPS * LANES
MISC_GA_OFF = 12

LOG2E = 1.4426950408889634
X_SEL = HEAD_DIM
X_POS = HEAD_DIM + N_SEL
N_POS_TERMS = 3
assert X_POS + N_POS_TERMS <= LANES


def _params(sem, vmem_bytes):
    return pltpu.CompilerParams(dimension_semantics=sem,
                                vmem_limit_bytes=min(int(vmem_bytes), V7X_VMEM_BYTES - (8 << 20)))


def _rms(x, g):
    return x * lax.rsqrt(jnp.mean(x * x, axis=-1, keepdims=True) + EPS) * g


def _dot(a, b, **kw):
    return jnp.dot(a, b, preferred_element_type=F32, **kw)


def _dot_nt(a, b, **kw):
    return lax.dot_general(a, b, (((1,), (1,)), ((), ())), preferred_element_type=F32, **kw)


def _dot_3pass(a, b):
    a_hi, b_hi = a.astype(BF16), b.astype(BF16)
    a_lo = (a - a_hi.astype(F32)).astype(BF16)
    b_lo = (b - b_hi.astype(F32)).astype(BF16)
    return _dot(a_hi, b_hi) + _dot(a_lo, b_hi) + _dot(a_hi, b_lo)


def _dot_tn(a, b, **kw):
    return lax.dot_general(a, b, (((0,), (0,)), ((), ())), preferred_element_type=F32, **kw)


def _ffn_kernel(*refs, tf, final_norm, mixer_out):
    if mixer_out:
        x_ref, a_ref, b_ref, wa_ref, wb_ref, g_ref, wg_ref, wu_ref, wd_ref, fg_ref, o_ref, act_ref = refs
        x = x_ref[...] + _dot(a_ref[...], wa_ref[...]) + _dot(b_ref[...], wb_ref[...])
    else:
        x_ref, g_ref, wg_ref, wu_ref, wd_ref, fg_ref, o_ref, act_ref = refs
        x = x_ref[...]
    h = _rms(x, g_ref[...]).astype(BF16)
    for c in range(D_FF // tf):
        cols = slice(c * tf, (c + 1) * tf)
        gate = _dot(h, wg_ref[:, cols])
        up = _dot(h, wu_ref[:, cols])
        act_ref[:, cols] = ((gate * jax.nn.sigmoid(gate)) * up).astype(BF16)
    y = x + 0.5 * _dot(act_ref[...], wd_ref[...])
    if final_norm:
        y = _rms(y, fg_ref[...])
    o_ref[...] = y


def _ffn(x, mixer, gain, wg, wu, wd, final_gain, *, final_norm, tm=512, tf=256):
    n, d = x.shape
    assert D_FF % tf == 0
    resident = lambda shape: pl.BlockSpec(shape, lambda i: (0, 0), pipeline_mode=pl.Buffered(1))
    rows = lambda width: pl.BlockSpec((tm, width), lambda i: (i, 0))
    vmem = 2 * 2 * tm * d * 4 + 3 * d * D_FF * 2 + tm * D_FF * 2 + tm * d * 2 + 4 * tm * tf * 4 + (8 << 20)
    mix_specs = []
    if mixer:
        a, b, w_a, w_b = mixer
        mix_specs = [rows(a.shape[1]), rows(b.shape[1]), resident(w_a.shape), resident(w_b.shape)]
        vmem += 2 * tm * (a.shape[1] + b.shape[1]) * 2 + (w_a.size + w_b.size) * 2
    return pl.pallas_call(
        functools.partial(_ffn_kernel, tf=tf, final_norm=final_norm, mixer_out=bool(mixer)),
        grid=(n // tm,),
        in_specs=[rows(d)] + mix_specs + [
            resident((1, d)),
            resident((d, D_FF)),
            resident((d, D_FF)),
            resident((D_FF, d)),
            resident((1, d)),
        ],
        out_specs=rows(d),
        out_shape=jax.ShapeDtypeStruct((n, d), F32),
        scratch_shapes=[pltpu.VMEM((tm, D_FF), BF16)],
        compiler_params=_params(("parallel",), vmem),
        name="ffn",
    )(x, *mixer, gain, wg, wu, wd, final_gain)


def _key_extras(pos, lane, with_block):
    lg = lane & (LANES - 1)
    rest = pos.astype(F32) * LOG2E
    ext = jnp.zeros(rest.shape, F32)
    for i in range(N_POS_TERMS):
        term = rest.astype(BF16).astype(F32) if i + 1 < N_POS_TERMS else rest
        ext = jnp.where(lg == X_POS + i, term, ext)
        rest = rest - term
    if with_block:
        ext = jnp.where((lg >= X_SEL) & (lg < X_POS) & ((pos >> SEL_SHIFT) == lg - X_SEL), 1.0, ext)
    return ext


def _w_in_pieces():
    hd, n_gate = HEAD_DIM, 3 * NSA_REP
    src = lambda lo, hi: ((lo, hi), hi - lo)
    pad = lambda k: (None, k)
    pieces = [src(0, 512), src(512, 640), src(640, 768),
              src(768, 768 + hd), src(1024, 1024 + hd), src(768 + hd, 768 + 2 * hd), src(1024 + hd, 1024 + 2 * hd),
              src(896, 1024), src(1152, 1280),
              src(1280, 1280 + n_gate), src(2840, 2856), pad(LANES - n_gate - GLA_LOWRANK),
              src(1280 + n_gate, 1304), pad(LANES - n_gate),
              src(1304, 1560), src(1560, 1816), src(1816, 2328), src(2328, 2840)]
    assert sum(k for _, k in pieces) == D_IN_PAD
    return pieces


def _stage_w_in(w_ref, wp_ref):
    n_src = w_ref.shape[1]
    tiles, cur, room = [], [], LANES
    for rng, k in _w_in_pieces():
        lo = rng[0] if rng else None
        while k:
            take = min(k, room)
            cur.append((lo, take))
            lo = None if lo is None else lo + take
            k, room = k - take, room - take
            if room == 0:
                tiles.append(cur)
                cur, room = [], LANES
    for t, parts in enumerate(tiles):
        vals = []
        for lo, k in parts:
            if lo is None:
                vals.append(jnp.zeros((w_ref.shape[0], k), F32))
                continue
            a_lo = lo // LANES * LANES
            a_hi = min(-(-(lo + k) // LANES) * LANES, n_src)
            vals.append(w_ref[:, a_lo:a_hi][:, lo - a_lo:lo - a_lo + k])
        tile = vals[0] if len(vals) == 1 else jnp.concatenate(vals, axis=1)
        wp_ref[:, t * LANES:(t + 1) * LANES] = tile.astype(BF16)


def _inproj_kernel(x_ref, g_ref, w_ref, q_ref, kvc_ref, ksa_ref, kwa_ref, vsw_ref, misc_ref,
                   gq_ref, gk_ref, gv_ref, gr_ref, wp_ref, *, tm):
    @pl.when(pl.program_id(0) == 0)
    def _():
        _stage_w_in(w_ref, wp_ref)

    h = _rms(x_ref[...], g_ref[...]).astype(BF16)

    def proj(c):
        return _dot(h, wp_ref[:, c[0]:c[1]])

    shape = (tm, 2 * LANES)
    lane = lax.broadcasted_iota(I32, shape, 1)
    pos = (pl.program_id(0) * tm + lax.broadcasted_iota(I32, shape, 0)) & (SEQ - 1)
    is_key = (lane & (LANES - 1)) < HEAD_DIM

    q_ref[...] = (proj(C_Q) * (HEAD_DIM ** -0.5 * LOG2E)).astype(BF16)
    assert C_KC[1] == C_VC[0] and C_VS[1] == C_VW[0]
    kvc_ref[...] = proj((C_KC[0], C_VC[1]))
    ksw = proj(C_KSW)
    kws = jnp.concatenate([pltpu.roll(ksw[:, g * LANES:(g + 1) * LANES], HEAD_DIM, axis=1)
                           for g in range(NSA_GROUPS)], axis=1)
    ksa_ref[...] = jnp.where(is_key, ksw, _key_extras(pos, lane, True)).astype(BF16)
    kwa_ref[...] = jnp.where(is_key, kws, _key_extras(pos, lane, False)).astype(BF16)
    vsw_ref[...] = proj((C_VS[0], C_VW[1])).astype(BF16)
    misc_ref[...] = proj(C_MISC)
    gq_ref[...] = proj(C_GQ)
    gk_ref[...] = proj(C_GK)
    gv_ref[...] = proj(C_GV).astype(BF16)
    gr_ref[...] = proj(C_GR)


def _inproj(x, gain, w, tm=512):
    n, d = x.shape
    assert SEQ % tm == 0 and SEQ & (SEQ - 1) == 0
    width = lambda c: c[1] - c[0]
    outs = [(width(C_Q), BF16), (width(C_KC) + width(C_VC), F32), (KEY_OUT_WIDTH, BF16), (KEY_OUT_WIDTH, BF16),
            (width(C_VS) + width(C_VW), BF16), (width(C_MISC), F32),
            (width(C_GQ), F32), (width(C_GK), F32), (width(C_GV), BF16), (width(C_GR), F32)]
    vmem = (2 * tm * d * 4 + w.size * 4 + d * D_IN_PAD * 2 + 2 * tm * (D_IN_PAD + KEY_OUT_WIDTH) * 4
            + 4 * d * LANES * 4 + (8 << 20))
    return pl.pallas_call(
        functools.partial(_inproj_kernel, tm=tm),
        grid=(n // tm,),
        in_specs=[
            pl.BlockSpec((tm, d), lambda i: (i, 0)),
            pl.BlockSpec((1, d), lambda i: (0, 0)),
            pl.BlockSpec(w.shape, lambda i: (0, 0), pipeline_mode=pl.Buffered(1)),
        ],
        out_specs=[pl.BlockSpec((tm, w_out), lambda i: (i, 0)) for w_out, _ in outs],
        out_shape=[jax.ShapeDtypeStruct((n, w_out), dt) for w_out, dt in outs],
        scratch_shapes=[pltpu.VMEM((d, D_IN_PAD), BF16)],
        compiler_params=_params(("arbitrary",), vmem),
        name="inproj",
    )(x, gain, w)


def _compress_kernel(kc_ref, vc_ref, pek_ref, pev_ref, w1k_ref, w1v_ref, w2k_ref, w2v_ref, kca_ref, vc2_ref):
    half = CMP_BLOCK // 2

    def hidden(x_ref, pe_ref, w1_ref):
        acc_a = jnp.zeros((N_CMP_PAD, 2 * CMP_HIDDEN), F32)
        acc_b = jnp.zeros((N_CMP_PAD, 2 * CMP_HIDDEN), F32)
        for l in range(half):
            rows = x_ref[pl.ds(l, N_CMP_PAD, stride=CMP_STRIDE), :]
            acc_a += _dot((rows + pe_ref[l:l + 1, :]).astype(BF16), w1_ref[l])
            acc_b += _dot((rows + pe_ref[half + l:half + l + 1, :]).astype(BF16), w1_ref[half + l])
        pre = acc_a + pltpu.roll(acc_b, N_CMP_PAD - 1, axis=0)
        return jax.nn.gelu(pre, approximate=True).astype(BF16)

    kc = _dot(hidden(kc_ref, pek_ref, w1k_ref), w2k_ref[...])
    vc = _dot(hidden(vc_ref, pev_ref, w1v_ref), w2v_ref[...])
    lane = lax.broadcasted_iota(I32, kc.shape, 1)
    row = lax.broadcasted_iota(I32, kc.shape, 0)
    kc = jnp.where((lane & (LANES - 1)) < HEAD_DIM, kc, _key_extras(2 * CMP_STRIDE * row + CMP_BLOCK - 1, lane, False))
    kca_ref[0] = jnp.where(row < N_CMP_PAD - 1, kc, 0.0).astype(BF16)
    vc2_ref[0] = jnp.where(row[:, :LANES] < N_CMP_PAD - 1, vc, 0.0).astype(BF16)


def _compress(kvc, pek, pev, w1k, w1v, w2k, w2v):
    n = kvc.shape[0]
    b = n // SEQ
    full = lambda a: pl.BlockSpec(a.shape, lambda i: (0,) * a.ndim)
    kc, vc = kvc, kvc
    return pl.pallas_call(
        _compress_kernel,
        grid=(b,),
        in_specs=[
            pl.BlockSpec((SEQ, LANES), lambda i: (i, 0)),
            pl.BlockSpec((SEQ, LANES), lambda i: (i, 1)),
            full(pek), full(pev), full(w1k), full(w1v), full(w2k), full(w2v),
        ],
        out_specs=[pl.BlockSpec((1, N_CMP_PAD, 2 * LANES), lambda i: (i, 0, 0)),
                   pl.BlockSpec((1, N_CMP_PAD, LANES), lambda i: (i, 0, 0))],
        out_shape=[jax.ShapeDtypeStruct((b, N_CMP_PAD, 2 * LANES), BF16),
                   jax.ShapeDtypeStruct((b, N_CMP_PAD, LANES), BF16)],
        compiler_params=_params(("parallel",), 32 << 20),
        name="compress",
    )(kc, vc, pek, pev, w1k, w1v, w2k, w2v)


def _attend_pipelined(tasks):
    n = len(tasks)
    scores, probs, outs = [None] * n, [None] * n, [None] * n
    for i in range(n + 2):
        if i < n:
            q, k, _, _ = tasks[i]
            scores[i] = _dot_nt(q, k)
        if 1 <= i <= n:
            _, _, _, chunk_masks = tasks[i - 1]
            cols = []
            for c, allowed in enumerate(chunk_masks):
                s = scores[i - 1][:, c * KV_CHUNK:(c + 1) * KV_CHUNK]
                cols.append(s if allowed is None else jnp.where(allowed, s, NEG_INF))
            s = cols[0] if len(cols) == 1 else jnp.concatenate(cols, axis=1)
            p = jnp.exp2(s - jnp.max(s, axis=-1, keepdims=True))
            probs[i - 1] = (p.astype(BF16), 1.0 / jnp.sum(p, axis=-1, keepdims=True))
            scores[i - 1] = None
        if 2 <= i:
            p, inv = probs[i - 2]
            outs[i - 2] = _dot(p, tasks[i - 2][2]) * inv
            probs[i - 2] = None
    return outs


def _head_queries(q_ref, g, slope_scale, sel_lanes):
    rows = q_ref.shape[0]
    lane = lax.broadcasted_iota(I32, (rows, LANES), 1)
    qf = q_ref[...].astype(F32)
    parts = []
    for r in range(NSA_REP):
        slope = slope_scale * jnp.where(g == 0, 2.0 ** -(r + 1), 2.0 ** -(r + 1 + NSA_REP)).astype(F32)
        tile = qf[:, (r // 2) * LANES:(r // 2 + 1) * LANES]
        if r % 2:
            tile = pltpu.roll(tile, HEAD_DIM, axis=1)
        ext = jnp.where((lane >= X_POS) & (lane < X_POS + N_POS_TERMS), slope, sel_lanes)
        parts.append(jnp.where(lane < HEAD_DIM, tile, ext).astype(BF16))
    return parts


def _head_layout(heads, g):
    low = lax.broadcasted_iota(I32, heads[0].shape, 1) < HEAD_DIM
    tiles = []
    for pair in range(NSA_REP // 2):
        even, odd = heads[2 * pair], heads[2 * pair + 1]
        left = jnp.where(g == 0, even, pltpu.roll(even, HEAD_DIM, axis=1))
        right = jnp.where(g == 0, pltpu.roll(odd, HEAD_DIM, axis=1), odd)
        tiles.append(jnp.where(low, left, right))
    return tiles


def _nsa_select_kernel(q_ref, misc_ref, kca_ref, vc2_ref, ocmp_ref, drop_ref):
    g = pl.program_id(1)
    rows = SEL_ROWS
    t0 = pl.program_id(2) * rows
    lane = lax.broadcasted_iota(I32, (rows, LANES), 1)
    tq = t0 + lax.broadcasted_iota(I32, (rows, LANES), 0)

    q4 = jnp.concatenate(_head_queries(q_ref, g, 0.5, 0.0), axis=0)
    s_c = _dot_nt(q4, kca_ref[0])
    valid_c = lane * CMP_STRIDE + (CMP_BLOCK - 1) <= tq
    p_sum = jnp.zeros((rows, LANES), F32)
    p_parts = []
    for r in range(NSA_REP):
        s = jnp.where(valid_c, s_c[r * rows:(r + 1) * rows], NEG_INF)
        e = jnp.exp2(s - jnp.max(s, axis=-1, keepdims=True))
        p = jnp.where(valid_c, e / jnp.sum(e, axis=-1, keepdims=True), 0.0)
        p_sum += p
        p_parts.append(p.astype(BF16))
    o_cmp = _dot(jnp.concatenate(p_parts, axis=0), vc2_ref[0])
    gates = jax.nn.sigmoid(misc_ref[...])
    tiles = _head_layout([gates[:, 3 * r:3 * r + 1] * o_cmp[r * rows:(r + 1) * rows] for r in range(NSA_REP)], g)
    for i, tile in enumerate(tiles):
        ocmp_ref[:, i * LANES:(i + 1) * LANES] = tile.astype(BF16)

    ov_j = lax.broadcasted_iota(I32, (LANES, LANES), 0)
    ov_i = lax.broadcasted_iota(I32, (LANES, LANES), 1)
    ov_t = ((ov_i < 4 * ov_j + 4) & (ov_i > 4 * ov_j - 2) & (ov_j < N_SEL)).astype(F32)
    imp_t = _dot_nt(ov_t, p_sum, precision=lax.Precision.HIGHEST)[:N_SEL]
    j_blk = lax.broadcasted_iota(I32, (N_SEL, rows), 0)
    cur = (t0 + lax.broadcasted_iota(I32, (N_SEL, rows), 1)) >> SEL_SHIFT
    forced = (j_blk == 0) | (j_blk == cur) | (j_blk == cur - 1)
    imp_t = jnp.where(forced, FORCE, imp_t)
    imp_t = jnp.where(j_blk <= cur, imp_t, -FORCE)
    rank = jnp.zeros((N_SEL, rows), I32)
    for jp in range(N_SEL):
        row = imp_t[jp:jp + 1, :]
        beats = (row > imp_t) | ((row == imp_t) & (j_blk > jp))
        rank += beats.astype(I32)
    drop_t = jnp.where(rank < SEL_TOPK, 0.0, NEG_INF)
    drop_t = jnp.concatenate([drop_t, jnp.zeros((LANES - N_SEL, rows), F32)], axis=0)
    drop_q = jnp.concatenate([drop_t[:, i * LANES:(i + 1) * LANES].T for i in range(rows // LANES)], axis=0)
    drop_ref[...] = pltpu.roll(drop_q, X_SEL, axis=1).astype(BF16)


def _nsa_kernel(q_ref, misc_ref, ocmp_ref, drop_ref, ksa_ref, vs_ref, kwa_ref, vw_ref, o_ref):
    for qb in range(N_QB):
        @pl.when(pl.program_id(2) == qb)
        def _(qb=qb):
            _nsa_body(qb, q_ref, misc_ref, ocmp_ref, drop_ref, ksa_ref, vs_ref, kwa_ref, vw_ref, o_ref)


def _nsa_body(qb, q_ref, misc_ref, ocmp_ref, drop_ref, ksa_ref, vs_ref, kwa_ref, vw_ref, o_ref):
    g = pl.program_id(1)
    t0 = qb * Q_BLOCK

    def dist_to(c):
        return (t0 + lax.broadcasted_iota(I32, (Q_BLOCK, KV_CHUNK), 0)
                - (c * KV_CHUNK + lax.broadcasted_iota(I32, (Q_BLOCK, KV_CHUNK), 1)))

    c_diag = qb
    dist_diag = dist_to(c_diag)
    n_slc = (c_diag + 1) * KV_CHUNK

    slc_masks = [None] * c_diag + [dist_diag >= 0]
    q_sel = _head_queries(q_ref, g, 1.0, drop_ref[...].astype(F32))

    c_first = max(c_diag - WINDOW // KV_CHUNK, 0)
    win_masks = [None] * (c_diag - c_first + 1)
    win_masks[-1] = dist_diag >= 0
    if c_diag - c_first == WINDOW // KV_CHUNK:
        win_masks[0] = dist_to(c_first) < WINDOW
    win_rows = slice(c_first * KV_CHUNK, n_slc)
    q_win = _head_queries(q_ref, g, 1.0, 0.0)

    both = lambda masks: [None if m is None else jnp.concatenate([m, m], axis=0) for m in masks]
    tasks = []
    for r in range(0, NSA_REP, 2):
        tasks.append((jnp.concatenate(q_win[r:r + 2], axis=0), kwa_ref[win_rows, :], vw_ref[win_rows, :],
                      both(win_masks)))
    for r in range(0, NSA_REP, 2):
        tasks.append((jnp.concatenate(q_sel[r:r + 2], axis=0), ksa_ref[:n_slc, :], vs_ref[:n_slc, :],
                      both(slc_masks)))
    outs = _attend_pipelined(tasks)
    split = lambda o: [o[:Q_BLOCK], o[Q_BLOCK:]]
    o_win = split(outs[0]) + split(outs[1])
    o_slc = split(outs[2]) + split(outs[3])

    gates = jax.nn.sigmoid(misc_ref[...])
    heads = [gates[:, 3 * r + 1:3 * r + 2] * o_slc[r] + gates[:, 3 * r + 2:3 * r + 3] * o_win[r]
             for r in range(NSA_REP)]
    for i, tile in enumerate(_head_layout(heads, g)):
        cols = slice(i * LANES, (i + 1) * LANES)
        o_ref[:, cols] = (tile + ocmp_ref[:, cols].astype(F32)).astype(BF16)


def _nsa(q, misc, kca, vc2, ksa, kwa, vsw):
    n = q.shape[0]
    b = n // SEQ
    gw = NSA_REP * HEAD_DIM
    assert WINDOW % KV_CHUNK == 0 and KV_CHUNK == Q_BLOCK and SEQ % SEL_ROWS == 0
    n_sel = SEQ // SEL_ROWS
    sel_rows = lambda width: pl.BlockSpec((SEL_ROWS, width), lambda i, g, j: (i * n_sel + j, g))
    ocmp, drop = pl.pallas_call(
        _nsa_select_kernel,
        grid=(b, NSA_GROUPS, n_sel),
        in_specs=[sel_rows(gw), sel_rows(LANES),
                  pl.BlockSpec((1, N_CMP_PAD, LANES), lambda i, g, j: (i, 0, g)),
                  pl.BlockSpec((1, N_CMP_PAD, LANES), lambda i, g, j: (i, 0, 0))],
        out_specs=[sel_rows(gw), sel_rows(LANES)],
        out_shape=[jax.ShapeDtypeStruct((n, NSA_GROUPS * gw), BF16),
                   jax.ShapeDtypeStruct((n, NSA_GROUPS * LANES), BF16)],
        compiler_params=_params(("parallel", "parallel", "parallel"), 32 << 20),
        name="nsa_select",
    )(q, misc, kca, vc2)

    q_rows = lambda width: pl.BlockSpec((Q_BLOCK, width), lambda i, g, c: (i * N_QB + c, g))
    return pl.pallas_call(
        _nsa_kernel,
        grid=(b, NSA_GROUPS, N_QB),
        in_specs=[q_rows(gw), q_rows(LANES), q_rows(gw), q_rows(LANES),
                  pl.BlockSpec((SEQ, LANES), lambda i, g, c: (i, g)),
                  pl.BlockSpec((SEQ, LANES), lambda i, g, c: (i, 0)),
                  pl.BlockSpec((SEQ, LANES), lambda i, g, c: (i, g)),
                  pl.BlockSpec((SEQ, LANES), lambda i, g, c: (i, 1))],
        out_specs=q_rows(gw),
        out_shape=jax.ShapeDtypeStruct((n, NSA_GROUPS * gw), BF16),
        compiler_params=_params(("parallel", "parallel", "arbitrary"), 48 << 20),
        name="nsa",
    )(q, misc, ocmp, drop, ksa, vsw, kwa, vsw)


def _gla_kernel(q_ref, k_ref, v_ref, r_ref, misc_ref, wa_ref, ba_ref, gn_ref, o_ref):
    c, blk = GLA_CHUNK, GLA_BLOCK
    n_c = blk // c
    lane = lax.broadcasted_iota(I32, (blk, LANES), 1)
    row = lax.broadcasted_iota(I32, (blk, LANES), 0)
    in_chunk = row & (c - 1)
    chunk_of_row = row >> GLA_CHUNK_SHIFT
    r2 = lax.broadcasted_iota(I32, (2 * blk, blk), 0) & (blk - 1)
    c2 = lax.broadcasted_iota(I32, (2 * blk, blk), 1)
    intra = (r2 >= c2) & ((r2 >> GLA_CHUNK_SHIFT) == (c2 >> GLA_CHUNK_SHIFT))
    lane_s = lax.broadcasted_iota(I32, (GLA_DV, LANES), 1)
    n_blk = SEQ // blk

    def prep(i_blk):
        rows = slice(i_blk * blk, (i_blk + 1) * blk)
        la = _dot_3pass(misc_ref[rows, :], wa_ref[...]) + ba_ref[...]
        b = (jnp.minimum(la, 0.0) - jnp.log(1.0 + jnp.exp(-jnp.abs(la)))) * (1.0 / GLA_TAU)
        shift = 1
        while shift < c:
            b = b + jnp.where(in_chunk >= shift, pltpu.roll(b, shift, axis=0), 0.0)
            shift *= 2
        b3 = b.reshape(n_c, c, LANES)
        b_last = b3[:, c - 1:c, :]
        k = k_ref[rows, :]
        q_in = q_ref[rows, :] * (GLA_DK ** -0.5) * jnp.exp(b)
        k_in = (k * jnp.exp(-b)).astype(BF16)
        k_st = (k.reshape(n_c, c, LANES) * jnp.exp(b_last - b3)).reshape(blk, LANES)
        decay = jnp.exp(b_last)
        v = v_ref[rows, :]

        q2 = jnp.concatenate([jnp.where(lane < GLA_DK, q_in, 0.0), jnp.where(lane >= GLA_DK, q_in, 0.0)],
                             axis=0).astype(BF16)
        k_cols = jnp.concatenate([jnp.where(chunk_of_row == i_c, k_st, 0.0) for i_c in range(n_c)], axis=1)
        return q2, k_in, k_cols.astype(BF16), v, decay

    def chunk_parallel(q2, k_in, k_cols, v, decay):
        a = jnp.where(intra, _dot_nt(q2, k_in), 0.0).astype(BF16)
        o_intra = _dot(a, v)
        inc = _dot_tn(v, k_cols)
        return q2, o_intra, inc, decay

    per_blk = []
    ready = prep(0)
    for i_blk in range(n_blk):
        upcoming = prep(i_blk + 1) if i_blk + 1 < n_blk else None
        per_blk.append(chunk_parallel(*ready))
        ready = upcoming

    st = jnp.zeros((GLA_DV, LANES), F32)
    state_before = []
    for _, _, inc, decay in per_blk:
        for i_c in range(n_c):
            state_before.append(st.astype(BF16))
            cols = slice(i_c * LANES, (i_c + 1) * LANES)
            st = st * decay[i_c] + jnp.where(lane_s < GLA_DK, inc[:GLA_DV, cols], inc[GLA_DV:, cols])

    for i_blk, (q2, o_intra, _, _) in enumerate(per_blk):
        rows = slice(i_blk * blk, (i_blk + 1) * blk)
        o_inter = []
        for i_c in range(n_c):
            q_c = jnp.concatenate([q2[i_c * c:(i_c + 1) * c], q2[blk + i_c * c:blk + (i_c + 1) * c]], axis=0)
            o_inter.append(_dot_nt(q_c, state_before[i_blk * n_c + i_c]))
        for h in range(2):
            cols = slice(h * GLA_DV, (h + 1) * GLA_DV)
            o = o_intra[h * blk:(h + 1) * blk, cols] + jnp.concatenate(
                [o_inter[i_c][h * c:(h + 1) * c] for i_c in range(n_c)], axis=0)
            gate = r_ref[rows, cols]
            o_ref[rows, cols] = (_rms(o, gn_ref[...]) * (gate * jax.nn.sigmoid(gate))).astype(BF16)


def _gla(gq, gk, gv, gr, misc, wa, ba, gn):
    n = gq.shape[0]
    b = n // SEQ
    return pl.pallas_call(
        _gla_kernel,
        grid=(b, GLA_HEADS // 2),
        in_specs=[
            pl.BlockSpec((SEQ, LANES), lambda i, p: (i, p)),
            pl.BlockSpec((SEQ, LANES), lambda i, p: (i, p)),
            pl.BlockSpec((SEQ, 2 * GLA_DV), lambda i, p: (i, p)),
            pl.BlockSpec((SEQ, 2 * GLA_DV), lambda i, p: (i, p)),
            pl.BlockSpec((SEQ, LANES), lambda i, p: (i, 0)),
            pl.BlockSpec((LANES, LANES), lambda i, p: (0, p)),
            pl.BlockSpec((1, LANES), lambda i, p: (0, p)),
            pl.BlockSpec((1, GLA_DV), lambda i, p: (0, 0)),
        ],
        out_specs=pl.BlockSpec((SEQ, 2 * GLA_DV), lambda i, p: (i, p)),
        out_shape=jax.ShapeDtypeStruct((n, GLA_HEADS * GLA_DV), BF16),
        compiler_params=_params(("parallel", "parallel"), 40 << 20),
        name="gla",
    )(gq, gk, gv, gr, misc, wa, ba, gn)


def _compress_weights(pe, w1, w2, group_stride):
    hd, hid = HEAD_DIM, CMP_HIDDEN
    pe2 = jnp.concatenate([pe, pe], axis=1)
    w1 = w1.reshape(CMP_BLOCK, hd, hid)
    z1 = jnp.zeros_like(w1)
    w1e = jnp.concatenate([jnp.concatenate([w1, z1], axis=2), jnp.concatenate([z1, w1], axis=2)], axis=1)
    w2e = jnp.zeros((NSA_GROUPS * hid, NSA_GROUPS * group_stride), w2.dtype)
    for g in range(NSA_GROUPS):
        w2e = w2e.at[g * hid:(g + 1) * hid, g * group_stride:g * group_stride + hd].set(w2)
    return pe2, w1e.astype(BF16), w2e.astype(BF16)


def kernel(x, ffn1_norm, ffn1_w_gate, ffn1_w_up, ffn1_w_down, mix_norm, w_in, nsa_pe_k, nsa_w1_k, nsa_w2_k,
           nsa_pe_v, nsa_w1_v, nsa_w2_v, gla_w_a2, gla_b_a, gla_norm, w_out, ffn2_norm, ffn2_w_gate,
           ffn2_w_up, ffn2_w_down, final_norm):
    bsz, seq, d = x.shape
    assert (seq, d) == (SEQ, D_MODEL) and ffn1_norm.shape[0] == 1
    n = bsz * seq
    xf = x.reshape(n, d)
    row = lambda v: v.reshape(1, -1).astype(F32)
    bf = lambda w: w.astype(BF16)
    ones = jnp.ones((1, d), F32)

    x1 = _ffn(xf, (), row(ffn1_norm[0]), bf(ffn1_w_gate[0]), bf(ffn1_w_up[0]), bf(ffn1_w_down[0]), ones,
              final_norm=False)

    q, kvc, ksa, kwa, vsw, misc, gq, gk, gv, gr = _inproj(x1, row(mix_norm[0]), w_in[0])

    pek, w1k, w2k = _compress_weights(nsa_pe_k[0], nsa_w1_k[0], nsa_w2_k[0], LANES)
    pev, w1v, w2v = _compress_weights(nsa_pe_v[0], nsa_w1_v[0], nsa_w2_v[0], HEAD_DIM)
    kca, vc2 = _compress(kvc, pek, pev, w1k, w1v, w2k, w2v)

    o_nsa = _nsa(q, misc, kca, vc2, ksa, kwa, vsw)

    wa = jnp.zeros((LANES, GLA_HEADS * GLA_DK), F32).at[MISC_GA_OFF:MISC_GA_OFF + GLA_LOWRANK].set(gla_w_a2[0])
    o_gla = _gla(gq, gk, gv, gr, misc, wa, row(gla_b_a[0]), row(gla_norm[0]))

    d_nsa = NSA_HEADS * HEAD_DIM
    w_o = bf(w_out[0])
    out = _ffn(x1, (o_nsa, o_gla, w_o[:d_nsa], w_o[d_nsa:]), row(ffn2_norm[0]), bf(ffn2_w_gate[0]),
               bf(ffn2_w_up[0]), bf(ffn2_w_down[0]), row(final_norm), final_norm=True)
    return out.reshape(bsz, seq, d)
```

```python
import functools

import jax
import jax.numpy as jnp
from jax import lax
from jax.experimental import pallas as pl
from jax.experimental.pallas import tpu as pltpu

F32 = jnp.float32
BF16 = jnp.bfloat16
I32 = jnp.int32

D_MODEL = 1024
SEQ = 2048
D_FF = 2816
EPS = 1e-6
NEG_INF = -1e30
FORCE = 1e9

NSA_HEADS = 8
NSA_GROUPS = 2
NSA_REP = NSA_HEADS // NSA_GROUPS
HEAD_DIM = 64
CMP_BLOCK = 32
CMP_STRIDE = 16
CMP_HIDDEN = 128
N_CMP_PAD = SEQ // CMP_STRIDE
SEL_BLOCK = 64
SEL_SHIFT = 6
assert 1 << SEL_SHIFT == SEL_BLOCK
N_SEL = SEQ // SEL_BLOCK
SEL_TOPK = 8
WINDOW = 512
Q_BLOCK = 256
N_QB = SEQ // Q_BLOCK
KV_CHUNK = Q_BLOCK
SEL_ROWS = 1024

GLA_HEADS = 4
GLA_DK = 64
GLA_DV = 128
GLA_CHUNK = 64
GLA_CHUNK_SHIFT = 6
assert 1 << GLA_CHUNK_SHIFT == GLA_CHUNK
GLA_BLOCK = 256
GLA_LOWRANK = 16
GLA_TAU = 16.0

LANES = 128
V7X_VMEM_BYTES = 64 * 1024 * 1024
MIB = 1 << 20
VMEM_RESERVE = 8 * MIB
VMEM_TEMPS = 8 * MIB
COMPRESS_VMEM, SELECT_VMEM, NSA_VMEM, GLA_VMEM = 32 * MIB, 32 * MIB, 48 * MIB, 40 * MIB

C_Q = (0, 512)
C_KC = (512, 640)
C_VC = (640, 768)
C_KSW = (768, 1024)
C_VS = (1024, 1152)
C_VW = (1152, 1280)
C_MISC = (1280, 1536)
C_GQ = (1536, 1792)
C_GK = (1792, 2048)
C_GV = (2048, 2560)
C_GR = (2560, 3072)
D_IN_PAD = 3072
KEY_OUT_WIDTH = NSA_GROUPS * LANES
MISC_GA_OFF = 12

LOG2E = 1.4426950408889634
X_SEL = HEAD_DIM
X_POS = HEAD_DIM + N_SEL
N_POS_TERMS = 3
assert X_POS + N_POS_TERMS <= LANES


def _params(sem, vmem_bytes):
    return pltpu.CompilerParams(dimension_semantics=sem,
                                vmem_limit_bytes=min(int(vmem_bytes), V7X_VMEM_BYTES - VMEM_RESERVE))


def _rms(x, g):
    return x * lax.rsqrt(jnp.mean(x * x, axis=-1, keepdims=True) + EPS) * g


def _dot(a, b, **kw):
    return jnp.dot(a, b, preferred_element_type=F32, **kw)


def _dot_nt(a, b, **kw):
    return lax.dot_general(a, b, (((1,), (1,)), ((), ())), preferred_element_type=F32, **kw)


def _dot_3pass(a, b):
    a_hi, b_hi = a.astype(BF16), b.astype(BF16)
    a_lo = (a - a_hi.astype(F32)).astype(BF16)
    b_lo = (b - b_hi.astype(F32)).astype(BF16)
    return _dot(a_hi, b_hi) + _dot(a_lo, b_hi) + _dot(a_hi, b_lo)


def _dot_tn(a, b, **kw):
    return lax.dot_general(a, b, (((0,), (0,)), ((), ())), preferred_element_type=F32, **kw)


def _ffn_kernel(*refs, tf, final_norm, mixer_out):
    if mixer_out:
        x_ref, a_ref, b_ref, wa_ref, wb_ref, g_ref, wg_ref, wu_ref, wd_ref, fg_ref, o_ref, act_ref = refs
        x = x_ref[...] + _dot(a_ref[...], wa_ref[...]) + _dot(b_ref[...], wb_ref[...])
    else:
        x_ref, g_ref, wg_ref, wu_ref, wd_ref, fg_ref, o_ref, act_ref = refs
        x = x_ref[...]
    h = _rms(x, g_ref[...]).astype(BF16)
    for c in range(D_FF // tf):
        cols = slice(c * tf, (c + 1) * tf)
        gate = _dot(h, wg_ref[:, cols])
        up = _dot(h, wu_ref[:, cols])
        act_ref[:, cols] = ((gate * jax.nn.sigmoid(gate)) * up).astype(BF16)
    y = x + 0.5 * _dot(act_ref[...], wd_ref[...])
    if final_norm:
        y = _rms(y, fg_ref[...])
    o_ref[...] = y


def _ffn(x, mixer, gain, wg, wu, wd, final_gain, *, final_norm, tm=512, tf=256):
    n, d = x.shape
    assert D_FF % tf == 0
    resident = lambda shape: pl.BlockSpec(shape, lambda i: (0, 0), pipeline_mode=pl.Buffered(1))
    rows = lambda width: pl.BlockSpec((tm, width), lambda i: (i, 0))
    vmem = 2 * 2 * tm * d * 4 + 3 * d * D_FF * 2 + tm * D_FF * 2 + tm * d * 2 + 4 * tm * tf * 4 + VMEM_TEMPS
    mix_specs = []
    if mixer:
        a, b, w_a, w_b = mixer
        mix_specs = [rows(a.shape[1]), rows(b.shape[1]), resident(w_a.shape), resident(w_b.shape)]
        vmem += 2 * tm * (a.shape[1] + b.shape[1]) * 2 + (w_a.size + w_b.size) * 2
    return pl.pallas_call(
        functools.partial(_ffn_kernel, tf=tf, final_norm=final_norm, mixer_out=bool(mixer)),
        grid=(n // tm,),
        in_specs=[rows(d)] + mix_specs + [
            resident((1, d)),
            resident((d, D_FF)),
            resident((d, D_FF)),
            resident((D_FF, d)),
            resident((1, d)),
        ],
        out_specs=rows(d),
        out_shape=jax.ShapeDtypeStruct((n, d), F32),
        scratch_shapes=[pltpu.VMEM((tm, D_FF), BF16)],
        compiler_params=_params(("parallel",), vmem),
        name="ffn",
    )(x, *mixer, gain, wg, wu, wd, final_gain)


def _key_extras(pos, lane, with_block):
    lg = lane & (LANES - 1)
    rest = pos.astype(F32) * LOG2E
    ext = jnp.zeros(rest.shape, F32)
    for i in range(N_POS_TERMS):
        term = rest.astype(BF16).astype(F32) if i + 1 < N_POS_TERMS else rest
        ext = jnp.where(lg == X_POS + i, term, ext)
        rest = rest - term
    if with_block:
        ext = jnp.where((lg >= X_SEL) & (lg < X_POS) & ((pos >> SEL_SHIFT) == lg - X_SEL), 1.0, ext)
    return ext


def _w_in_pieces():
    hd, n_gate = HEAD_DIM, 3 * NSA_REP
    src = lambda lo, hi: ((lo, hi), hi - lo)
    pad = lambda k: (None, k)
    pieces = [src(0, 512), src(512, 640), src(640, 768),
              src(768, 768 + hd), src(1024, 1024 + hd), src(768 + hd, 768 + 2 * hd), src(1024 + hd, 1024 + 2 * hd),
              src(896, 1024), src(1152, 1280),
              src(1280, 1280 + n_gate), src(2840, 2856), pad(LANES - n_gate - GLA_LOWRANK),
              src(1280 + n_gate, 1304), pad(LANES - n_gate),
              src(1304, 1560), src(1560, 1816), src(1816, 2328), src(2328, 2840)]
    assert sum(k for _, k in pieces) == D_IN_PAD
    return pieces


def _stage_w_in(w_ref, wp_ref):
    n_src = w_ref.shape[1]
    tiles, cur, room = [], [], LANES
    for rng, k in _w_in_pieces():
        lo = rng[0] if rng else None
        while k:
            take = min(k, room)
            cur.append((lo, take))
            lo = None if lo is None else lo + take
            k, room = k - take, room - take
            if room == 0:
                tiles.append(cur)
                cur, room = [], LANES
    for t, parts in enumerate(tiles):
        vals = []
        for lo, k in parts:
            if lo is None:
                vals.append(jnp.zeros((w_ref.shape[0], k), F32))
                continue
            a_lo = lo // LANES * LANES
            a_hi = min(-(-(lo + k) // LANES) * LANES, n_src)
            vals.append(w_ref[:, a_lo:a_hi][:, lo - a_lo:lo - a_lo + k])
        tile = vals[0] if len(vals) == 1 else jnp.concatenate(vals, axis=1)
        wp_ref[:, t * LANES:(t + 1) * LANES] = tile.astype(BF16)


def _inproj_kernel(x_ref, g_ref, w_ref, q_ref, kvc_ref, ksa_ref, kwa_ref, vsw_ref, misc_ref,
                   gq_ref, gk_ref, gv_ref, gr_ref, wp_ref, *, tm):
    @pl.when(pl.program_id(0) == 0)
    def _():
        _stage_w_in(w_ref, wp_ref)

    h = _rms(x_ref[...], g_ref[...]).astype(BF16)

    def proj(c):
        return _dot(h, wp_ref[:, c[0]:c[1]])

    shape = (tm, 2 * LANES)
    lane = lax.broadcasted_iota(I32, shape, 1)
    pos = (pl.program_id(0) * tm + lax.broadcasted_iota(I32, shape, 0)) & (SEQ - 1)
    is_key = (lane & (LANES - 1)) < HEAD_DIM

    q_ref[...] = (proj(C_Q) * (HEAD_DIM ** -0.5 * LOG2E)).astype(BF16)
    assert C_KC[1] == C_VC[0] and C_VS[1] == C_VW[0]
    kvc_ref[...] = proj((C_KC[0], C_VC[1]))
    ksw = proj(C_KSW)
    kws = jnp.concatenate([pltpu.roll(ksw[:, g * LANES:(g + 1) * LANES], HEAD_DIM, axis=1)
                           for g in range(NSA_GROUPS)], axis=1)
    ksa_ref[...] = jnp.where(is_key, ksw, _key_extras(pos, lane, True)).astype(BF16)
    kwa_ref[...] = jnp.where(is_key, kws, _key_extras(pos, lane, False)).astype(BF16)
    vsw_ref[...] = proj((C_VS[0], C_VW[1])).astype(BF16)
    misc_ref[...] = proj(C_MISC)
    gq_ref[...] = proj(C_GQ)
    gk_ref[...] = proj(C_GK)
    gv_ref[...] = proj(C_GV).astype(BF16)
    gr_ref[...] = proj(C_GR)


def _inproj(x, gain, w, tm=512):
    n, d = x.shape
    assert SEQ % tm == 0 and SEQ & (SEQ - 1) == 0
    width = lambda c: c[1] - c[0]
    outs = [(width(C_Q), BF16), (width(C_KC) + width(C_VC), F32), (KEY_OUT_WIDTH, BF16), (KEY_OUT_WIDTH, BF16),
            (width(C_VS) + width(C_VW), BF16), (width(C_MISC), F32),
            (width(C_GQ), F32), (width(C_GK), F32), (width(C_GV), BF16), (width(C_GR), F32)]
    vmem = (2 * tm * d * 4 + w.size * 4 + d * D_IN_PAD * 2 + 2 * tm * (D_IN_PAD + KEY_OUT_WIDTH) * 4
            + 4 * d * LANES * 4 + VMEM_TEMPS)
    return pl.pallas_call(
        functools.partial(_inproj_kernel, tm=tm),
        grid=(n // tm,),
        in_specs=[
            pl.BlockSpec((tm, d), lambda i: (i, 0)),
            pl.BlockSpec((1, d), lambda i: (0, 0)),
            pl.BlockSpec(w.shape, lambda i: (0, 0), pipeline_mode=pl.Buffered(1)),
        ],
        out_specs=[pl.BlockSpec((tm, w_out), lambda i: (i, 0)) for w_out, _ in outs],
        out_shape=[jax.ShapeDtypeStruct((n, w_out), dt) for w_out, dt in outs],
        scratch_shapes=[pltpu.VMEM((d, D_IN_PAD), BF16)],
        compiler_params=_params(("arbitrary",), vmem),
        name="inproj",
    )(x, gain, w)


def _compress_kernel(kc_ref, vc_ref, pek_ref, pev_ref, w1k_ref, w1v_ref, w2k_ref, w2v_ref, kca_ref, vc2_ref):
    half = CMP_BLOCK // 2

    def hidden(x_ref, pe_ref, w1_ref):
        acc_a = jnp.zeros((N_CMP_PAD, 2 * CMP_HIDDEN), F32)
        acc_b = jnp.zeros((N_CMP_PAD, 2 * CMP_HIDDEN), F32)
        for l in range(half):
            rows = x_ref[pl.ds(l, N_CMP_PAD, stride=CMP_STRIDE), :]
            acc_a += _dot((rows + pe_ref[l:l + 1, :]).astype(BF16), w1_ref[l])
            acc_b += _dot((rows + pe_ref[half + l:half + l + 1, :]).astype(BF16), w1_ref[half + l])
        pre = acc_a + pltpu.roll(acc_b, N_CMP_PAD - 1, axis=0)
        return jax.nn.gelu(pre, approximate=True).astype(BF16)

    kc = _dot(hidden(kc_ref, pek_ref, w1k_ref), w2k_ref[...])
    vc = _dot(hidden(vc_ref, pev_ref, w1v_ref), w2v_ref[...])
    lane = lax.broadcasted_iota(I32, kc.shape, 1)
    row = lax.broadcasted_iota(I32, kc.shape, 0)
    kc = jnp.where((lane & (LANES - 1)) < HEAD_DIM, kc, _key_extras(2 * CMP_STRIDE * row + CMP_BLOCK - 1, lane, False))
    kca_ref[0] = jnp.where(row < N_CMP_PAD - 1, kc, 0.0).astype(BF16)
    vc2_ref[0] = jnp.where(row[:, :LANES] < N_CMP_PAD - 1, vc, 0.0).astype(BF16)


def _compress(kvc, pek, pev, w1k, w1v, w2k, w2v):
    n = kvc.shape[0]
    b = n // SEQ
    full = lambda a: pl.BlockSpec(a.shape, lambda i: (0,) * a.ndim)
    kc, vc = kvc, kvc
    return pl.pallas_call(
        _compress_kernel,
        grid=(b,),
        in_specs=[
            pl.BlockSpec((SEQ, LANES), lambda i: (i, 0)),
            pl.BlockSpec((SEQ, LANES), lambda i: (i, 1)),
            full(pek), full(pev), full(w1k), full(w1v), full(w2k), full(w2v),
        ],
        out_specs=[pl.BlockSpec((1, N_CMP_PAD, 2 * LANES), lambda i: (i, 0, 0)),
                   pl.BlockSpec((1, N_CMP_PAD, LANES), lambda i: (i, 0, 0))],
        out_shape=[jax.ShapeDtypeStruct((b, N_CMP_PAD, 2 * LANES), BF16),
                   jax.ShapeDtypeStruct((b, N_CMP_PAD, LANES), BF16)],
        compiler_params=_params(("parallel",), COMPRESS_VMEM),
        name="compress",
    )(kc, vc, pek, pev, w1k, w1v, w2k, w2v)


def _attend_pipelined(tasks):
    n = len(tasks)
    scores, probs, outs = [None] * n, [None] * n, [None] * n
    for i in range(n + 2):
        if i < n:
            q, k, _, _ = tasks[i]
            scores[i] = _dot_nt(q, k)
        if 1 <= i <= n:
            _, _, _, chunk_masks = tasks[i - 1]
            cols = []
            for c, allowed in enumerate(chunk_masks):
                s = scores[i - 1][:, c * KV_CHUNK:(c + 1) * KV_CHUNK]
                cols.append(s if allowed is None else jnp.where(allowed, s, NEG_INF))
            s = cols[0] if len(cols) == 1 else jnp.concatenate(cols, axis=1)
            p = jnp.exp2(s - jnp.max(s, axis=-1, keepdims=True))
            probs[i - 1] = (p.astype(BF16), 1.0 / jnp.sum(p, axis=-1, keepdims=True))
            scores[i - 1] = None
        if 2 <= i:
            p, inv = probs[i - 2]
            outs[i - 2] = _dot(p, tasks[i - 2][2]) * inv
            probs[i - 2] = None
    return outs


def _head_queries(q_ref, g, slope_scale, sel_lanes):
    rows = q_ref.shape[0]
    lane = lax.broadcasted_iota(I32, (rows, LANES), 1)
    qf = q_ref[...].astype(F32)
    parts = []
    for r in range(NSA_REP):
        slope = slope_scale * jnp.where(g == 0, 2.0 ** -(r + 1), 2.0 ** -(r + 1 + NSA_REP)).astype(F32)
        tile = qf[:, (r // 2) * LANES:(r // 2 + 1) * LANES]
        if r % 2:
            tile = pltpu.roll(tile, HEAD_DIM, axis=1)
        ext = jnp.where((lane >= X_POS) & (lane < X_POS + N_POS_TERMS), slope, sel_lanes)
        parts.append(jnp.where(lane < HEAD_DIM, tile, ext).astype(BF16))
    return parts


def _head_layout(heads, g):
    low = lax.broadcasted_iota(I32, heads[0].shape, 1) < HEAD_DIM
    tiles = []
    for pair in range(NSA_REP // 2):
        even, odd = heads[2 * pair], heads[2 * pair + 1]
        left = jnp.where(g == 0, even, pltpu.roll(even, HEAD_DIM, axis=1))
        right = jnp.where(g == 0, pltpu.roll(odd, HEAD_DIM, axis=1), odd)
        tiles.append(jnp.where(low, left, right))
    return tiles


def _nsa_select_kernel(q_ref, misc_ref, kca_ref, vc2_ref, ocmp_ref, drop_ref):
    g = pl.program_id(1)
    rows = SEL_ROWS
    t0 = pl.program_id(2) * rows
    lane = lax.broadcasted_iota(I32, (rows, LANES), 1)
    tq = t0 + lax.broadcasted_iota(I32, (rows, LANES), 0)

    q4 = jnp.concatenate(_head_queries(q_ref, g, 0.5, 0.0), axis=0)
    s_c = _dot_nt(q4, kca_ref[0])
    valid_c = lane * CMP_STRIDE + (CMP_BLOCK - 1) <= tq
    p_sum = jnp.zeros((rows, LANES), F32)
    p_parts = []
    for r in range(NSA_REP):
        s = jnp.where(valid_c, s_c[r * rows:(r + 1) * rows], NEG_INF)
        e = jnp.exp2(s - jnp.max(s, axis=-1, keepdims=True))
        p = jnp.where(valid_c, e / jnp.sum(e, axis=-1, keepdims=True), 0.0)
        p_sum += p
        p_parts.append(p.astype(BF16))
    o_cmp = _dot(jnp.concatenate(p_parts, axis=0), vc2_ref[0])
    gates = jax.nn.sigmoid(misc_ref[...])
    tiles = _head_layout([gates[:, 3 * r:3 * r + 1] * o_cmp[r * rows:(r + 1) * rows] for r in range(NSA_REP)], g)
    for i, tile in enumerate(tiles):
        ocmp_ref[:, i * LANES:(i + 1) * LANES] = tile.astype(BF16)

    ov_j = lax.broadcasted_iota(I32, (LANES, LANES), 0)
    ov_i = lax.broadcasted_iota(I32, (LANES, LANES), 1)
    ov_t = ((ov_i < 4 * ov_j + 4) & (ov_i > 4 * ov_j - 2) & (ov_j < N_SEL)).astype(F32)
    imp_t = _dot_nt(ov_t, p_sum, precision=lax.Precision.HIGHEST)[:N_SEL]
    j_blk = lax.broadcasted_iota(I32, (N_SEL, rows), 0)
    cur = (t0 + lax.broadcasted_iota(I32, (N_SEL, rows), 1)) >> SEL_SHIFT
    forced = (j_blk == 0) | (j_blk == cur) | (j_blk == cur - 1)
    imp_t = jnp.where(forced, FORCE, imp_t)
    imp_t = jnp.where(j_blk <= cur, imp_t, -FORCE)
    rank = jnp.zeros((N_SEL, rows), I32)
    for jp in range(N_SEL):
        row = imp_t[jp:jp + 1, :]
        beats = (row > imp_t) | ((row == imp_t) & (j_blk > jp))
        rank += beats.astype(I32)
    drop_t = jnp.where(rank < SEL_TOPK, 0.0, NEG_INF)
    drop_t = jnp.concatenate([drop_t, jnp.zeros((LANES - N_SEL, rows), F32)], axis=0)
    drop_q = jnp.concatenate([drop_t[:, i * LANES:(i + 1) * LANES].T for i in range(rows // LANES)], axis=0)
    drop_ref[...] = pltpu.roll(drop_q, X_SEL, axis=1).astype(BF16)


def _nsa_kernel(q_ref, misc_ref, ocmp_ref, drop_ref, ksa_ref, vs_ref, kwa_ref, vw_ref, o_ref):
    for qb in range(N_QB):
        @pl.when(pl.program_id(2) == qb)
        def _(qb=qb):
            _nsa_body(qb, q_ref, misc_ref, ocmp_ref, drop_ref, ksa_ref, vs_ref, kwa_ref, vw_ref, o_ref)


def _nsa_body(qb, q_ref, misc_ref, ocmp_ref, drop_ref, ksa_ref, vs_ref, kwa_ref, vw_ref, o_ref):
    g = pl.program_id(1)
    t0 = qb * Q_BLOCK

    def dist_to(c):
        return (t0 + lax.broadcasted_iota(I32, (Q_BLOCK, KV_CHUNK), 0)
                - (c * KV_CHUNK + lax.broadcasted_iota(I32, (Q_BLOCK, KV_CHUNK), 1)))

    c_diag = qb
    dist_diag = dist_to(c_diag)
    n_slc = (c_diag + 1) * KV_CHUNK

    slc_masks = [None] * c_diag + [dist_diag >= 0]
    q_sel = _head_queries(q_ref, g, 1.0, drop_ref[...].astype(F32))

    c_first = max(c_diag - WINDOW // KV_CHUNK, 0)
    win_masks = [None] * (c_diag - c_first + 1)
    win_masks[-1] = dist_diag >= 0
    if c_diag - c_first == WINDOW // KV_CHUNK:
        win_masks[0] = dist_to(c_first) < WINDOW
    win_rows = slice(c_first * KV_CHUNK, n_slc)
    q_win = _head_queries(q_ref, g, 1.0, 0.0)

    both = lambda masks: [None if m is None else jnp.concatenate([m, m], axis=0) for m in masks]
    tasks = []
    for r in range(0, NSA_REP, 2):
        tasks.append((jnp.concatenate(q_win[r:r + 2], axis=0), kwa_ref[win_rows, :], vw_ref[win_rows, :],
                      both(win_masks)))
    for r in range(0, NSA_REP, 2):
        tasks.append((jnp.concatenate(q_sel[r:r + 2], axis=0), ksa_ref[:n_slc, :], vs_ref[:n_slc, :],
                      both(slc_masks)))
    outs = _attend_pipelined(tasks)
    split = lambda o: [o[:Q_BLOCK], o[Q_BLOCK:]]
    o_win = split(outs[0]) + split(outs[1])
    o_slc = split(outs[2]) + split(outs[3])

    gates = jax.nn.sigmoid(misc_ref[...])
    heads = [gates[:, 3 * r + 1:3 * r + 2] * o_slc[r] + gates[:, 3 * r + 2:3 * r + 3] * o_win[r]
             for r in range(NSA_REP)]
    for i, tile in enumerate(_head_layout(heads, g)):
        cols = slice(i * LANES, (i + 1) * LANES)
        o_ref[:, cols] = (tile + ocmp_ref[:, cols].astype(F32)).astype(BF16)


def _nsa(q, misc, kca, vc2, ksa, kwa, vsw):
    n = q.shape[0]
    b = n // SEQ
    gw = NSA_REP * HEAD_DIM
    assert WINDOW % KV_CHUNK == 0 and KV_CHUNK == Q_BLOCK and SEQ % SEL_ROWS == 0
    n_sel = SEQ // SEL_ROWS
    sel_rows = lambda width: pl.BlockSpec((SEL_ROWS, width), lambda i, g, j: (i * n_sel + j, g))
    ocmp, drop = pl.pallas_call(
        _nsa_select_kernel,
        grid=(b, NSA_GROUPS, n_sel),
        in_specs=[sel_rows(gw), sel_rows(LANES),
                  pl.BlockSpec((1, N_CMP_PAD, LANES), lambda i, g, j: (i, 0, g)),
                  pl.BlockSpec((1, N_CMP_PAD, LANES), lambda i, g, j: (i, 0, 0))],
        out_specs=[sel_rows(gw), sel_rows(LANES)],
        out_shape=[jax.ShapeDtypeStruct((n, NSA_GROUPS * gw), BF16),
                   jax.ShapeDtypeStruct((n, NSA_GROUPS * LANES), BF16)],
        compiler_params=_params(("parallel", "parallel", "parallel"), SELECT_VMEM),
        name="nsa_select",
    )(q, misc, kca, vc2)

    q_rows = lambda width: pl.BlockSpec((Q_BLOCK, width), lambda i, g, c: (i * N_QB + c, g))
    return pl.pallas_call(
        _nsa_kernel,
        grid=(b, NSA_GROUPS, N_QB),
        in_specs=[q_rows(gw), q_rows(LANES), q_rows(gw), q_rows(LANES),
                  pl.BlockSpec((SEQ, LANES), lambda i, g, c: (i, g)),
                  pl.BlockSpec((SEQ, LANES), lambda i, g, c: (i, 0)),
                  pl.BlockSpec((SEQ, LANES), lambda i, g, c: (i, g)),
                  pl.BlockSpec((SEQ, LANES), lambda i, g, c: (i, 1))],
        out_specs=q_rows(gw),
        out_shape=jax.ShapeDtypeStruct((n, NSA_GROUPS * gw), BF16),
        compiler_params=_params(("parallel", "parallel", "arbitrary"), NSA_VMEM),
        name="nsa",
    )(q, misc, ocmp, drop, ksa, vsw, kwa, vsw)


def _gla_kernel(q_ref, k_ref, v_ref, r_ref, misc_ref, wa_ref, ba_ref, gn_ref, o_ref):
    c, blk = GLA_CHUNK, GLA_BLOCK
    n_c = blk // c
    lane = lax.broadcasted_iota(I32, (blk, LANES), 1)
    row = lax.broadcasted_iota(I32, (blk, LANES), 0)
    in_chunk = row & (c - 1)
    chunk_of_row = row >> GLA_CHUNK_SHIFT
    r2 = lax.broadcasted_iota(I32, (2 * blk, blk), 0) & (blk - 1)
    c2 = lax.broadcasted_iota(I32, (2 * blk, blk), 1)
    intra = (r2 >= c2) & ((r2 >> GLA_CHUNK_SHIFT) == (c2 >> GLA_CHUNK_SHIFT))
    lane_s = lax.broadcasted_iota(I32, (GLA_DV, LANES), 1)
    n_blk = SEQ // blk

    def prep(i_blk):
        rows = slice(i_blk * blk, (i_blk + 1) * blk)
        la = _dot_3pass(misc_ref[rows, :], wa_ref[...]) + ba_ref[...]
        b = (jnp.minimum(la, 0.0) - jnp.log(1.0 + jnp.exp(-jnp.abs(la)))) * (1.0 / GLA_TAU)
        shift = 1
        while shift < c:
            b = b + jnp.where(in_chunk >= shift, pltpu.roll(b, shift, axis=0), 0.0)
            shift *= 2
        b3 = b.reshape(n_c, c, LANES)
        b_last = b3[:, c - 1:c, :]
        k = k_ref[rows, :]
        q_in = q_ref[rows, :] * (GLA_DK ** -0.5) * jnp.exp(b)
        k_in = (k * jnp.exp(-b)).astype(BF16)
        k_st = (k.reshape(n_c, c, LANES) * jnp.exp(b_last - b3)).reshape(blk, LANES)
        decay = jnp.exp(b_last)
        v = v_ref[rows, :]

        q2 = jnp.concatenate([jnp.where(lane < GLA_DK, q_in, 0.0), jnp.where(lane >= GLA_DK, q_in, 0.0)],
                             axis=0).astype(BF16)
        k_cols = jnp.concatenate([jnp.where(chunk_of_row == i_c, k_st, 0.0) for i_c in range(n_c)], axis=1)
        return q2, k_in, k_cols.astype(BF16), v, decay

    def chunk_parallel(q2, k_in, k_cols, v, decay):
        a = jnp.where(intra, _dot_nt(q2, k_in), 0.0).astype(BF16)
        o_intra = _dot(a, v)
        inc = _dot_tn(v, k_cols)
        return q2, o_intra, inc, decay

    per_blk = []
    ready = prep(0)
    for i_blk in range(n_blk):
        upcoming = prep(i_blk + 1) if i_blk + 1 < n_blk else None
        per_blk.append(chunk_parallel(*ready))
        ready = upcoming

    st = jnp.zeros((GLA_DV, LANES), F32)
    state_before = []
    for _, _, inc, decay in per_blk:
        for i_c in range(n_c):
            state_before.append(st.astype(BF16))
            cols = slice(i_c * LANES, (i_c + 1) * LANES)
            st = st * decay[i_c] + jnp.where(lane_s < GLA_DK, inc[:GLA_DV, cols], inc[GLA_DV:, cols])

    for i_blk, (q2, o_intra, _, _) in enumerate(per_blk):
        rows = slice(i_blk * blk, (i_blk + 1) * blk)
        o_inter = []
        for i_c in range(n_c):
            q_c = jnp.concatenate([q2[i_c * c:(i_c + 1) * c], q2[blk + i_c * c:blk + (i_c + 1) * c]], axis=0)
            o_inter.append(_dot_nt(q_c, state_before[i_blk * n_c + i_c]))
        for h in range(2):
            cols = slice(h * GLA_DV, (h + 1) * GLA_DV)
            o = o_intra[h * blk:(h + 1) * blk, cols] + jnp.concatenate(
                [o_inter[i_c][h * c:(h + 1) * c] for i_c in range(n_c)], axis=0)
            gate = r_ref[rows, cols]
            o_ref[rows, cols] = (_rms(o, gn_ref[...]) * (gate * jax.nn.sigmoid(gate))).astype(BF16)


def _gla(gq, gk, gv, gr, misc, wa, ba, gn):
    n = gq.shape[0]
    b = n // SEQ
    return pl.pallas_call(
        _gla_kernel,
        grid=(b, GLA_HEADS // 2),
        in_specs=[
            pl.BlockSpec((SEQ, LANES), lambda i, p: (i, p)),
            pl.BlockSpec((SEQ, LANES), lambda i, p: (i, p)),
            pl.BlockSpec((SEQ, 2 * GLA_DV), lambda i, p: (i, p)),
            pl.BlockSpec((SEQ, 2 * GLA_DV), lambda i, p: (i, p)),
            pl.BlockSpec((SEQ, LANES), lambda i, p: (i, 0)),
            pl.BlockSpec((LANES, LANES), lambda i, p: (0, p)),
            pl.BlockSpec((1, LANES), lambda i, p: (0, p)),
            pl.BlockSpec((1, GLA_DV), lambda i, p: (0, 0)),
        ],
        out_specs=pl.BlockSpec((SEQ, 2 * GLA_DV), lambda i, p: (i, p)),
        out_shape=jax.ShapeDtypeStruct((n, GLA_HEADS * GLA_DV), BF16),
        compiler_params=_params(("parallel", "parallel"), GLA_VMEM),
        name="gla",
    )(gq, gk, gv, gr, misc, wa, ba, gn)


def _compress_weights(pe, w1, w2, group_stride):
    hd, hid = HEAD_DIM, CMP_HIDDEN
    pe2 = jnp.concatenate([pe, pe], axis=1)
    w1 = w1.reshape(CMP_BLOCK, hd, hid)
    z1 = jnp.zeros_like(w1)
    w1e = jnp.concatenate([jnp.concatenate([w1, z1], axis=2), jnp.concatenate([z1, w1], axis=2)], axis=1)
    w2e = jnp.zeros((NSA_GROUPS * hid, NSA_GROUPS * group_stride), w2.dtype)
    for g in range(NSA_GROUPS):
        w2e = w2e.at[g * hid:(g + 1) * hid, g * group_stride:g * group_stride + hd].set(w2)
    return pe2, w1e.astype(BF16), w2e.astype(BF16)


def kernel(x, ffn1_norm, ffn1_w_gate, ffn1_w_up, ffn1_w_down, mix_norm, w_in, nsa_pe_k, nsa_w1_k, nsa_w2_k,
           nsa_pe_v, nsa_w1_v, nsa_w2_v, gla_w_a2, gla_b_a, gla_norm, w_out, ffn2_norm, ffn2_w_gate,
           ffn2_w_up, ffn2_w_down, final_norm):
    bsz, seq, d = x.shape
    assert (seq, d) == (SEQ, D_MODEL) and ffn1_norm.shape[0] == 1
    n = bsz * seq
    xf = x.reshape(n, d)
    row = lambda v: v.reshape(1, -1).astype(F32)
    bf = lambda w: w.astype(BF16)
    ones = jnp.ones((1, d), F32)

    x1 = _ffn(xf, (), row(ffn1_norm[0]), bf(ffn1_w_gate[0]), bf(ffn1_w_up[0]), bf(ffn1_w_down[0]), ones,
              final_norm=False)

    q, kvc, ksa, kwa, vsw, misc, gq, gk, gv, gr = _inproj(x1, row(mix_norm[0]), w_in[0])

    pek, w1k, w2k = _compress_weights(nsa_pe_k[0], nsa_w1_k[0], nsa_w2_k[0], LANES)
    pev, w1v, w2v = _compress_weights(nsa_pe_v[0], nsa_w1_v[0], nsa_w2_v[0], HEAD_DIM)
    kca, vc2 = _compress(kvc, pek, pev, w1k, w1v, w2k, w2v)

    o_nsa = _nsa(q, misc, kca, vc2, ksa, kwa, vsw)

    wa = jnp.zeros((LANES, GLA_HEADS * GLA_DK), F32).at[MISC_GA_OFF:MISC_GA_OFF + GLA_LOWRANK].set(gla_w_a2[0])
    o_gla = _gla(gq, gk, gv, gr, misc, wa, row(gla_b_a[0]), row(gla_norm[0]))

    d_nsa = NSA_HEADS * HEAD_DIM
    w_o = bf(w_out[0])
    out = _ffn(x1, (o_nsa, o_gla, w_o[:d_nsa], w_o[d_nsa:]), row(ffn2_norm[0]), bf(ffn2_w_gate[0]),
               bf(ffn2_w_up[0]), bf(ffn2_w_down[0]), row(final_norm), final_norm=True)
    return out.reshape(bsz, seq, d)
```

```python
import functools

import jax
import jax.numpy as jnp
from jax import lax
from jax.experimental import pallas as pl
from jax.experimental.pallas import tpu as pltpu

F32 = jnp.float32
BF16 = jnp.bfloat16
I32 = jnp.int32

D_MODEL = 1024
SEQ = 2048
D_FF = 2816
EPS = 1e-6
NEG_INF = -1e30
FORCE = 1e9

NSA_HEADS = 8
NSA_GROUPS = 2
NSA_REP = NSA_HEADS // NSA_GROUPS
HEAD_DIM = 64
CMP_BLOCK = 32
CMP_STRIDE = 16
CMP_HIDDEN = 128
N_CMP_PAD = SEQ // CMP_STRIDE
SEL_BLOCK = 64
SEL_SHIFT = 6
assert 1 << SEL_SHIFT == SEL_BLOCK
N_SEL = SEQ // SEL_BLOCK
SEL_TOPK = 8
WINDOW = 512
Q_BLOCK = 256
N_QB = SEQ // Q_BLOCK
KV_CHUNK = Q_BLOCK
SEL_ROWS = 1024
SLC_SINGLE_HEAD_FROM = 4

GLA_HEADS = 4
GLA_DK = 64
GLA_DV = 128
GLA_CHUNK = 64
GLA_CHUNK_SHIFT = 6
assert 1 << GLA_CHUNK_SHIFT == GLA_CHUNK
GLA_BLOCK = 256
GLA_LOWRANK = 16
GLA_TAU = 16.0

LANES = 128
V7X_VMEM_BYTES = 64 * 1024 * 1024
MIB = 1 << 20
VMEM_RESERVE = 8 * MIB
VMEM_TEMPS = 8 * MIB
COMPRESS_VMEM, SELECT_VMEM, NSA_VMEM, GLA_VMEM = 32 * MIB, 32 * MIB, 48 * MIB, 40 * MIB

C_Q = (0, 512)
C_KC = (512, 640)
C_VC = (640, 768)
C_KSW = (768, 1024)
C_VS = (1024, 1152)
C_VW = (1152, 1280)
C_MISC = (1280, 1536)
C_GQ = (1536, 1792)
C_GK = (1792, 2048)
C_GV = (2048, 2560)
C_GR = (2560, 3072)
D_IN_PAD = 3072
KEY_OUT_WIDTH = NSA_GROUPS * LANES
MISC_GA_OFF = 12

LOG2E = 1.4426950408889634
X_SEL = HEAD_DIM
X_POS = HEAD_DIM + N_SEL
N_POS_TERMS = 3
assert X_POS + N_POS_TERMS <= LANES


def _params(sem, vmem_bytes):
    return pltpu.CompilerParams(dimension_semantics=sem,
                                vmem_limit_bytes=min(int(vmem_bytes), V7X_VMEM_BYTES - VMEM_RESERVE))


def _rms(x, g):
    return x * lax.rsqrt(jnp.mean(x * x, axis=-1, keepdims=True) + EPS) * g


def _dot(a, b, **kw):
    return jnp.dot(a, b, preferred_element_type=F32, **kw)


def _dot_nt(a, b, **kw):
    return lax.dot_general(a, b, (((1,), (1,)), ((), ())), preferred_element_type=F32, **kw)


def _dot_3pass(a, b):
    a_hi, b_hi = a.astype(BF16), b.astype(BF16)
    a_lo = (a - a_hi.astype(F32)).astype(BF16)
    b_lo = (b - b_hi.astype(F32)).astype(BF16)
    return _dot(a_hi, b_hi) + _dot(a_lo, b_hi) + _dot(a_hi, b_lo)


def _dot_tn(a, b, **kw):
    return lax.dot_general(a, b, (((0,), (0,)), ((), ())), preferred_element_type=F32, **kw)


def _ffn_kernel(*refs, tf, final_norm, mixer_out):
    if mixer_out:
        x_ref, a_ref, b_ref, wa_ref, wb_ref, g_ref, wg_ref, wu_ref, wd_ref, fg_ref, o_ref, act_ref = refs
        x = x_ref[...] + _dot(a_ref[...], wa_ref[...]) + _dot(b_ref[...], wb_ref[...])
    else:
        x_ref, g_ref, wg_ref, wu_ref, wd_ref, fg_ref, o_ref, act_ref = refs
        x = x_ref[...]
    h = _rms(x, g_ref[...]).astype(BF16)
    for c in range(D_FF // tf):
        cols = slice(c * tf, (c + 1) * tf)
        gate = _dot(h, wg_ref[:, cols])
        up = _dot(h, wu_ref[:, cols])
        act_ref[:, cols] = ((gate * jax.nn.sigmoid(gate)) * up).astype(BF16)
    y = x + 0.5 * _dot(act_ref[...], wd_ref[...])
    if final_norm:
        y = _rms(y, fg_ref[...])
    o_ref[...] = y


def _ffn(x, mixer, gain, wg, wu, wd, final_gain, *, final_norm, tm=1024, tf=256):
    n, d = x.shape
    assert D_FF % tf == 0
    resident = lambda shape: pl.BlockSpec(shape, lambda i: (0, 0), pipeline_mode=pl.Buffered(1))
    rows = lambda width: pl.BlockSpec((tm, width), lambda i: (i, 0))
    vmem = 2 * 2 * tm * d * 4 + 3 * d * D_FF * 2 + tm * D_FF * 2 + tm * d * 2 + 4 * tm * tf * 4 + VMEM_TEMPS
    mix_specs = []
    if mixer:
        a, b, w_a, w_b = mixer
        mix_specs = [rows(a.shape[1]), rows(b.shape[1]), resident(w_a.shape), resident(w_b.shape)]
        vmem += 2 * tm * (a.shape[1] + b.shape[1]) * 2 + (w_a.size + w_b.size) * 2
    return pl.pallas_call(
        functools.partial(_ffn_kernel, tf=tf, final_norm=final_norm, mixer_out=bool(mixer)),
        grid=(n // tm,),
        in_specs=[rows(d)] + mix_specs + [
            resident((1, d)),
            resident((d, D_FF)),
            resident((d, D_FF)),
            resident((D_FF, d)),
            resident((1, d)),
        ],
        out_specs=rows(d),
        out_shape=jax.ShapeDtypeStruct((n, d), F32),
        scratch_shapes=[pltpu.VMEM((tm, D_FF), BF16)],
        compiler_params=_params(("parallel",), vmem),
        name="ffn",
    )(x, *mixer, gain, wg, wu, wd, final_gain)


def _key_extras(pos, lane, with_block):
    lg = lane & (LANES - 1)
    rest = pos.astype(F32) * LOG2E
    ext = jnp.zeros(rest.shape, F32)
    for i in range(N_POS_TERMS):
        term = rest.astype(BF16).astype(F32) if i + 1 < N_POS_TERMS else rest
        ext = jnp.where(lg == X_POS + i, term, ext)
        rest = rest - term
    if with_block:
        ext = jnp.where((lg >= X_SEL) & (lg < X_POS) & ((pos >> SEL_SHIFT) == lg - X_SEL), 1.0, ext)
    return ext


def _w_in_pieces():
    hd, n_gate = HEAD_DIM, 3 * NSA_REP
    src = lambda lo, hi: ((lo, hi), hi - lo)
    pad = lambda k: (None, k)
    pieces = [src(0, 512), src(512, 640), src(640, 768),
              src(768, 768 + hd), src(1024, 1024 + hd), src(768 + hd, 768 + 2 * hd), src(1024 + hd, 1024 + 2 * hd),
              src(896, 1024), src(1152, 1280),
              src(1280, 1280 + n_gate), src(2840, 2856), pad(LANES - n_gate - GLA_LOWRANK),
              src(1280 + n_gate, 1304), pad(LANES - n_gate),
              src(1304, 1560), src(1560, 1816), src(1816, 2328), src(2328, 2840)]
    assert sum(k for _, k in pieces) == D_IN_PAD
    return pieces


def _stage_w_in(w_ref, wp_ref):
    n_src = w_ref.shape[1]
    tiles, cur, room = [], [], LANES
    for rng, k in _w_in_pieces():
        lo = rng[0] if rng else None
        while k:
            take = min(k, room)
            cur.append((lo, take))
            lo = None if lo is None else lo + take
            k, room = k - take, room - take
            if room == 0:
                tiles.append(cur)
                cur, room = [], LANES
    for t, parts in enumerate(tiles):
        vals = []
        for lo, k in parts:
            if lo is None:
                vals.append(jnp.zeros((w_ref.shape[0], k), F32))
                continue
            a_lo = lo // LANES * LANES
            a_hi = min(-(-(lo + k) // LANES) * LANES, n_src)
            vals.append(w_ref[:, a_lo:a_hi][:, lo - a_lo:lo - a_lo + k])
        tile = vals[0] if len(vals) == 1 else jnp.concatenate(vals, axis=1)
        wp_ref[:, t * LANES:(t + 1) * LANES] = tile.astype(BF16)


def _inproj_kernel(x_ref, g_ref, w_ref, q_ref, kvc_ref, ksa_ref, kwa_ref, vsw_ref, misc_ref,
                   gq_ref, gk_ref, gv_ref, gr_ref, wp_ref, *, tm):
    @pl.when(pl.program_id(0) == 0)
    def _():
        _stage_w_in(w_ref, wp_ref)

    h = _rms(x_ref[...], g_ref[...]).astype(BF16)

    def proj(c):
        return _dot(h, wp_ref[:, c[0]:c[1]])

    shape = (tm, 2 * LANES)
    lane = lax.broadcasted_iota(I32, shape, 1)
    pos = (pl.program_id(0) * tm + lax.broadcasted_iota(I32, shape, 0)) & (SEQ - 1)
    is_key = (lane & (LANES - 1)) < HEAD_DIM

    q_ref[...] = (proj(C_Q) * (HEAD_DIM ** -0.5 * LOG2E)).astype(BF16)
    assert C_KC[1] == C_VC[0] and C_VS[1] == C_VW[0]
    kvc_ref[...] = proj((C_KC[0], C_VC[1]))
    ksw = proj(C_KSW)
    kws = jnp.concatenate([pltpu.roll(ksw[:, g * LANES:(g + 1) * LANES], HEAD_DIM, axis=1)
                           for g in range(NSA_GROUPS)], axis=1)
    ksa_ref[...] = jnp.where(is_key, ksw, _key_extras(pos, lane, True)).astype(BF16)
    kwa_ref[...] = jnp.where(is_key, kws, _key_extras(pos, lane, False)).astype(BF16)
    vsw_ref[...] = proj((C_VS[0], C_VW[1])).astype(BF16)
    misc_ref[...] = proj(C_MISC)
    gq_ref[...] = proj(C_GQ)
    gk_ref[...] = proj(C_GK)
    gv_ref[...] = proj(C_GV).astype(BF16)
    gr_ref[...] = proj(C_GR)


def _inproj(x, gain, w, tm=512):
    n, d = x.shape
    assert SEQ % tm == 0 and SEQ & (SEQ - 1) == 0
    width = lambda c: c[1] - c[0]
    outs = [(width(C_Q), BF16), (width(C_KC) + width(C_VC), F32), (KEY_OUT_WIDTH, BF16), (KEY_OUT_WIDTH, BF16),
            (width(C_VS) + width(C_VW), BF16), (width(C_MISC), F32),
            (width(C_GQ), F32), (width(C_GK), F32), (width(C_GV), BF16), (width(C_GR), F32)]
    vmem = (2 * tm * d * 4 + w.size * 4 + d * D_IN_PAD * 2 + 2 * tm * (D_IN_PAD + KEY_OUT_WIDTH) * 4
            + 4 * d * LANES * 4 + VMEM_TEMPS)
    return pl.pallas_call(
        functools.partial(_inproj_kernel, tm=tm),
        grid=(n // tm,),
        in_specs=[
            pl.BlockSpec((tm, d), lambda i: (i, 0)),
            pl.BlockSpec((1, d), lambda i: (0, 0)),
            pl.BlockSpec(w.shape, lambda i: (0, 0), pipeline_mode=pl.Buffered(1)),
        ],
        out_specs=[pl.BlockSpec((tm, w_out), lambda i: (i, 0)) for w_out, _ in outs],
        out_shape=[jax.ShapeDtypeStruct((n, w_out), dt) for w_out, dt in outs],
        scratch_shapes=[pltpu.VMEM((d, D_IN_PAD), BF16)],
        compiler_params=_params(("arbitrary",), vmem),
        name="inproj",
    )(x, gain, w)


def _compress_kernel(kc_ref, vc_ref, pek_ref, pev_ref, w1k_ref, w1v_ref, w2k_ref, w2v_ref, kca_ref, vc2_ref):
    half = CMP_BLOCK // 2

    def hidden(x_ref, pe_ref, w1_ref):
        acc_a = jnp.zeros((N_CMP_PAD, 2 * CMP_HIDDEN), F32)
        acc_b = jnp.zeros((N_CMP_PAD, 2 * CMP_HIDDEN), F32)
        for l in range(half):
            rows = x_ref[pl.ds(l, N_CMP_PAD, stride=CMP_STRIDE), :]
            acc_a += _dot((rows + pe_ref[l:l + 1, :]).astype(BF16), w1_ref[l])
            acc_b += _dot((rows + pe_ref[half + l:half + l + 1, :]).astype(BF16), w1_ref[half + l])
        pre = acc_a + pltpu.roll(acc_b, N_CMP_PAD - 1, axis=0)
        return jax.nn.gelu(pre, approximate=True).astype(BF16)

    kc = _dot(hidden(kc_ref, pek_ref, w1k_ref), w2k_ref[...])
    vc = _dot(hidden(vc_ref, pev_ref, w1v_ref), w2v_ref[...])
    lane = lax.broadcasted_iota(I32, kc.shape, 1)
    row = lax.broadcasted_iota(I32, kc.shape, 0)
    kc = jnp.where((lane & (LANES - 1)) < HEAD_DIM, kc, _key_extras(2 * CMP_STRIDE * row + CMP_BLOCK - 1, lane, False))
    kca_ref[0] = jnp.where(row < N_CMP_PAD - 1, kc, 0.0).astype(BF16)
    vc2_ref[0] = jnp.where(row[:, :LANES] < N_CMP_PAD - 1, vc, 0.0).astype(BF16)


def _compress(kvc, pek, pev, w1k, w1v, w2k, w2v):
    n = kvc.shape[0]
    b = n // SEQ
    full = lambda a: pl.BlockSpec(a.shape, lambda i: (0,) * a.ndim)
    kc, vc = kvc, kvc
    return pl.pallas_call(
        _compress_kernel,
        grid=(b,),
        in_specs=[
            pl.BlockSpec((SEQ, LANES), lambda i: (i, 0)),
            pl.BlockSpec((SEQ, LANES), lambda i: (i, 1)),
            full(pek), full(pev), full(w1k), full(w1v), full(w2k), full(w2v),
        ],
        out_specs=[pl.BlockSpec((1, N_CMP_PAD, 2 * LANES), lambda i: (i, 0, 0)),
                   pl.BlockSpec((1, N_CMP_PAD, LANES), lambda i: (i, 0, 0))],
        out_shape=[jax.ShapeDtypeStruct((b, N_CMP_PAD, 2 * LANES), BF16),
                   jax.ShapeDtypeStruct((b, N_CMP_PAD, LANES), BF16)],
        compiler_params=_params(("parallel",), COMPRESS_VMEM),
        name="compress",
    )(kc, vc, pek, pev, w1k, w1v, w2k, w2v)


def _attend_pipelined(tasks):
    n = len(tasks)
    scores, probs, outs = [None] * n, [None] * n, [None] * n
    for i in range(n + 2):
        if i < n:
            q, k, _, _ = tasks[i]
            scores[i] = _dot_nt(q, k)
        if 1 <= i <= n:
            _, _, _, chunk_masks = tasks[i - 1]
            cols = []
            for c, allowed in enumerate(chunk_masks):
                s = scores[i - 1][:, c * KV_CHUNK:(c + 1) * KV_CHUNK]
                cols.append(s if allowed is None else jnp.where(allowed, s, NEG_INF))
            s = cols[0] if len(cols) == 1 else jnp.concatenate(cols, axis=1)
            p = jnp.exp2(s - jnp.max(s, axis=-1, keepdims=True))
            probs[i - 1] = (p.astype(BF16), 1.0 / jnp.sum(p, axis=-1, keepdims=True))
            scores[i - 1] = None
        if 2 <= i:
            p, inv = probs[i - 2]
            outs[i - 2] = _dot(p, tasks[i - 2][2]) * inv
            probs[i - 2] = None
    return outs


def _head_queries(q_ref, g, slope_scale, sel_lanes):
    rows = q_ref.shape[0]
    lane = lax.broadcasted_iota(I32, (rows, LANES), 1)
    qf = q_ref[...].astype(F32)
    parts = []
    for r in range(NSA_REP):
        slope = slope_scale * jnp.where(g == 0, 2.0 ** -(r + 1), 2.0 ** -(r + 1 + NSA_REP)).astype(F32)
        tile = qf[:, (r // 2) * LANES:(r // 2 + 1) * LANES]
        if r % 2:
            tile = pltpu.roll(tile, HEAD_DIM, axis=1)
        ext = jnp.where((lane >= X_POS) & (lane < X_POS + N_POS_TERMS), slope, sel_lanes)
        parts.append(jnp.where(lane < HEAD_DIM, tile, ext).astype(BF16))
    return parts


def _head_layout(heads, g):
    low = lax.broadcasted_iota(I32, heads[0].shape, 1) < HEAD_DIM
    tiles = []
    for pair in range(NSA_REP // 2):
        even, odd = heads[2 * pair], heads[2 * pair + 1]
        left = jnp.where(g == 0, even, pltpu.roll(even, HEAD_DIM, axis=1))
        right = jnp.where(g == 0, pltpu.roll(odd, HEAD_DIM, axis=1), odd)
        tiles.append(jnp.where(low, left, right))
    return tiles


def _nsa_select_kernel(q_ref, misc_ref, kca_ref, vc2_ref, ocmp_ref, drop_ref):
    g = pl.program_id(1)
    rows = SEL_ROWS
    t0 = pl.program_id(2) * rows
    lane = lax.broadcasted_iota(I32, (rows, LANES), 1)
    tq = t0 + lax.broadcasted_iota(I32, (rows, LANES), 0)

    q4 = jnp.concatenate(_head_queries(q_ref, g, 0.5, 0.0), axis=0)
    s_c = _dot_nt(q4, kca_ref[0])
    valid_c = lane * CMP_STRIDE + (CMP_BLOCK - 1) <= tq
    p_sum = jnp.zeros((rows, LANES), F32)
    p_parts = []
    for r in range(NSA_REP):
        s = jnp.where(valid_c, s_c[r * rows:(r + 1) * rows], NEG_INF)
        e = jnp.exp2(s - jnp.max(s, axis=-1, keepdims=True))
        p = jnp.where(valid_c, e / jnp.sum(e, axis=-1, keepdims=True), 0.0)
        p_sum += p
        p_parts.append(p.astype(BF16))
    o_cmp = _dot(jnp.concatenate(p_parts, axis=0), vc2_ref[0])
    gates = jax.nn.sigmoid(misc_ref[...])
    tiles = _head_layout([gates[:, 3 * r:3 * r + 1] * o_cmp[r * rows:(r + 1) * rows] for r in range(NSA_REP)], g)
    for i, tile in enumerate(tiles):
        ocmp_ref[:, i * LANES:(i + 1) * LANES] = tile.astype(BF16)

    ov_j = lax.broadcasted_iota(I32, (LANES, LANES), 0)
    ov_i = lax.broadcasted_iota(I32, (LANES, LANES), 1)
    ov_t = ((ov_i < 4 * ov_j + 4) & (ov_i > 4 * ov_j - 2) & (ov_j < N_SEL)).astype(F32)
    imp_t = _dot_nt(ov_t, p_sum, precision=lax.Precision.HIGHEST)[:N_SEL]
    j_blk = lax.broadcasted_iota(I32, (N_SEL, rows), 0)
    cur = (t0 + lax.broadcasted_iota(I32, (N_SEL, rows), 1)) >> SEL_SHIFT
    forced = (j_blk == 0) | (j_blk == cur) | (j_blk == cur - 1)
    imp_t = jnp.where(forced, FORCE, imp_t)
    imp_t = jnp.where(j_blk <= cur, imp_t, -FORCE)
    rank = jnp.zeros((N_SEL, rows), I32)
    for jp in range(N_SEL):
        row = imp_t[jp:jp + 1, :]
        beats = (row > imp_t) | ((row == imp_t) & (j_blk > jp))
        rank += beats.astype(I32)
    drop_t = jnp.where(rank < SEL_TOPK, 0.0, NEG_INF)
    drop_t = jnp.concatenate([drop_t, jnp.zeros((LANES - N_SEL, rows), F32)], axis=0)
    drop_q = jnp.concatenate([drop_t[:, i * LANES:(i + 1) * LANES].T for i in range(rows // LANES)], axis=0)
    drop_ref[...] = pltpu.roll(drop_q, X_SEL, axis=1).astype(BF16)


def _nsa_kernel(q_ref, misc_ref, ocmp_ref, drop_ref, ksa_ref, vs_ref, kwa_ref, vw_ref, o_ref):
    for qb in range(N_QB):
        @pl.when(pl.program_id(2) == qb)
        def _(qb=qb):
            _nsa_body(qb, q_ref, misc_ref, ocmp_ref, drop_ref, ksa_ref, vs_ref, kwa_ref, vw_ref, o_ref)


def _nsa_body(qb, q_ref, misc_ref, ocmp_ref, drop_ref, ksa_ref, vs_ref, kwa_ref, vw_ref, o_ref):
    g = pl.program_id(1)
    t0 = qb * Q_BLOCK

    def dist_to(c):
        return (t0 + lax.broadcasted_iota(I32, (Q_BLOCK, KV_CHUNK), 0)
                - (c * KV_CHUNK + lax.broadcasted_iota(I32, (Q_BLOCK, KV_CHUNK), 1)))

    c_diag = qb
    dist_diag = dist_to(c_diag)
    n_slc = (c_diag + 1) * KV_CHUNK

    slc_masks = [None] * c_diag + [dist_diag >= 0]
    q_sel = _head_queries(q_ref, g, 1.0, drop_ref[...].astype(F32))

    c_first = max(c_diag - WINDOW // KV_CHUNK, 0)
    win_masks = [None] * (c_diag - c_first + 1)
    win_masks[-1] = dist_diag >= 0
    if c_diag - c_first == WINDOW // KV_CHUNK:
        win_masks[0] = dist_to(c_first) < WINDOW
    win_rows = slice(c_first * KV_CHUNK, n_slc)
    q_win = _head_queries(q_ref, g, 1.0, 0.0)

    both = lambda masks: [None if m is None else jnp.concatenate([m, m], axis=0) for m in masks]
    tasks = []
    for r in range(0, NSA_REP, 2):
        tasks.append((jnp.concatenate(q_win[r:r + 2], axis=0), kwa_ref[win_rows, :], vw_ref[win_rows, :],
                      both(win_masks)))
    split = lambda o: [o[:Q_BLOCK], o[Q_BLOCK:]]
    if c_diag < SLC_SINGLE_HEAD_FROM:
        for r in range(0, NSA_REP, 2):
            tasks.append((jnp.concatenate(q_sel[r:r + 2], axis=0), ksa_ref[:n_slc, :], vs_ref[:n_slc, :],
                          both(slc_masks)))
        outs = _attend_pipelined(tasks)
        o_slc = split(outs[2]) + split(outs[3])
    else:
        for r in range(NSA_REP):
            tasks.append((q_sel[r], ksa_ref[:n_slc, :], vs_ref[:n_slc, :], slc_masks))
        outs = _attend_pipelined(tasks)
        o_slc = outs[2:]
    o_win = split(outs[0]) + split(outs[1])

    gates = jax.nn.sigmoid(misc_ref[...])
    heads = [gates[:, 3 * r + 1:3 * r + 2] * o_slc[r] + gates[:, 3 * r + 2:3 * r + 3] * o_win[r]
             for r in range(NSA_REP)]
    for i, tile in enumerate(_head_layout(heads, g)):
        cols = slice(i * LANES, (i + 1) * LANES)
        o_ref[:, cols] = (tile + ocmp_ref[:, cols].astype(F32)).astype(BF16)


def _nsa(q, misc, kca, vc2, ksa, kwa, vsw):
    n = q.shape[0]
    b = n // SEQ
    gw = NSA_REP * HEAD_DIM
    assert WINDOW % KV_CHUNK == 0 and KV_CHUNK == Q_BLOCK and SEQ % SEL_ROWS == 0
    n_sel = SEQ // SEL_ROWS
    sel_rows = lambda width: pl.BlockSpec((SEL_ROWS, width), lambda i, g, j: (i * n_sel + j, g))
    ocmp, drop = pl.pallas_call(
        _nsa_select_kernel,
        grid=(b, NSA_GROUPS, n_sel),
        in_specs=[sel_rows(gw), sel_rows(LANES),
                  pl.BlockSpec((1, N_CMP_PAD, LANES), lambda i, g, j: (i, 0, g)),
                  pl.BlockSpec((1, N_CMP_PAD, LANES), lambda i, g, j: (i, 0, 0))],
        out_specs=[sel_rows(gw), sel_rows(LANES)],
        out_shape=[jax.ShapeDtypeStruct((n, NSA_GROUPS * gw), BF16),
                   jax.ShapeDtypeStruct((n, NSA_GROUPS * LANES), BF16)],
        compiler_params=_params(("parallel", "parallel", "parallel"), SELECT_VMEM),
        name="nsa_select",
    )(q, misc, kca, vc2)

    q_rows = lambda width: pl.BlockSpec((Q_BLOCK, width), lambda i, g, c: (i * N_QB + c, g))
    return pl.pallas_call(
        _nsa_kernel,
        grid=(b, NSA_GROUPS, N_QB),
        in_specs=[q_rows(gw), q_rows(LANES), q_rows(gw), q_rows(LANES),
                  pl.BlockSpec((SEQ, LANES), lambda i, g, c: (i, g)),
                  pl.BlockSpec((SEQ, LANES), lambda i, g, c: (i, 0)),
                  pl.BlockSpec((SEQ, LANES), lambda i, g, c: (i, g)),
                  pl.BlockSpec((SEQ, LANES), lambda i, g, c: (i, 1))],
        out_specs=q_rows(gw),
        out_shape=jax.ShapeDtypeStruct((n, NSA_GROUPS * gw), BF16),
        compiler_params=_params(("parallel", "parallel", "arbitrary"), NSA_VMEM),
        name="nsa",
    )(q, misc, ocmp, drop, ksa, vsw, kwa, vsw)


def _gla_kernel(q_ref, k_ref, v_ref, r_ref, misc_ref, wa_ref, ba_ref, gn_ref, o_ref):
    c, blk = GLA_CHUNK, GLA_BLOCK
    n_c = blk // c
    lane = lax.broadcasted_iota(I32, (blk, LANES), 1)
    row = lax.broadcasted_iota(I32, (blk, LANES), 0)
    in_chunk = row & (c - 1)
    chunk_of_row = row >> GLA_CHUNK_SHIFT
    r2 = lax.broadcasted_iota(I32, (2 * blk, blk), 0) & (blk - 1)
    c2 = lax.broadcasted_iota(I32, (2 * blk, blk), 1)
    intra = (r2 >= c2) & ((r2 >> GLA_CHUNK_SHIFT) == (c2 >> GLA_CHUNK_SHIFT))
    lane_s = lax.broadcasted_iota(I32, (GLA_DV, LANES), 1)
    n_blk = SEQ // blk

    def prep(i_blk):
        rows = slice(i_blk * blk, (i_blk + 1) * blk)
        la = _dot_3pass(misc_ref[rows, :], wa_ref[...]) + ba_ref[...]
        b = (jnp.minimum(la, 0.0) - jnp.log(1.0 + jnp.exp(-jnp.abs(la)))) * (1.0 / GLA_TAU)
        shift = 1
        while shift < c:
            b = b + jnp.where(in_chunk >= shift, pltpu.roll(b, shift, axis=0), 0.0)
            shift *= 2
        b3 = b.reshape(n_c, c, LANES)
        b_last = b3[:, c - 1:c, :]
        k = k_ref[rows, :]
        q_in = q_ref[rows, :] * (GLA_DK ** -0.5) * jnp.exp(b)
        k_in = (k * jnp.exp(-b)).astype(BF16)
        k_st = (k.reshape(n_c, c, LANES) * jnp.exp(b_last - b3)).reshape(blk, LANES)
        decay = jnp.exp(b_last)
        v = v_ref[rows, :]

        q2 = jnp.concatenate([jnp.where(lane < GLA_DK, q_in, 0.0), jnp.where(lane >= GLA_DK, q_in, 0.0)],
                             axis=0).astype(BF16)
        k_cols = jnp.concatenate([jnp.where(chunk_of_row == i_c, k_st, 0.0) for i_c in range(n_c)], axis=1)
        return q2, k_in, k_cols.astype(BF16), v, decay

    def chunk_parallel(q2, k_in, k_cols, v, decay):
        a = jnp.where(intra, _dot_nt(q2, k_in), 0.0).astype(BF16)
        o_intra = _dot(a, v)
        inc = _dot_tn(v, k_cols)
        return q2, o_intra, inc, decay

    per_blk = []
    ready = prep(0)
    for i_blk in range(n_blk):
        upcoming = prep(i_blk + 1) if i_blk + 1 < n_blk else None
        per_blk.append(chunk_parallel(*ready))
        ready = upcoming

    st = jnp.zeros((GLA_DV, LANES), F32)
    state_before = []
    for _, _, inc, decay in per_blk:
        for i_c in range(n_c):
            state_before.append(st.astype(BF16))
            cols = slice(i_c * LANES, (i_c + 1) * LANES)
            st = st * decay[i_c] + jnp.where(lane_s < GLA_DK, inc[:GLA_DV, cols], inc[GLA_DV:, cols])

    for i_blk, (q2, o_intra, _, _) in enumerate(per_blk):
        rows = slice(i_blk * blk, (i_blk + 1) * blk)
        o_inter = []
        for i_c in range(n_c):
            q_c = jnp.concatenate([q2[i_c * c:(i_c + 1) * c], q2[blk + i_c * c:blk + (i_c + 1) * c]], axis=0)
            o_inter.append(_dot_nt(q_c, state_before[i_blk * n_c + i_c]))
        for h in range(2):
            cols = slice(h * GLA_DV, (h + 1) * GLA_DV)
            o = o_intra[h * blk:(h + 1) * blk, cols] + jnp.concatenate(
                [o_inter[i_c][h * c:(h + 1) * c] for i_c in range(n_c)], axis=0)
            gate = r_ref[rows, cols]
            o_ref[rows, cols] = (_rms(o, gn_ref[...]) * (gate * jax.nn.sigmoid(gate))).astype(BF16)


def _gla(gq, gk, gv, gr, misc, wa, ba, gn):
    n = gq.shape[0]
    b = n // SEQ
    return pl.pallas_call(
        _gla_kernel,
        grid=(b, GLA_HEADS // 2),
        in_specs=[
            pl.BlockSpec((SEQ, LANES), lambda i, p: (i, p)),
            pl.BlockSpec((SEQ, LANES), lambda i, p: (i, p)),
            pl.BlockSpec((SEQ, 2 * GLA_DV), lambda i, p: (i, p)),
            pl.BlockSpec((SEQ, 2 * GLA_DV), lambda i, p: (i, p)),
            pl.BlockSpec((SEQ, LANES), lambda i, p: (i, 0)),
            pl.BlockSpec((LANES, LANES), lambda i, p: (0, p)),
            pl.BlockSpec((1, LANES), lambda i, p: (0, p)),
            pl.BlockSpec((1, GLA_DV), lambda i, p: (0, 0)),
        ],
        out_specs=pl.BlockSpec((SEQ, 2 * GLA_DV), lambda i, p: (i, p)),
        out_shape=jax.ShapeDtypeStruct((n, GLA_HEADS * GLA_DV), BF16),
        compiler_params=_params(("parallel", "parallel"), GLA_VMEM),
        name="gla",
    )(gq, gk, gv, gr, misc, wa, ba, gn)


def _compress_weights(pe, w1, w2, group_stride):
    hd, hid = HEAD_DIM, CMP_HIDDEN
    pe2 = jnp.concatenate([pe, pe], axis=1)
    w1 = w1.reshape(CMP_BLOCK, hd, hid)
    z1 = jnp.zeros_like(w1)
    w1e = jnp.concatenate([jnp.concatenate([w1, z1], axis=2), jnp.concatenate([z1, w1], axis=2)], axis=1)
    w2e = jnp.zeros((NSA_GROUPS * hid, NSA_GROUPS * group_stride), w2.dtype)
    for g in range(NSA_GROUPS):
        w2e = w2e.at[g * hid:(g + 1) * hid, g * group_stride:g * group_stride + hd].set(w2)
    return pe2, w1e.astype(BF16), w2e.astype(BF16)


def kernel(x, ffn1_norm, ffn1_w_gate, ffn1_w_up, ffn1_w_down, mix_norm, w_in, nsa_pe_k, nsa_w1_k, nsa_w2_k,
           nsa_pe_v, nsa_w1_v, nsa_w2_v, gla_w_a2, gla_b_a, gla_norm, w_out, ffn2_norm, ffn2_w_gate,
           ffn2_w_up, ffn2_w_down, final_norm):
    bsz, seq, d = x.shape
    assert (seq, d) == (SEQ, D_MODEL) and ffn1_norm.shape[0] == 1
    n = bsz * seq
    xf = x.reshape(n, d)
    row = lambda v: v.reshape(1, -1).astype(F32)
    bf = lambda w: w.astype(BF16)
    ones = jnp.ones((1, d), F32)

    x1 = _ffn(xf, (), row(ffn1_norm[0]), bf(ffn1_w_gate[0]), bf(ffn1_w_up[0]), bf(ffn1_w_down[0]), ones,
              final_norm=False)

    q, kvc, ksa, kwa, vsw, misc, gq, gk, gv, gr = _inproj(x1, row(mix_norm[0]), w_in[0])

    pek, w1k, w2k = _compress_weights(nsa_pe_k[0], nsa_w1_k[0], nsa_w2_k[0], LANES)
    pev, w1v, w2v = _compress_weights(nsa_pe_v[0], nsa_w1_v[0], nsa_w2_v[0], HEAD_DIM)
    kca, vc2 = _compress(kvc, pek, pev, w1k, w1v, w2k, w2v)

    o_nsa = _nsa(q, misc, kca, vc2, ksa, kwa, vsw)

    wa = jnp.zeros((LANES, GLA_HEADS * GLA_DK), F32).at[MISC_GA_OFF:MISC_GA_OFF + GLA_LOWRANK].set(gla_w_a2[0])
    o_gla = _gla(gq, gk, gv, gr, misc, wa, row(gla_b_a[0]), row(gla_norm[0]))

    d_nsa = NSA_HEADS * HEAD_DIM
    w_o = bf(w_out[0])
    out = _ffn(x1, (o_nsa, o_gla, w_o[:d_nsa], w_o[d_nsa:]), row(ffn2_norm[0]), bf(ffn2_w_gate[0]),
               bf(ffn2_w_up[0]), bf(ffn2_w_down[0]), row(final_norm), final_norm=True)
    return out.reshape(bsz, seq, d)
```

```python
import functools

import jax
import jax.numpy as jnp
from jax import lax
from jax.experimental import pallas as pl
from jax.experimental.pallas import tpu as pltpu

F32 = jnp.float32
BF16 = jnp.bfloat16
I32 = jnp.int32

D_MODEL = 1024
SEQ = 2048
D_FF = 2816
EPS = 1e-6
NEG_INF = -1e30
FORCE = 1e9

NSA_HEADS = 8
NSA_GROUPS = 2
NSA_REP = NSA_HEADS // NSA_GROUPS
HEAD_DIM = 64
CMP_BLOCK = 32
CMP_STRIDE = 16
CMP_HIDDEN = 128
N_CMP_PAD = SEQ // CMP_STRIDE
SEL_BLOCK = 64
SEL_SHIFT = 6
assert 1 << SEL_SHIFT == SEL_BLOCK
N_SEL = SEQ // SEL_BLOCK
SEL_TOPK = 8
WINDOW = 512
Q_BLOCK = 256
N_QB = SEQ // Q_BLOCK
KV_CHUNK = Q_BLOCK
SEL_ROWS = 1024
SLC_SINGLE_HEAD_FROM = 3

GLA_HEADS = 4
GLA_DK = 64
GLA_DV = 128
GLA_CHUNK = 64
GLA_CHUNK_SHIFT = 6
assert 1 << GLA_CHUNK_SHIFT == GLA_CHUNK
GLA_BLOCK = 256
GLA_LOWRANK = 16
GLA_TAU = 16.0

LANES = 128
V7X_VMEM_BYTES = 64 * 1024 * 1024
MIB = 1 << 20
VMEM_RESERVE = 8 * MIB
VMEM_TEMPS = 8 * MIB
COMPRESS_VMEM, SELECT_VMEM, NSA_VMEM, GLA_VMEM = 32 * MIB, 32 * MIB, 48 * MIB, 40 * MIB

C_Q = (0, 512)
C_KC = (512, 640)
C_VC = (640, 768)
C_KSW = (768, 1024)
C_VS = (1024, 1152)
C_VW = (1152, 1280)
C_MISC = (1280, 1536)
C_GQ = (1536, 1792)
C_GK = (1792, 2048)
C_GV = (2048, 2560)
C_GR = (2560, 3072)
D_IN_PAD = 3072
KEY_OUT_WIDTH = NSA_GROUPS * LANES
MISC_GA_OFF = 12

LOG2E = 1.4426950408889634
X_SEL = HEAD_DIM
X_POS = HEAD_DIM + N_SEL
N_POS_TERMS = 3
assert X_POS + N_POS_TERMS <= LANES


def _params(sem, vmem_bytes):
    return pltpu.CompilerParams(dimension_semantics=sem,
                                vmem_limit_bytes=min(int(vmem_bytes), V7X_VMEM_BYTES - VMEM_RESERVE))


def _rms(x, g):
    return x * lax.rsqrt(jnp.mean(x * x, axis=-1, keepdims=True) + EPS) * g


def _dot(a, b, **kw):
    return jnp.dot(a, b, preferred_element_type=F32, **kw)


def _dot_nt(a, b, **kw):
    return lax.dot_general(a, b, (((1,), (1,)), ((), ())), preferred_element_type=F32, **kw)


def _dot_3pass(a, b):
    a_hi, b_hi = a.astype(BF16), b.astype(BF16)
    a_lo = (a - a_hi.astype(F32)).astype(BF16)
    b_lo = (b - b_hi.astype(F32)).astype(BF16)
    return _dot(a_hi, b_hi) + _dot(a_lo, b_hi) + _dot(a_hi, b_lo)


def _dot_tn(a, b, **kw):
    return lax.dot_general(a, b, (((0,), (0,)), ((), ())), preferred_element_type=F32, **kw)


def _ffn_kernel(*refs, tf, final_norm, mixer_out):
    if mixer_out:
        x_ref, a_ref, b_ref, wa_ref, wb_ref, g_ref, wg_ref, wu_ref, wd_ref, fg_ref, o_ref, act_ref = refs
        x = x_ref[...] + _dot(a_ref[...], wa_ref[...]) + _dot(b_ref[...], wb_ref[...])
    else:
        x_ref, g_ref, wg_ref, wu_ref, wd_ref, fg_ref, o_ref, act_ref = refs
        x = x_ref[...]
    h = _rms(x, g_ref[...]).astype(BF16)
    for c in range(D_FF // tf):
        cols = slice(c * tf, (c + 1) * tf)
        gate = _dot(h, wg_ref[:, cols])
        up = _dot(h, wu_ref[:, cols])
        act_ref[:, cols] = ((gate * jax.nn.sigmoid(gate)) * up).astype(BF16)
    y = x + 0.5 * _dot(act_ref[...], wd_ref[...])
    if final_norm:
        y = _rms(y, fg_ref[...])
    o_ref[...] = y


def _ffn(x, mixer, gain, wg, wu, wd, final_gain, *, final_norm, tm=1024, tf=256):
    n, d = x.shape
    assert D_FF % tf == 0
    resident = lambda shape: pl.BlockSpec(shape, lambda i: (0, 0), pipeline_mode=pl.Buffered(1))
    rows = lambda width: pl.BlockSpec((tm, width), lambda i: (i, 0))
    vmem = 2 * 2 * tm * d * 4 + 3 * d * D_FF * 2 + tm * D_FF * 2 + tm * d * 2 + 4 * tm * tf * 4 + VMEM_TEMPS
    mix_specs = []
    if mixer:
        a, b, w_a, w_b = mixer
        mix_specs = [rows(a.shape[1]), rows(b.shape[1]), resident(w_a.shape), resident(w_b.shape)]
        vmem += 2 * tm * (a.shape[1] + b.shape[1]) * 2 + (w_a.size + w_b.size) * 2
    return pl.pallas_call(
        functools.partial(_ffn_kernel, tf=tf, final_norm=final_norm, mixer_out=bool(mixer)),
        grid=(n // tm,),
        in_specs=[rows(d)] + mix_specs + [
            resident((1, d)),
            resident((d, D_FF)),
            resident((d, D_FF)),
            resident((D_FF, d)),
            resident((1, d)),
        ],
        out_specs=rows(d),
        out_shape=jax.ShapeDtypeStruct((n, d), F32),
        scratch_shapes=[pltpu.VMEM((tm, D_FF), BF16)],
        compiler_params=_params(("parallel",), vmem),
        name="ffn",
    )(x, *mixer, gain, wg, wu, wd, final_gain)


def _key_extras(pos, lane, with_block):
    lg = lane & (LANES - 1)
    rest = pos.astype(F32) * LOG2E
    ext = jnp.zeros(rest.shape, F32)
    for i in range(N_POS_TERMS):
        term = rest.astype(BF16).astype(F32) if i + 1 < N_POS_TERMS else rest
        ext = jnp.where(lg == X_POS + i, term, ext)
        rest = rest - term
    if with_block:
        ext = jnp.where((lg >= X_SEL) & (lg < X_POS) & ((pos >> SEL_SHIFT) == lg - X_SEL), 1.0, ext)
    return ext


def _w_in_pieces():
    hd, n_gate = HEAD_DIM, 3 * NSA_REP
    src = lambda lo, hi: ((lo, hi), hi - lo)
    pad = lambda k: (None, k)
    pieces = [src(0, 512), src(512, 640), src(640, 768),
              src(768, 768 + hd), src(1024, 1024 + hd), src(768 + hd, 768 + 2 * hd), src(1024 + hd, 1024 + 2 * hd),
              src(896, 1024), src(1152, 1280),
              src(1280, 1280 + n_gate), src(2840, 2856), pad(LANES - n_gate - GLA_LOWRANK),
              src(1280 + n_gate, 1304), pad(LANES - n_gate),
              src(1304, 1560), src(1560, 1816), src(1816, 2328), src(2328, 2840)]
    assert sum(k for _, k in pieces) == D_IN_PAD
    return pieces


def _stage_w_in(w_ref, wp_ref):
    n_src = w_ref.shape[1]
    tiles, cur, room = [], [], LANES
    for rng, k in _w_in_pieces():
        lo = rng[0] if rng else None
        while k:
            take = min(k, room)
            cur.append((lo, take))
            lo = None if lo is None else lo + take
            k, room = k - take, room - take
            if room == 0:
                tiles.append(cur)
                cur, room = [], LANES
    for t, parts in enumerate(tiles):
        vals = []
        for lo, k in parts:
            if lo is None:
                vals.append(jnp.zeros((w_ref.shape[0], k), F32))
                continue
            a_lo = lo // LANES * LANES
            a_hi = min(-(-(lo + k) // LANES) * LANES, n_src)
            vals.append(w_ref[:, a_lo:a_hi][:, lo - a_lo:lo - a_lo + k])
        tile = vals[0] if len(vals) == 1 else jnp.concatenate(vals, axis=1)
        wp_ref[:, t * LANES:(t + 1) * LANES] = tile.astype(BF16)


def _inproj_kernel(x_ref, g_ref, w_ref, q_ref, kvc_ref, ksa_ref, kwa_ref, vsw_ref, misc_ref,
                   gq_ref, gk_ref, gv_ref, gr_ref, wp_ref, *, tm):
    @pl.when(pl.program_id(0) == 0)
    def _():
        _stage_w_in(w_ref, wp_ref)

    h = _rms(x_ref[...], g_ref[...]).astype(BF16)

    def proj(c):
        return _dot(h, wp_ref[:, c[0]:c[1]])

    shape = (tm, 2 * LANES)
    lane = lax.broadcasted_iota(I32, shape, 1)
    pos = (pl.program_id(0) * tm + lax.broadcasted_iota(I32, shape, 0)) & (SEQ - 1)
    is_key = (lane & (LANES - 1)) < HEAD_DIM

    q_ref[...] = (proj(C_Q) * (HEAD_DIM ** -0.5 * LOG2E)).astype(BF16)
    assert C_KC[1] == C_VC[0] and C_VS[1] == C_VW[0]
    kvc_ref[...] = proj((C_KC[0], C_VC[1]))
    ksw = proj(C_KSW)
    kws = jnp.concatenate([pltpu.roll(ksw[:, g * LANES:(g + 1) * LANES], HEAD_DIM, axis=1)
                           for g in range(NSA_GROUPS)], axis=1)
    ksa_ref[...] = jnp.where(is_key, ksw, _key_extras(pos, lane, True)).astype(BF16)
    kwa_ref[...] = jnp.where(is_key, kws, _key_extras(pos, lane, False)).astype(BF16)
    vsw_ref[...] = proj((C_VS[0], C_VW[1])).astype(BF16)
    misc_ref[...] = proj(C_MISC)
    gq_ref[...] = proj(C_GQ)
    gk_ref[...] = proj(C_GK)
    gv_ref[...] = proj(C_GV).astype(BF16)
    gr_ref[...] = proj(C_GR)


def _inproj(x, gain, w, tm=1024):
    n, d = x.shape
    assert SEQ % tm == 0 and SEQ & (SEQ - 1) == 0
    width = lambda c: c[1] - c[0]
    outs = [(width(C_Q), BF16), (width(C_KC) + width(C_VC), F32), (KEY_OUT_WIDTH, BF16), (KEY_OUT_WIDTH, BF16),
            (width(C_VS) + width(C_VW), BF16), (width(C_MISC), F32),
            (width(C_GQ), F32), (width(C_GK), F32), (width(C_GV), BF16), (width(C_GR), F32)]
    vmem = (2 * tm * d * 4 + w.size * 4 + d * D_IN_PAD * 2 + 2 * tm * (D_IN_PAD + KEY_OUT_WIDTH) * 4
            + 4 * d * LANES * 4 + VMEM_TEMPS)
    return pl.pallas_call(
        functools.partial(_inproj_kernel, tm=tm),
        grid=(n // tm,),
        in_specs=[
            pl.BlockSpec((tm, d), lambda i: (i, 0)),
            pl.BlockSpec((1, d), lambda i: (0, 0)),
            pl.BlockSpec(w.shape, lambda i: (0, 0), pipeline_mode=pl.Buffered(1)),
        ],
        out_specs=[pl.BlockSpec((tm, w_out), lambda i: (i, 0)) for w_out, _ in outs],
        out_shape=[jax.ShapeDtypeStruct((n, w_out), dt) for w_out, dt in outs],
        scratch_shapes=[pltpu.VMEM((d, D_IN_PAD), BF16)],
        compiler_params=_params(("arbitrary",), vmem),
        name="inproj",
    )(x, gain, w)


def _compress_kernel(kc_ref, vc_ref, pek_ref, pev_ref, w1k_ref, w1v_ref, w2k_ref, w2v_ref, kca_ref, vc2_ref):
    half = CMP_BLOCK // 2

    def hidden(x_ref, pe_ref, w1_ref):
        acc_a = jnp.zeros((N_CMP_PAD, 2 * CMP_HIDDEN), F32)
        acc_b = jnp.zeros((N_CMP_PAD, 2 * CMP_HIDDEN), F32)
        for l in range(half):
            rows = x_ref[pl.ds(l, N_CMP_PAD, stride=CMP_STRIDE), :]
            acc_a += _dot((rows + pe_ref[l:l + 1, :]).astype(BF16), w1_ref[l])
            acc_b += _dot((rows + pe_ref[half + l:half + l + 1, :]).astype(BF16), w1_ref[half + l])
        pre = acc_a + pltpu.roll(acc_b, N_CMP_PAD - 1, axis=0)
        return jax.nn.gelu(pre, approximate=True).astype(BF16)

    kc = _dot(hidden(kc_ref, pek_ref, w1k_ref), w2k_ref[...])
    vc = _dot(hidden(vc_ref, pev_ref, w1v_ref), w2v_ref[...])
    lane = lax.broadcasted_iota(I32, kc.shape, 1)
    row = lax.broadcasted_iota(I32, kc.shape, 0)
    kc = jnp.where((lane & (LANES - 1)) < HEAD_DIM, kc, _key_extras(2 * CMP_STRIDE * row + CMP_BLOCK - 1, lane, False))
    kca_ref[0] = jnp.where(row < N_CMP_PAD - 1, kc, 0.0).astype(BF16)
    vc2_ref[0] = jnp.where(row[:, :LANES] < N_CMP_PAD - 1, vc, 0.0).astype(BF16)


def _compress(kvc, pek, pev, w1k, w1v, w2k, w2v):
    n = kvc.shape[0]
    b = n // SEQ
    full = lambda a: pl.BlockSpec(a.shape, lambda i: (0,) * a.ndim)
    kc, vc = kvc, kvc
    return pl.pallas_call(
        _compress_kernel,
        grid=(b,),
        in_specs=[
            pl.BlockSpec((SEQ, LANES), lambda i: (i, 0)),
            pl.BlockSpec((SEQ, LANES), lambda i: (i, 1)),
            full(pek), full(pev), full(w1k), full(w1v), full(w2k), full(w2v),
        ],
        out_specs=[pl.BlockSpec((1, N_CMP_PAD, 2 * LANES), lambda i: (i, 0, 0)),
                   pl.BlockSpec((1, N_CMP_PAD, LANES), lambda i: (i, 0, 0))],
        out_shape=[jax.ShapeDtypeStruct((b, N_CMP_PAD, 2 * LANES), BF16),
                   jax.ShapeDtypeStruct((b, N_CMP_PAD, LANES), BF16)],
        compiler_params=_params(("parallel",), COMPRESS_VMEM),
        name="compress",
    )(kc, vc, pek, pev, w1k, w1v, w2k, w2v)


def _attend_pipelined(tasks):
    n = len(tasks)
    scores, probs, outs = [None] * n, [None] * n, [None] * n
    for i in range(n + 2):
        if i < n:
            q, k, _, _ = tasks[i]
            scores[i] = _dot_nt(q, k)
        if 1 <= i <= n:
            _, _, _, chunk_masks = tasks[i - 1]
            cols = []
            for c, allowed in enumerate(chunk_masks):
                s = scores[i - 1][:, c * KV_CHUNK:(c + 1) * KV_CHUNK]
                cols.append(s if allowed is None else jnp.where(allowed, s, NEG_INF))
            s = cols[0] if len(cols) == 1 else jnp.concatenate(cols, axis=1)
            p = jnp.exp2(s - jnp.max(s, axis=-1, keepdims=True))
            probs[i - 1] = (p.astype(BF16), 1.0 / jnp.sum(p, axis=-1, keepdims=True))
            scores[i - 1] = None
        if 2 <= i:
            p, inv = probs[i - 2]
            outs[i - 2] = _dot(p, tasks[i - 2][2]) * inv
            probs[i - 2] = None
    return outs


def _head_queries(q_ref, g, slope_scale, sel_lanes):
    rows = q_ref.shape[0]
    lane = lax.broadcasted_iota(I32, (rows, LANES), 1)
    qf = q_ref[...].astype(F32)
    parts = []
    for r in range(NSA_REP):
        slope = slope_scale * jnp.where(g == 0, 2.0 ** -(r + 1), 2.0 ** -(r + 1 + NSA_REP)).astype(F32)
        tile = qf[:, (r // 2) * LANES:(r // 2 + 1) * LANES]
        if r % 2:
            tile = pltpu.roll(tile, HEAD_DIM, axis=1)
        ext = jnp.where((lane >= X_POS) & (lane < X_POS + N_POS_TERMS), slope, sel_lanes)
        parts.append(jnp.where(lane < HEAD_DIM, tile, ext).astype(BF16))
    return parts


def _head_layout(heads, g):
    low = lax.broadcasted_iota(I32, heads[0].shape, 1) < HEAD_DIM
    tiles = []
    for pair in range(NSA_REP // 2):
        even, odd = heads[2 * pair], heads[2 * pair + 1]
        left = jnp.where(g == 0, even, pltpu.roll(even, HEAD_DIM, axis=1))
        right = jnp.where(g == 0, pltpu.roll(odd, HEAD_DIM, axis=1), odd)
        tiles.append(jnp.where(low, left, right))
    return tiles


def _nsa_select_kernel(q_ref, misc_ref, kca_ref, vc2_ref, ocmp_ref, drop_ref):
    g = pl.program_id(1)
    rows = SEL_ROWS
    t0 = pl.program_id(2) * rows
    lane = lax.broadcasted_iota(I32, (rows, LANES), 1)
    tq = t0 + lax.broadcasted_iota(I32, (rows, LANES), 0)

    q4 = jnp.concatenate(_head_queries(q_ref, g, 0.5, 0.0), axis=0)
    s_c = _dot_nt(q4, kca_ref[0])
    valid_c = lane * CMP_STRIDE + (CMP_BLOCK - 1) <= tq
    p_sum = jnp.zeros((rows, LANES), F32)
    p_parts = []
    for r in range(NSA_REP):
        s = jnp.where(valid_c, s_c[r * rows:(r + 1) * rows], NEG_INF)
        e = jnp.exp2(s - jnp.max(s, axis=-1, keepdims=True))
        p = jnp.where(valid_c, e / jnp.sum(e, axis=-1, keepdims=True), 0.0)
        p_sum += p
        p_parts.append(p.astype(BF16))
    o_cmp = _dot(jnp.concatenate(p_parts, axis=0), vc2_ref[0])
    gates = jax.nn.sigmoid(misc_ref[...])
    tiles = _head_layout([gates[:, 3 * r:3 * r + 1] * o_cmp[r * rows:(r + 1) * rows] for r in range(NSA_REP)], g)
    for i, tile in enumerate(tiles):
        ocmp_ref[:, i * LANES:(i + 1) * LANES] = tile.astype(BF16)

    ov_j = lax.broadcasted_iota(I32, (LANES, LANES), 0)
    ov_i = lax.broadcasted_iota(I32, (LANES, LANES), 1)
    ov_t = ((ov_i < 4 * ov_j + 4) & (ov_i > 4 * ov_j - 2) & (ov_j < N_SEL)).astype(F32)
    imp_t = _dot_nt(ov_t, p_sum, precision=lax.Precision.HIGHEST)[:N_SEL]
    j_blk = lax.broadcasted_iota(I32, (N_SEL, rows), 0)
    cur = (t0 + lax.broadcasted_iota(I32, (N_SEL, rows), 1)) >> SEL_SHIFT
    forced = (j_blk == 0) | (j_blk == cur) | (j_blk == cur - 1)
    imp_t = jnp.where(forced, FORCE, imp_t)
    imp_t = jnp.where(j_blk <= cur, imp_t, -FORCE)
    rank = jnp.zeros((N_SEL, rows), I32)
    for jp in range(N_SEL):
        row = imp_t[jp:jp + 1, :]
        beats = (row > imp_t) | ((row == imp_t) & (j_blk > jp))
        rank += beats.astype(I32)
    drop_t = jnp.where(rank < SEL_TOPK, 0.0, NEG_INF)
    drop_t = jnp.concatenate([drop_t, jnp.zeros((LANES - N_SEL, rows), F32)], axis=0)
    drop_q = jnp.concatenate([drop_t[:, i * LANES:(i + 1) * LANES].T for i in range(rows // LANES)], axis=0)
    drop_ref[...] = pltpu.roll(drop_q, X_SEL, axis=1).astype(BF16)


def _nsa_kernel(q_ref, misc_ref, ocmp_ref, drop_ref, ksa_ref, vs_ref, kwa_ref, vw_ref, o_ref):
    for qb in range(N_QB):
        @pl.when(pl.program_id(2) == qb)
        def _(qb=qb):
            _nsa_body(qb, q_ref, misc_ref, ocmp_ref, drop_ref, ksa_ref, vs_ref, kwa_ref, vw_ref, o_ref)


def _nsa_body(qb, q_ref, misc_ref, ocmp_ref, drop_ref, ksa_ref, vs_ref, kwa_ref, vw_ref, o_ref):
    g = pl.program_id(1)
    t0 = qb * Q_BLOCK

    def dist_to(c):
        return (t0 + lax.broadcasted_iota(I32, (Q_BLOCK, KV_CHUNK), 0)
                - (c * KV_CHUNK + lax.broadcasted_iota(I32, (Q_BLOCK, KV_CHUNK), 1)))

    c_diag = qb
    dist_diag = dist_to(c_diag)
    n_slc = (c_diag + 1) * KV_CHUNK

    slc_masks = [None] * c_diag + [dist_diag >= 0]
    q_sel = _head_queries(q_ref, g, 1.0, drop_ref[...].astype(F32))

    c_first = max(c_diag - WINDOW // KV_CHUNK, 0)
    win_masks = [None] * (c_diag - c_first + 1)
    win_masks[-1] = dist_diag >= 0
    if c_diag - c_first == WINDOW // KV_CHUNK:
        win_masks[0] = dist_to(c_first) < WINDOW
    win_rows = slice(c_first * KV_CHUNK, n_slc)
    q_win = _head_queries(q_ref, g, 1.0, 0.0)

    both = lambda masks: [None if m is None else jnp.concatenate([m, m], axis=0) for m in masks]
    tasks = []
    for r in range(0, NSA_REP, 2):
        tasks.append((jnp.concatenate(q_win[r:r + 2], axis=0), kwa_ref[win_rows, :], vw_ref[win_rows, :],
                      both(win_masks)))
    split = lambda o: [o[:Q_BLOCK], o[Q_BLOCK:]]
    if c_diag < SLC_SINGLE_HEAD_FROM:
        for r in range(0, NSA_REP, 2):
            tasks.append((jnp.concatenate(q_sel[r:r + 2], axis=0), ksa_ref[:n_slc, :], vs_ref[:n_slc, :],
                          both(slc_masks)))
        outs = _attend_pipelined(tasks)
        o_slc = split(outs[2]) + split(outs[3])
    else:
        for r in range(NSA_REP):
            tasks.append((q_sel[r], ksa_ref[:n_slc, :], vs_ref[:n_slc, :], slc_masks))
        outs = _attend_pipelined(tasks)
        o_slc = outs[2:]
    o_win = split(outs[0]) + split(outs[1])

    gates = jax.nn.sigmoid(misc_ref[...])
    heads = [gates[:, 3 * r + 1:3 * r + 2] * o_slc[r] + gates[:, 3 * r + 2:3 * r + 3] * o_win[r]
             for r in range(NSA_REP)]
    for i, tile in enumerate(_head_layout(heads, g)):
        cols = slice(i * LANES, (i + 1) * LANES)
        o_ref[:, cols] = (tile + ocmp_ref[:, cols].astype(F32)).astype(BF16)


def _nsa(q, misc, kca, vc2, ksa, kwa, vsw):
    n = q.shape[0]
    b = n // SEQ
    gw = NSA_REP * HEAD_DIM
    assert WINDOW % KV_CHUNK == 0 and KV_CHUNK == Q_BLOCK and SEQ % SEL_ROWS == 0
    n_sel = SEQ // SEL_ROWS
    sel_rows = lambda width: pl.BlockSpec((SEL_ROWS, width), lambda i, g, j: (i * n_sel + j, g))
    ocmp, drop = pl.pallas_call(
        _nsa_select_kernel,
        grid=(b, NSA_GROUPS, n_sel),
        in_specs=[sel_rows(gw), sel_rows(LANES),
                  pl.BlockSpec((1, N_CMP_PAD, LANES), lambda i, g, j: (i, 0, g)),
                  pl.BlockSpec((1, N_CMP_PAD, LANES), lambda i, g, j: (i, 0, 0))],
        out_specs=[sel_rows(gw), sel_rows(LANES)],
        out_shape=[jax.ShapeDtypeStruct((n, NSA_GROUPS * gw), BF16),
                   jax.ShapeDtypeStruct((n, NSA_GROUPS * LANES), BF16)],
        compiler_params=_params(("parallel", "parallel", "parallel"), SELECT_VMEM),
        name="nsa_select",
    )(q, misc, kca, vc2)

    q_rows = lambda width: pl.BlockSpec((Q_BLOCK, width), lambda i, g, c: (i * N_QB + c, g))
    return pl.pallas_call(
        _nsa_kernel,
        grid=(b, NSA_GROUPS, N_QB),
        in_specs=[q_rows(gw), q_rows(LANES), q_rows(gw), q_rows(LANES),
                  pl.BlockSpec((SEQ, LANES), lambda i, g, c: (i, g)),
                  pl.BlockSpec((SEQ, LANES), lambda i, g, c: (i, 0)),
                  pl.BlockSpec((SEQ, LANES), lambda i, g, c: (i, g)),
                  pl.BlockSpec((SEQ, LANES), lambda i, g, c: (i, 1))],
        out_specs=q_rows(gw),
        out_shape=jax.ShapeDtypeStruct((n, NSA_GROUPS * gw), BF16),
        compiler_params=_params(("parallel", "parallel", "arbitrary"), NSA_VMEM),
        name="nsa",
    )(q, misc, ocmp, drop, ksa, vsw, kwa, vsw)


def _gla_kernel(q_ref, k_ref, v_ref, r_ref, misc_ref, wa_ref, ba_ref, gn_ref, o_ref):
    c, blk = GLA_CHUNK, GLA_BLOCK
    n_c = blk // c
    lane = lax.broadcasted_iota(I32, (blk, LANES), 1)
    row = lax.broadcasted_iota(I32, (blk, LANES), 0)
    in_chunk = row & (c - 1)
    chunk_of_row = row >> GLA_CHUNK_SHIFT
    r2 = lax.broadcasted_iota(I32, (2 * blk, blk), 0) & (blk - 1)
    c2 = lax.broadcasted_iota(I32, (2 * blk, blk), 1)
    intra = (r2 >= c2) & ((r2 >> GLA_CHUNK_SHIFT) == (c2 >> GLA_CHUNK_SHIFT))
    lane_s = lax.broadcasted_iota(I32, (GLA_DV, LANES), 1)
    n_blk = SEQ // blk

    def prep(i_blk):
        rows = slice(i_blk * blk, (i_blk + 1) * blk)
        la = _dot_3pass(misc_ref[rows, :], wa_ref[...]) + ba_ref[...]
        b = (jnp.minimum(la, 0.0) - jnp.log(1.0 + jnp.exp(-jnp.abs(la)))) * (1.0 / GLA_TAU)
        shift = 1
        while shift < c:
            b = b + jnp.where(in_chunk >= shift, pltpu.roll(b, shift, axis=0), 0.0)
            shift *= 2
        b3 = b.reshape(n_c, c, LANES)
        b_last = b3[:, c - 1:c, :]
        k = k_ref[rows, :]
        q_in = q_ref[rows, :] * (GLA_DK ** -0.5) * jnp.exp(b)
        k_in = (k * jnp.exp(-b)).astype(BF16)
        k_st = (k.reshape(n_c, c, LANES) * jnp.exp(b_last - b3)).reshape(blk, LANES)
        decay = jnp.exp(b_last)
        v = v_ref[rows, :]

        q2 = jnp.concatenate([jnp.where(lane < GLA_DK, q_in, 0.0), jnp.where(lane >= GLA_DK, q_in, 0.0)],
                             axis=0).astype(BF16)
        k_cols = jnp.concatenate([jnp.where(chunk_of_row == i_c, k_st, 0.0) for i_c in range(n_c)], axis=1)
        return q2, k_in, k_cols.astype(BF16), v, decay

    def chunk_parallel(q2, k_in, k_cols, v, decay):
        a = jnp.where(intra, _dot_nt(q2, k_in), 0.0).astype(BF16)
        o_intra = _dot(a, v)
        inc = _dot_tn(v, k_cols)
        return q2, o_intra, inc, decay

    per_blk = []
    ready = prep(0)
    for i_blk in range(n_blk):
        upcoming = prep(i_blk + 1) if i_blk + 1 < n_blk else None
        per_blk.append(chunk_parallel(*ready))
        ready = upcoming

    st = jnp.zeros((GLA_DV, LANES), F32)
    state_before = []
    for _, _, inc, decay in per_blk:
        for i_c in range(n_c):
            state_before.append(st.astype(BF16))
            cols = slice(i_c * LANES, (i_c + 1) * LANES)
            st = st * decay[i_c] + jnp.where(lane_s < GLA_DK, inc[:GLA_DV, cols], inc[GLA_DV:, cols])

    for i_blk, (q2, o_intra, _, _) in enumerate(per_blk):
        rows = slice(i_blk * blk, (i_blk + 1) * blk)
        o_inter = []
        for i_c in range(n_c):
            q_c = jnp.concatenate([q2[i_c * c:(i_c + 1) * c], q2[blk + i_c * c:blk + (i_c + 1) * c]], axis=0)
            o_inter.append(_dot_nt(q_c, state_before[i_blk * n_c + i_c]))
        for h in range(2):
            cols = slice(h * GLA_DV, (h + 1) * GLA_DV)
            o = o_intra[h * blk:(h + 1) * blk, cols] + jnp.concatenate(
                [o_inter[i_c][h * c:(h + 1) * c] for i_c in range(n_c)], axis=0)
            gate = r_ref[rows, cols]
            o_ref[rows, cols] = (_rms(o, gn_ref[...]) * (gate * jax.nn.sigmoid(gate))).astype(BF16)


def _gla(gq, gk, gv, gr, misc, wa, ba, gn):
    n = gq.shape[0]
    b = n // SEQ
    return pl.pallas_call(
        _gla_kernel,
        grid=(b, GLA_HEADS // 2),
        in_specs=[
            pl.BlockSpec((SEQ, LANES), lambda i, p: (i, p)),
            pl.BlockSpec((SEQ, LANES), lambda i, p: (i, p)),
            pl.BlockSpec((SEQ, 2 * GLA_DV), lambda i, p: (i, p)),
            pl.BlockSpec((SEQ, 2 * GLA_DV), lambda i, p: (i, p)),
            pl.BlockSpec((SEQ, LANES), lambda i, p: (i, 0)),
            pl.BlockSpec((LANES, LANES), lambda i, p: (0, p)),
            pl.BlockSpec((1, LANES), lambda i, p: (0, p)),
            pl.BlockSpec((1, GLA_DV), lambda i, p: (0, 0)),
        ],
        out_specs=pl.BlockSpec((SEQ, 2 * GLA_DV), lambda i, p: (i, p)),
        out_shape=jax.ShapeDtypeStruct((n, GLA_HEADS * GLA_DV), BF16),
        compiler_params=_params(("parallel", "parallel"), GLA_VMEM),
        name="gla",
    )(gq, gk, gv, gr, misc, wa, ba, gn)


def _compress_weights(pe, w1, w2, group_stride):
    hd, hid = HEAD_DIM, CMP_HIDDEN
    pe2 = jnp.concatenate([pe, pe], axis=1)
    w1 = w1.reshape(CMP_BLOCK, hd, hid)
    z1 = jnp.zeros_like(w1)
    w1e = jnp.concatenate([jnp.concatenate([w1, z1], axis=2), jnp.concatenate([z1, w1], axis=2)], axis=1)
    w2e = jnp.zeros((NSA_GROUPS * hid, NSA_GROUPS * group_stride), w2.dtype)
    for g in range(NSA_GROUPS):
        w2e = w2e.at[g * hid:(g + 1) * hid, g * group_stride:g * group_stride + hd].set(w2)
    return pe2, w1e.astype(BF16), w2e.astype(BF16)


def kernel(x, ffn1_norm, ffn1_w_gate, ffn1_w_up, ffn1_w_down, mix_norm, w_in, nsa_pe_k, nsa_w1_k, nsa_w2_k,
           nsa_pe_v, nsa_w1_v, nsa_w2_v, gla_w_a2, gla_b_a, gla_norm, w_out, ffn2_norm, ffn2_w_gate,
           ffn2_w_up, ffn2_w_down, final_norm):
    bsz, seq, d = x.shape
    assert (seq, d) == (SEQ, D_MODEL) and ffn1_norm.shape[0] == 1
    n = bsz * seq
    xf = x.reshape(n, d)
    row = lambda v: v.reshape(1, -1).astype(F32)
    bf = lambda w: w.astype(BF16)
    ones = jnp.ones((1, d), F32)

    x1 = _ffn(xf, (), row(ffn1_norm[0]), bf(ffn1_w_gate[0]), bf(ffn1_w_up[0]), bf(ffn1_w_down[0]), ones,
              final_norm=False)

    q, kvc, ksa, kwa, vsw, misc, gq, gk, gv, gr = _inproj(x1, row(mix_norm[0]), w_in[0])

    pek, w1k, w2k = _compress_weights(nsa_pe_k[0], nsa_w1_k[0], nsa_w2_k[0], LANES)
    pev, w1v, w2v = _compress_weights(nsa_pe_v[0], nsa_w1_v[0], nsa_w2_v[0], HEAD_DIM)
    kca, vc2 = _compress(kvc, pek, pev, w1k, w1v, w2k, w2v)

    o_nsa = _nsa(q, misc, kca, vc2, ksa, kwa, vsw)

    wa = jnp.zeros((LANES, GLA_HEADS * GLA_DK), F32).at[MISC_GA_OFF:MISC_GA_OFF + GLA_LOWRANK].set(gla_w_a2[0])
    o_gla = _gla(gq, gk, gv, gr, misc, wa, row(gla_b_a[0]), row(gla_norm[0]))

    d_nsa = NSA_HEADS * HEAD_DIM
    w_o = bf(w_out[0])
    out = _ffn(x1, (o_nsa, o_gla, w_o[:d_nsa], w_o[d_nsa:]), row(ffn2_norm[0]), bf(ffn2_w_gate[0]),
               bf(ffn2_w_up[0]), bf(ffn2_w_down[0]), row(final_norm), final_norm=True)
    return out.reshape(bsz, seq, d)
```
